```python
import jax, jax.numpy as jnp
from jax import lax
import numpy as np

D_MODEL = 2048
BATCH = 4
SEQ = 2048
DEPTH = 4
DEC_BATCH = 128
DEC_SEQ = 1
PAST_LEN = 16384
PAGE_SIZE = 128

N_BRANCH = 3
BRANCH_WIDTH = D_MODEL // 2
SGU_GROUPS = 8
SGU_CHUNK = 128
SGU_GROUP_WIDTH = BRANCH_WIDTH // SGU_GROUPS
HG_HEADS = 8
HG_DK = BRANCH_WIDTH // HG_HEADS
HG_DV = BRANCH_WIDTH // HG_HEADS
GLA_HEADS = 4
GLA_DK_TOTAL = BRANCH_WIDTH // 2
GLA_DK = GLA_DK_TOTAL // GLA_HEADS
GLA_DV = BRANCH_WIDTH // GLA_HEADS
GLA_RANK = 16
GLA_TAU = 16.0
D_FF = 4 * D_MODEL
REC_CHUNK = 64
EPS = 1e-6
IN_SPLITS = (2 * BRANCH_WIDTH, BRANCH_WIDTH, BRANCH_WIDTH, BRANCH_WIDTH, BRANCH_WIDTH, GLA_DK_TOTAL, GLA_DK_TOTAL, BRANCH_WIDTH, BRANCH_WIDTH, GLA_RANK, N_BRANCH * D_MODEL)
IN_WIDTH = sum(IN_SPLITS)

kernel_name = 'hybrid_sgu_hgrn2_gla_decoder_step'


def rmsnorm(x, g):
    xf = x.astype(jnp.float32)
    y = xf * lax.rsqrt(jnp.mean(xf * xf, axis=-1, keepdims=True) + EPS)
    return (y * g.astype(jnp.float32)).astype(x.dtype)


def layernorm(x, g, b):
    xf = x.astype(jnp.float32)
    mu = jnp.mean(xf, axis=-1, keepdims=True)
    xc = xf - mu
    y = xc * lax.rsqrt(jnp.mean(xc * xc, axis=-1, keepdims=True) + EPS)
    return (y * g.astype(jnp.float32) + b.astype(jnp.float32)).astype(x.dtype)


def head_rmsnorm(o, g):
    y = o * lax.rsqrt(jnp.mean(o * o, axis=-1, keepdims=True) + EPS)
    return y * g.astype(jnp.float32)


def gated_linear_recurrence(q, k, v, log_f, s0):
    B, T, H, _ = q.shape
    dv = v.shape[-1]
    C = min(REC_CHUNK, T)
    n = -(-T // C)
    pad = n * C - T

    def prep(a):
        a = jnp.pad(a.astype(jnp.float32), ((0, 0), (0, pad), (0, 0), (0, 0)))
        return a.reshape(B, n, C, H, a.shape[-1]).transpose(1, 0, 2, 3, 4)

    qs, ks, vs, ls = prep(q), prep(k), prep(v), prep(log_f)
    causal = jnp.tril(jnp.ones((C, C), dtype=bool))[None, :, :, None, None]

    def step(S, inp):
        qc, kc, vc, lc = inp
        b = jnp.cumsum(lc, axis=1)
        diff = b[:, :, None] - b[:, None, :]
        decay = jnp.exp(jnp.where(causal, diff, -jnp.inf))
        att = jnp.einsum('bthd,bshd,btshd->bhts', qc, kc, decay)
        o = jnp.einsum('bhts,bshv->bthv', att, vc) + jnp.einsum('bthd,bhdv->bthv', qc * jnp.exp(b), S)
        b_last = b[:, -1]
        S = jnp.exp(b_last)[..., None] * S + jnp.einsum('bshd,bshv->bhdv', kc * jnp.exp(b_last[:, None] - b), vc)
        return S, o

    S, os_ = lax.scan(step, s0.astype(jnp.float32), (qs, ks, vs, ls))
    o = os_.transpose(1, 0, 2, 3, 4).reshape(B, n * C, H, dv)[:, :T]
    return o, S


def spatial_gating(u, v, w_s, b_s):
    B, T, G, e = v.shape
    n = -(-T // SGU_CHUNK)
    pad = n * SGU_CHUNK - T
    vp = jnp.pad(v, ((0, 0), (0, pad), (0, 0), (0, 0))).reshape(B, n, SGU_CHUNK, G, e)
    mask = jnp.tril(jnp.ones((SGU_CHUNK, SGU_CHUNK), dtype=v.dtype))
    s = jnp.einsum('gnm,bcmge->bcnge', w_s * mask, vp) + b_s.T[:, :, None]
    s = s.reshape(B, n * SGU_CHUNK, G, e)[:, :T]
    return u * s


def hgrn_lower_bounds(hg_lb):
    p = jax.nn.softmax(hg_lb.astype(jnp.float32), axis=0)
    return jnp.maximum(jnp.cumsum(p, axis=0) - p[0:1], 0.0)


def trunk(x, c, s_hg, s_gla, p):
    B, T, _ = x.shape
    lb_all = hgrn_lower_bounds(p['hg_lb'])
    hg_states, gla_states, v_rows = [], [], []
    offsets = [int(o) for o in np.cumsum(IN_SPLITS)[:-1]]
    for l in range(DEPTH):
        mod = jax.nn.silu(c) @ p['w_ada'][l] + p['b_ada'][l]
        sh1, sc1, gt1, sh2, sc2, gt2 = jnp.split(mod[:, None, :], 6, axis=-1)

        h = rmsnorm(x, p['g_pre_mix'][l]) * (1.0 + sc1) + sh1
        z = h @ p['w_in'][l]
        (z_uv, hq, hf, hi, hg, gq, gk, gv, gr, glr, z_gate) = jnp.split(z, offsets, axis=-1)

        uv = jax.nn.gelu(z_uv)
        u, v = jnp.split(uv, 2, axis=-1)
        v = layernorm(v, p['sgu_ln_g'][l], p['sgu_ln_b'][l])
        v_rows.append(v)
        y_a = spatial_gating(u.reshape(B, T, SGU_GROUPS, SGU_GROUP_WIDTH), v.reshape(B, T, SGU_GROUPS, SGU_GROUP_WIDTH), p['sgu_w_s'][l], p['sgu_b_s'][l]).reshape(B, T, BRANCH_WIDTH)

        lb = lb_all[l].reshape(HG_HEADS, HG_DK)
        zf = hf.astype(jnp.float32).reshape(B, T, HG_HEADS, HG_DK)
        log_f = jnp.logaddexp(jnp.log(lb), jnp.log1p(-lb) + jax.nn.log_sigmoid(zf))
        k_hg = (1.0 - lb) * jax.nn.sigmoid(-zf)
        q_hg = jax.nn.silu(hq.astype(jnp.float32)).reshape(B, T, HG_HEADS, HG_DK)
        o_hg, S_hg = gated_linear_recurrence(q_hg, k_hg, hi.reshape(B, T, HG_HEADS, HG_DV), log_f, s_hg[l])
        y_b = (head_rmsnorm(o_hg, p['hg_norm_g'][l]).reshape(B, T, BRANCH_WIDTH) * jax.nn.silu(hg.astype(jnp.float32))).astype(x.dtype)

        log_a = jax.nn.log_sigmoid((glr @ p['gla_w_up'][l] + p['gla_b_up'][l]).astype(jnp.float32)) / GLA_TAU
        q_g = gq.reshape(B, T, GLA_HEADS, GLA_DK)
        k_g = gk.reshape(B, T, GLA_HEADS, GLA_DK).astype(jnp.float32) * (GLA_DK ** -0.5)
        o_g, S_g = gated_linear_recurrence(q_g, k_g, gv.reshape(B, T, GLA_HEADS, GLA_DV), log_a.reshape(B, T, GLA_HEADS, GLA_DK), s_gla[l])
        y_c = (head_rmsnorm(o_g, p['gla_norm_g'][l]).reshape(B, T, BRANCH_WIDTH) * jax.nn.silu(gr.astype(jnp.float32))).astype(x.dtype)

        gates = jax.nn.sigmoid(z_gate).reshape(B, T, N_BRANCH, D_MODEL)
        branches = (y_a, y_b, y_c)
        merged = gates[:, :, 0] * (branches[0] @ p['w_branch'][l, 0])
        for n_b in range(1, N_BRANCH):
            merged = merged + gates[:, :, n_b] * (branches[n_b] @ p['w_branch'][l, n_b])
        out = merged @ p['w_out'][l]
        x = x + gt1 * rmsnorm(out, p['g_post_mix'][l])

        h2 = rmsnorm(x, p['g_pre_mlp'][l]) * (1.0 + sc2) + sh2
        y2 = jnp.square(jax.nn.relu(h2 @ p['w_mlp_up'][l])) @ p['w_mlp_down'][l]
        x = x + gt2 * rmsnorm(y2, p['g_post_mlp'][l])

        hg_states.append(S_hg)
        gla_states.append(S_g)
    return x, jnp.stack(hg_states), jnp.stack(gla_states), jnp.stack(v_rows)


def setup_inputs(seed: int = 0) -> dict:
    key = jax.random.key(seed)
    ks = jax.random.split(key, 32)

    def nrm(k, shape, scale):
        return jax.random.normal(k, shape, jnp.float32) * scale

    def gain(k, shape):
        return 1.0 + 0.1 * jax.random.normal(k, shape, jnp.float32)

    return {
        'x_prompt': nrm(ks[0], (BATCH, SEQ, D_MODEL), 1.0),
        'x_sample': nrm(ks[1], (DEC_BATCH, DEC_SEQ, D_MODEL), 1.0),
        'state_hgrn': nrm(ks[2], (DEPTH, DEC_BATCH, HG_HEADS, HG_DK, HG_DV), 0.5),
        'state_gla': nrm(ks[3], (DEPTH, DEC_BATCH, GLA_HEADS, GLA_DK, GLA_DV), 0.5),
        'c_prompt': nrm(ks[4], (BATCH, D_MODEL), 1.0),
        'c_sample': nrm(ks[5], (DEC_BATCH, D_MODEL), 1.0),
        'w_ada': nrm(ks[6], (DEPTH, D_MODEL, 6 * D_MODEL), 0.5 * D_MODEL ** -0.5),
        'b_ada': nrm(ks[7], (DEPTH, 6 * D_MODEL), 0.01),
        'g_pre_mix': gain(ks[8], (DEPTH, D_MODEL)),
        'g_post_mix': gain(ks[9], (DEPTH, D_MODEL)),
        'g_pre_mlp': gain(ks[10], (DEPTH, D_MODEL)),
        'g_post_mlp': gain(ks[11], (DEPTH, D_MODEL)),
        'w_in': nrm(ks[12], (DEPTH, D_MODEL, IN_WIDTH), D_MODEL ** -0.5),
        'sgu_ln_g': gain(ks[13], (DEPTH, BRANCH_WIDTH)),
        'sgu_ln_b': nrm(ks[14], (DEPTH, BRANCH_WIDTH), 0.01),
        'sgu_w_s': nrm(ks[15], (DEPTH, SGU_GROUPS, SGU_CHUNK, SGU_CHUNK), SGU_CHUNK ** -0.5),
        'sgu_b_s': gain(ks[16], (DEPTH, SGU_GROUPS, SGU_CHUNK)),
        'hg_lb': nrm(ks[17], (DEPTH, BRANCH_WIDTH), 1.0),
        'hg_norm_g': gain(ks[18], (DEPTH, HG_DV)),
        'gla_w_up': nrm(ks[19], (DEPTH, GLA_RANK, GLA_DK_TOTAL), GLA_RANK ** -0.5),
        'gla_b_up': nrm(ks[20], (DEPTH, GLA_DK_TOTAL), 0.1),
        'gla_norm_g': gain(ks[21], (DEPTH, GLA_DV)),
        'w_branch': nrm(ks[22], (DEPTH, N_BRANCH, BRANCH_WIDTH, D_MODEL), BRANCH_WIDTH ** -0.5),
        'w_out': nrm(ks[23], (DEPTH, D_MODEL, D_MODEL), D_MODEL ** -0.5),
        'w_mlp_up': nrm(ks[24], (DEPTH, D_MODEL, D_FF), D_MODEL ** -0.5),
        'w_mlp_down': nrm(ks[25], (DEPTH, D_FF, D_MODEL), D_FF ** -0.5),
    }


def reference(x_prompt, x_sample, state_hgrn, state_gla, c_prompt, c_sample, w_ada, b_ada, g_pre_mix, g_post_mix, g_pre_mlp, g_post_mlp, w_in, sgu_ln_g, sgu_ln_b, sgu_w_s, sgu_b_s, hg_lb, hg_norm_g, gla_w_up, gla_b_up, gla_norm_g, w_branch, w_out, w_mlp_up, w_mlp_down):
    params = {
        'w_ada': w_ada, 'b_ada': b_ada,
        'g_pre_mix': g_pre_mix, 'g_post_mix': g_post_mix,
        'g_pre_mlp': g_pre_mlp, 'g_post_mlp': g_post_mlp,
        'w_in': w_in, 'sgu_ln_g': sgu_ln_g, 'sgu_ln_b': sgu_ln_b,
        'sgu_w_s': sgu_w_s, 'sgu_b_s': sgu_b_s,
        'hg_lb': hg_lb, 'hg_norm_g': hg_norm_g,
        'gla_w_up': gla_w_up, 'gla_b_up': gla_b_up, 'gla_norm_g': gla_norm_g,
        'w_branch': w_branch, 'w_out': w_out,
        'w_mlp_up': w_mlp_up, 'w_mlp_down': w_mlp_down,
    }
    zero_hg = jnp.zeros((DEPTH, BATCH, HG_HEADS, HG_DK, HG_DV), jnp.float32)
    zero_gla = jnp.zeros((DEPTH, BATCH, GLA_HEADS, GLA_DK, GLA_DV), jnp.float32)
    y_prompt, hgrn_prompt, gla_prompt, _ = trunk(x_prompt, c_prompt, zero_hg, zero_gla, params)
    y_sample, hgrn_sample, gla_sample, sgu_v_sample = trunk(x_sample, c_sample, state_hgrn, state_gla, params)
    return (y_prompt, y_sample, hgrn_prompt, gla_prompt, hgrn_sample, gla_sample, sgu_v_sample)
```

```python
import functools

import jax
import jax.numpy as jnp
from jax import lax
from jax.experimental import pallas as pl
from jax.experimental.pallas import tpu as pltpu

F32 = jnp.float32
BF16 = jnp.bfloat16

D_MODEL = 2048
BATCH = 4
SEQ = 2048
DEPTH = 4
DEC_BATCH = 128
BRANCH_WIDTH = D_MODEL // 2
N_BRANCH = 3
SGU_GROUPS = 8
SGU_CHUNK = 128
HG_HEADS = 8
HG_DK = BRANCH_WIDTH // HG_HEADS
HG_DV = BRANCH_WIDTH // HG_HEADS
GLA_HEADS = 4
GLA_DK_TOTAL = BRANCH_WIDTH // 2
GLA_DK = GLA_DK_TOTAL // GLA_HEADS
GLA_DV = BRANCH_WIDTH // GLA_HEADS
GLA_RANK = 16
GLA_TAU = 16.0
D_FF = 4 * D_MODEL
EPS = 1e-6

N_PROMPT = BATCH * SEQ
N_ALL = N_PROMPT + DEC_BATCH
MOD_ROWS = DEC_BATCH + 8

OFF_GLR = 2 * BRANCH_WIDTH + 4 * BRANCH_WIDTH + 2 * GLA_DK_TOTAL + 2 * BRANCH_WIDTH
OFF_GATE = OFF_GLR + GLA_RANK
Z_MAIN = OFF_GLR

LANES = 128
ROW_TILE = 128
MM_TM = 1040
REC_CHUNK = 128
HALF = REC_CHUNK // 2
SAMPLE_NB = 8
VMEM_LIMIT = 56 * 1024 * 1024


def _cparams(sem):
    return pltpu.CompilerParams(dimension_semantics=sem, vmem_limit_bytes=VMEM_LIMIT)


def _sigmoid(x):
    return jax.nn.sigmoid(x)


def _silu(x):
    return x * jax.nn.sigmoid(x)


def _log_sigmoid(x):
    return jnp.minimum(x, 0.0) - jnp.log1p(jnp.exp(-jnp.abs(x)))


def _rms(x, g):
    return x * lax.rsqrt(jnp.mean(x * x, axis=-1, keepdims=True) + EPS) * g


def _mod_kernel(c_ref, w_ref, b_ref, o_ref):
    a = _silu(c_ref[...]).astype(BF16)
    o_ref[...] = jnp.dot(a, w_ref[...].astype(BF16), preferred_element_type=F32) + b_ref[...]


def _modulation(c_all, w_ada, b_ada):
    tn = 1024
    n = 6 * D_MODEL
    return pl.pallas_call(
        _mod_kernel,
        grid=(DEPTH, n // tn),
        in_specs=[
            pl.BlockSpec((MOD_ROWS, D_MODEL), lambda l, j: (0, 0)),
            pl.BlockSpec((None, D_MODEL, tn), lambda l, j: (l, 0, j)),
            pl.BlockSpec((None, 1, tn), lambda l, j: (l, 0, j)),
        ],
        out_specs=pl.BlockSpec((None, MOD_ROWS, tn), lambda l, j: (l, 0, j)),
        out_shape=jax.ShapeDtypeStruct((DEPTH, MOD_ROWS, n), F32),
        compiler_params=_cparams(("parallel", "parallel")),
        name="adaln_mod",
    )(c_all, w_ada, b_ada.reshape(DEPTH, 1, n))


def _mod_specs(l, j):
    return [
        pl.BlockSpec((None, DEC_BATCH, D_MODEL), lambda i: (l, 0, j)),
        pl.BlockSpec((None, 8, D_MODEL), lambda i: (l, DEC_BATCH // 8, j)),
    ]


def _pick_mod(s_ref, p_ref):
    i = pl.program_id(0)
    tiles_per_seq = SEQ // ROW_TILE
    b = jnp.minimum(i // tiles_per_seq, BATCH - 1)
    is_sample = i >= N_PROMPT // ROW_TILE
    return jnp.where(is_sample, s_ref[...], p_ref[pl.ds(b, 1), :])


def _prenorm_kernel(x_ref, g_ref, sc_s, sc_p, sh_s, sh_p, h_ref):
    h = _rms(x_ref[...], g_ref[...]) * (1.0 + _pick_mod(sc_s, sc_p)) + _pick_mod(sh_s, sh_p)
    h_ref[...] = h.astype(BF16)


def _prenorm(x, g, mod, l, j_shift, j_scale):
    row = pl.BlockSpec((ROW_TILE, D_MODEL), lambda i: (i, 0))
    vec = pl.BlockSpec((1, D_MODEL), lambda i: (0, 0))
    return pl.pallas_call(
        _prenorm_kernel,
        grid=(N_ALL // ROW_TILE,),
        in_specs=[row, vec] + _mod_specs(l, j_scale) + _mod_specs(l, j_shift),
        out_specs=row,
        out_shape=jax.ShapeDtypeStruct((N_ALL, D_MODEL), BF16),
        compiler_params=_cparams(("parallel",)),
        name="prenorm",
    )(x, g.reshape(1, D_MODEL), mod, mod, mod, mod)


def _resid_kernel(x_ref, y_ref, gpost_ref, gt_s, gt_p, *rest, with_next):
    x_new = x_ref[...] + _pick_mod(gt_s, gt_p) * _rms(y_ref[...], gpost_ref[...])
    if with_next:
        gpre_ref, sc_s, sc_p, sh_s, sh_p, xo_ref, h_ref = rest
        xo_ref[...] = x_new
        h = _rms(x_new, gpre_ref[...]) * (1.0 + _pick_mod(sc_s, sc_p)) + _pick_mod(sh_s, sh_p)
        h_ref[...] = h.astype(BF16)
    else:
        (xo_ref,) = rest
        xo_ref[...] = x_new


def _resid(x, y, g_post, mod, l_gate, j_gate, nxt=None):
    row = pl.BlockSpec((ROW_TILE, D_MODEL), lambda i: (i, 0))
    vec = pl.BlockSpec((1, D_MODEL), lambda i: (0, 0))
    in_specs = [row, row, vec] + _mod_specs(l_gate, j_gate)
    args = [x, y, g_post.reshape(1, D_MODEL), mod, mod]
    out_specs = [row]
    out_shape = [jax.ShapeDtypeStruct((N_ALL, D_MODEL), F32)]
    if nxt is not None:
        g_pre, l_n, j_shift, j_scale = nxt
        in_specs += [vec] + _mod_specs(l_n, j_scale) + _mod_specs(l_n, j_shift)
        args += [g_pre.reshape(1, D_MODEL), mod, mod, mod, mod]
        out_specs.append(row)
        out_shape.append(jax.ShapeDtypeStruct((N_ALL, D_MODEL), BF16))
    return pl.pallas_call(
        functools.partial(_resid_kernel, with_next=nxt is not None),
        grid=(N_ALL // ROW_TILE,),
        in_specs=in_specs,
        out_specs=out_specs,
        out_shape=out_shape,
        compiler_params=_cparams(("parallel",)),
        name="resid_norm",
    )(*args)


def _act_none(x):
    return x


def _act_relu2(x):
    return jnp.square(jnp.maximum(x, 0.0))


def _mm_kernel(a_ref, w_ref, o_ref, *scratch, nk, act):
    part = jnp.dot(a_ref[...], w_ref[...].astype(BF16), preferred_element_type=F32)
    if nk == 1:
        o_ref[...] = act(part).astype(o_ref.dtype)
        return
    (acc_ref,) = scratch
    k = pl.program_id(2)

    @pl.when(k == 0)
    def _():
        acc_ref[...] = part

    @pl.when(k > 0)
    def _():
        acc_ref[...] += part

    @pl.when(k == nk - 1)
    def _():
        o_ref[...] = act(acc_ref[...]).astype(o_ref.dtype)


def _matmul(a, w, l, *, n_out, col_off=0, tn=512, tk=None, act=_act_none, out_dtype=F32, name):
    k_dim = a.shape[1]
    tk = k_dim if tk is None else tk
    nk = k_dim // tk
    cb = col_off // tn
    return pl.pallas_call(
        functools.partial(_mm_kernel, nk=nk, act=act),
        grid=(N_ALL // MM_TM, n_out // tn, nk),
        in_specs=[
            pl.BlockSpec((MM_TM, tk), lambda i, j, k: (i, k)),
            pl.BlockSpec((None, tk, tn), lambda i, j, k: (l, k, cb + j)),
        ],
        out_specs=pl.BlockSpec((MM_TM, tn), lambda i, j, k: (i, j)),
        out_shape=jax.ShapeDtypeStruct((N_ALL, n_out), out_dtype),
        scratch_shapes=[] if nk == 1 else [pltpu.VMEM((MM_TM, tn), F32)],
        compiler_params=_cparams(("parallel", "parallel", "arbitrary")),
        name=name,
    )(a, w)


def _merge_kernel(ya_ref, yb_ref, yc_ref, wa_ref, wb_ref, wc_ref, ga_ref, gb_ref, gc_ref, o_ref):
    def branch(y_ref, w_ref, g_ref):
        p = jnp.dot(y_ref[...], w_ref[...].astype(BF16), preferred_element_type=F32)
        return _sigmoid(g_ref[...]) * p

    merged = branch(ya_ref, wa_ref, ga_ref)
    merged = merged + branch(yb_ref, wb_ref, gb_ref)
    merged = merged + branch(yc_ref, wc_ref, gc_ref)
    o_ref[...] = merged.astype(BF16)


def _merge(ya, yb, yc, w_branch, z_gate, l):
    tn = 256
    nb = D_MODEL // tn
    y_spec = pl.BlockSpec((MM_TM, BRANCH_WIDTH), lambda i, j: (i, 0))

    def w_spec(b):
        return pl.BlockSpec((None, None, BRANCH_WIDTH, tn), lambda i, j: (l, b, 0, j))

    def g_spec(b):
        return pl.BlockSpec((MM_TM, tn), lambda i, j: (i, b * nb + j))

    return pl.pallas_call(
        _merge_kernel,
        grid=(N_ALL // MM_TM, nb),
        in_specs=[y_spec, y_spec, y_spec, w_spec(0), w_spec(1), w_spec(2),
                  g_spec(0), g_spec(1), g_spec(2)],
        out_specs=pl.BlockSpec((MM_TM, tn), lambda i, j: (i, j)),
        out_shape=jax.ShapeDtypeStruct((N_ALL, D_MODEL), BF16),
        compiler_params=_cparams(("parallel", "parallel")),
        name="branch_merge",
    )(ya, yb, yc, w_branch, w_branch, w_branch, z_gate, z_gate, z_gate)


def _sgu_kernel(zu_ref, zv_ref, lg_ref, lb_ref, ws_ref, bst_ref, y_ref, vs_ref):
    i = pl.program_id(0)
    u = jax.nn.gelu(zu_ref[...], approximate=True)
    v = jax.nn.gelu(zv_ref[...], approximate=True)
    mu = jnp.mean(v, axis=-1, keepdims=True)
    vc = v - mu
    v = vc * lax.rsqrt(jnp.mean(vc * vc, axis=-1, keepdims=True) + EPS) * lg_ref[...] + lb_ref[...]
    is_sample = i == N_PROMPT // ROW_TILE

    @pl.when(jnp.logical_not(is_sample))
    def _():
        n_idx = lax.broadcasted_iota(jnp.int32, (SGU_CHUNK, SGU_CHUNK), 0)
        m_idx = lax.broadcasted_iota(jnp.int32, (SGU_CHUNK, SGU_CHUNK), 1)
        causal = m_idx <= n_idx
        for g in range(SGU_GROUPS):
            sl = slice(g * LANES, (g + 1) * LANES)
            wm = jnp.where(causal, ws_ref[g], 0.0).astype(BF16)
            s = jnp.dot(wm, v[:, sl].astype(BF16), preferred_element_type=F32) + bst_ref[:, g:g + 1]
            y_ref[:, sl] = (u[:, sl] * s).astype(BF16)

    @pl.when(is_sample)
    def _():
        vs_ref[...] = v
        for g in range(SGU_GROUPS):
            sl = slice(g * LANES, (g + 1) * LANES)
            s = v[:, sl] * ws_ref[g][0:1, 0:1] + bst_ref[0:1, g:g + 1]
            y_ref[:, sl] = (u[:, sl] * s).astype(BF16)


def _sgu(z, ln_g, ln_b, w_s, b_s_t, l):
    vec = pl.BlockSpec((1, BRANCH_WIDTH), lambda i: (0, 0))
    return pl.pallas_call(
        _sgu_kernel,
        grid=(N_ALL // ROW_TILE,),
        in_specs=[
            pl.BlockSpec((ROW_TILE, BRANCH_WIDTH), lambda i: (i, 0)),
            pl.BlockSpec((ROW_TILE, BRANCH_WIDTH), lambda i: (i, 1)),
            vec, vec,
            pl.BlockSpec((None, SGU_GROUPS, SGU_CHUNK, SGU_CHUNK), lambda i: (l, 0, 0, 0)),
            pl.BlockSpec((None, SGU_CHUNK, SGU_GROUPS), lambda i: (l, 0, 0)),
        ],
        out_specs=[
            pl.BlockSpec((ROW_TILE, BRANCH_WIDTH), lambda i: (i, 0)),
            pl.BlockSpec((DEC_BATCH, BRANCH_WIDTH), lambda i: (0, 0)),
        ],
        out_shape=[
            jax.ShapeDtypeStruct((N_ALL, BRANCH_WIDTH), BF16),
            jax.ShapeDtypeStruct((DEC_BATCH, BRANCH_WIDTH), F32),
        ],
        compiler_params=_cparams(("arbitrary",)),
        name="sgu",
    )(z, z, ln_g[l].reshape(1, BRANCH_WIDTH), ln_b[l].reshape(1, BRANCH_WIDTH), w_s, b_s_t)


def _hgrn_lower_bound(lb_ref, l):
    raw = lb_ref[...]
    e = jnp.exp(raw - jnp.max(raw, axis=0, keepdims=True))
    p = e / jnp.sum(e, axis=0, keepdims=True)
    acc = p[0:1, :]
    for j in range(1, l + 1):
        acc = acc + p[j:j + 1, :]
    return jnp.maximum(acc - p[0:1, :], 0.0)


def _hgrn_prep(zq, zf, lb):
    a = jnp.log(lb)
    c = jnp.log1p(-lb) + _log_sigmoid(zf)
    log_f = jnp.maximum(a, c) + jnp.log1p(jnp.exp(-jnp.abs(a - c)))
    k = (1.0 - lb) * _sigmoid(-zf)
    return _silu(zq), k, log_f


def _gla_prep(zq, zk, glr, wup_ref, bup_ref):
    x = jnp.dot(glr[:, :GLA_RANK].astype(BF16), wup_ref[...].astype(BF16),
                preferred_element_type=F32) + bup_ref[...]
    log_a = _log_sigmoid(x) / GLA_TAU
    return zq, zk * (GLA_DK ** -0.5), log_a


def _lane_bcast_col(row):
    return jnp.broadcast_to(row, (LANES, LANES)).T


def _nt(a, b):
    return lax.dot_general(a, b, (((1,), (1,)), ((), ())), preferred_element_type=F32)


def _head_out(o, g_row, gate):
    o = o * lax.rsqrt(jnp.mean(o * o, axis=-1, keepdims=True) + EPS) * g_row
    return o * _silu(gate)


def _prompt_chunk(q, k, v, log_f, gate, gnorm_ref, s_ref, y_ref, so_ref, *, heads, dk, dv):
    c = pl.program_id(1)
    n_chunks = pl.num_programs(1)

    @pl.when(c == 0)
    def _():
        s_ref[...] = jnp.zeros_like(s_ref)

    t_idx = lax.broadcasted_iota(jnp.int32, (REC_CHUNK, REC_CHUNK), 0)
    s_idx = lax.broadcasted_iota(jnp.int32, (REC_CHUNK, REC_CHUNK), 1)
    tril = jnp.where(s_idx <= t_idx, 1.0, 0.0).astype(F32)
    b_all = jnp.dot(tril, log_f, precision=lax.Precision.HIGHEST, preferred_element_type=F32)
    causal_half = (lax.broadcasted_iota(jnp.int32, (HALF, HALF), 1)
                   <= lax.broadcasted_iota(jnp.int32, (HALF, HALF), 0))

    for h in range(heads):
        ks = slice(h * dk, (h + 1) * dk)
        vs = slice(h * dv, (h + 1) * dv)
        qh, kh, bh = q[:, ks], k[:, ks], b_all[:, ks]
        vh = v[:, vs].astype(BF16)
        q_a, q_b = qh[:HALF], qh[HALF:]
        k_a, k_b = kh[:HALF], kh[HALF:]
        b_a, b_b = bh[:HALF], bh[HALF:]
        v_a, v_b = vh[:HALF], vh[HALF:]
        r_a = b_a[HALF // 2 - 1:HALF // 2]
        r_b = b_b[HALF // 2 - 1:HALF // 2]
        r_m = b_a[HALF - 1:HALF]
        att_aa = _nt((q_a * jnp.exp(b_a - r_a)).astype(BF16), (k_a * jnp.exp(r_a - b_a)).astype(BF16))
        att_bb = _nt((q_b * jnp.exp(b_b - r_b)).astype(BF16), (k_b * jnp.exp(r_b - b_b)).astype(BF16))
        att_ba = _nt((q_b * jnp.exp(b_b - r_m)).astype(BF16), (k_a * jnp.exp(r_m - b_a)).astype(BF16))
        att_aa = jnp.where(causal_half, att_aa, 0.0).astype(BF16)
        att_bb = jnp.where(causal_half, att_bb, 0.0).astype(BF16)
        o_a = jnp.dot(att_aa, v_a, preferred_element_type=F32)
        o_b = (jnp.dot(att_ba.astype(BF16), v_a, preferred_element_type=F32)
               + jnp.dot(att_bb, v_b, preferred_element_type=F32))
        s_old = s_ref[h]
        o = jnp.concatenate([o_a, o_b], axis=0) + jnp.dot(
            (qh * jnp.exp(bh)).astype(BF16), s_old.astype(BF16), preferred_element_type=F32)
        y_ref[:, vs] = _head_out(o, gnorm_ref[...], gate[:, vs]).astype(BF16)

        b_last = bh[REC_CHUNK - 1:REC_CHUNK]
        k_dec_t = (kh * jnp.exp(b_last - bh)).T.astype(BF16)
        upd = jnp.dot(k_dec_t, vh, preferred_element_type=F32)
        decay = _lane_bcast_col(jnp.exp(b_last))
        for j in range(dv // LANES):
            ls = slice(j * LANES, (j + 1) * LANES)
            s_ref[h, :, ls] = decay * s_old[:, ls] + upd[:, ls]

    @pl.when(c == n_chunks - 1)
    def _():
        so_ref[...] = s_ref[...]


def _hgrn_prompt_kernel(zq_ref, zf_ref, zi_ref, zg_ref, lb_ref, gnorm_ref, y_ref, so_ref, s_ref, *, l):
    lb = _hgrn_lower_bound(lb_ref, l)
    q, k, log_f = _hgrn_prep(zq_ref[...], zf_ref[...], lb)
    _prompt_chunk(q, k, zi_ref[...], log_f, zg_ref[...], gnorm_ref, s_ref, y_ref, so_ref,
                  heads=HG_HEADS, dk=HG_DK, dv=HG_DV)


def _gla_prompt_kernel(zq_ref, zk_ref, zv_ref, zr_ref, glr_ref, wup_ref, bup_ref, gnorm_ref,
                       y_ref, so_ref, s_ref):
    q, k, log_a = _gla_prep(zq_ref[...], zk_ref[...], glr_ref[...], wup_ref, bup_ref)
    _prompt_chunk(q, k, zv_ref[...], log_a, zr_ref[...], gnorm_ref, s_ref, y_ref, so_ref,
                  heads=GLA_HEADS, dk=GLA_DK, dv=GLA_DV)


def _zcol(width, block):
    n_c = SEQ // REC_CHUNK
    return pl.BlockSpec((REC_CHUNK, width), lambda b, c: (b * n_c + c, block))


def _rec_prompt_call(kernel, in_specs, args, heads, dk, dv, name):
    n_c = SEQ // REC_CHUNK
    return pl.pallas_call(
        kernel,
        grid=(BATCH, n_c),
        in_specs=in_specs,
        out_specs=[
            pl.BlockSpec((REC_CHUNK, BRANCH_WIDTH), lambda b, c: (b * n_c + c, 0)),
            pl.BlockSpec((None, heads, dk, dv), lambda b, c: (b, 0, 0, 0)),
        ],
        out_shape=[
            jax.ShapeDtypeStruct((N_PROMPT, BRANCH_WIDTH), BF16),
            jax.ShapeDtypeStruct((BATCH, heads, dk, dv), F32),
        ],
        scratch_shapes=[pltpu.VMEM((heads, dk, dv), F32)],
        compiler_params=_cparams(("parallel", "arbitrary")),
        name=name,
    )(*args)


def _hgrn_prompt(z, hg_lb, g_norm, l):
    w = BRANCH_WIDTH
    in_specs = [_zcol(w, 2), _zcol(w, 3), _zcol(w, 4), _zcol(w, 5),
                pl.BlockSpec((DEPTH, w), lambda b, c: (0, 0)),
                pl.BlockSpec((1, HG_DV), lambda b, c: (0, 0))]
    args = [z, z, z, z, hg_lb, g_norm[l].reshape(1, HG_DV)]
    return _rec_prompt_call(functools.partial(_hgrn_prompt_kernel, l=l), in_specs, args,
                            HG_HEADS, HG_DK, HG_DV, "hgrn_prompt")


def _gla_prompt(z, z_glr, w_up, b_up, g_norm, l):
    w = BRANCH_WIDTH
    kw = GLA_DK_TOTAL
    in_specs = [_zcol(kw, 6 * w // kw), _zcol(kw, 6 * w // kw + 1), _zcol(w, 7), _zcol(w, 8),
                _zcol(LANES, 0),
                pl.BlockSpec((None, GLA_RANK, kw), lambda b, c: (l, 0, 0)),
                pl.BlockSpec((1, kw), lambda b, c: (0, 0)),
                pl.BlockSpec((1, GLA_DV), lambda b, c: (0, 0))]
    args = [z, z, z, z, z_glr, w_up, b_up[l].reshape(1, kw), g_norm[l].reshape(1, GLA_DV)]
    return _rec_prompt_call(_gla_prompt_kernel, in_specs, args,
                            GLA_HEADS, GLA_DK, GLA_DV, "gla_prompt")


def _sample_step(q, k, v, log_f, gate, gnorm_ref, s_ref, so_ref, i, *, heads, dk, dv):
    outs = []
    for h in range(heads):
        ks = slice(h * dk, (h + 1) * dk)
        f_col = _lane_bcast_col(jnp.exp(log_f[:, ks]))
        k_col = _lane_bcast_col(k[:, ks])
        q16 = jnp.broadcast_to(q[:, ks], (16, dk)).astype(BF16)
        parts = []
        for j in range(dv // LANES):
            ls = slice(j * LANES, (j + 1) * LANES)
            v_row = v[:, h * dv + j * LANES:h * dv + (j + 1) * LANES]
            s_new = f_col * s_ref[i, h, :, ls] + k_col * v_row
            so_ref[i, h, :, ls] = s_new
            parts.append(jnp.dot(q16, s_new.astype(BF16), preferred_element_type=F32)[0:1])
        o = parts[0] if len(parts) == 1 else jnp.concatenate(parts, axis=1)
        outs.append(_head_out(o, gnorm_ref[...], gate[:, h * dv:(h + 1) * dv]))
    return jnp.concatenate(outs, axis=1)


def _hgrn_sample_kernel(zq_ref, zf_ref, zi_ref, zg_ref, lb_ref, gnorm_ref, s_ref, y_ref, so_ref, *, l):
    lb = _hgrn_lower_bound(lb_ref, l)

    def body(i, carry):
        row = pl.ds(i, 1)
        q, k, log_f = _hgrn_prep(zq_ref[row, :], zf_ref[row, :], lb)
        y_ref[row, :] = _sample_step(q, k, zi_ref[row, :], log_f, zg_ref[row, :], gnorm_ref,
                                     s_ref, so_ref, i, heads=HG_HEADS, dk=HG_DK, dv=HG_DV)
        return carry

    lax.fori_loop(0, SAMPLE_NB, body, 0)


def _gla_sample_kernel(zq_ref, zk_ref, zv_ref, zr_ref, glr_ref, wup_ref, bup_ref, gnorm_ref, s_ref,
                       y_ref, so_ref):
    def body(i, carry):
        row = pl.ds(i, 1)
        glr = jnp.broadcast_to(glr_ref[row, :], (16, LANES))
        q, k, log_a = _gla_prep(zq_ref[row, :], zk_ref[row, :], glr, wup_ref, bup_ref)
        y_ref[row, :] = _sample_step(q, k, zv_ref[row, :], log_a[0:1], zr_ref[row, :],
                                     gnorm_ref, s_ref, so_ref, i,
                                     heads=GLA_HEADS, dk=GLA_DK, dv=GLA_DV)
        return carry

    lax.fori_loop(0, SAMPLE_NB, body, 0)


def _zrow(width, block):
    return pl.BlockSpec((SAMPLE_NB, width), lambda i: (N_PROMPT // SAMPLE_NB + i, block))


def _rec_sample_call(kernel, in_specs, args, state, l, heads, dk, dv, name):
    st = pl.BlockSpec((None, SAMPLE_NB, heads, dk, dv), lambda i: (l, i, 0, 0, 0))
    return pl.pallas_call(
        kernel,
        grid=(DEC_BATCH // SAMPLE_NB,),
        in_specs=in_specs + [st],
        out_specs=[
            pl.BlockSpec((SAMPLE_NB, BRANCH_WIDTH), lambda i: (i, 0)),
            pl.BlockSpec((SAMPLE_NB, heads, dk, dv), lambda i: (i, 0, 0, 0)),
        ],
        out_shape=[
            jax.ShapeDtypeStruct((DEC_BATCH, BRANCH_WIDTH), F32),
            jax.ShapeDtypeStruct((DEC_BATCH, heads, dk, dv), F32),
        ],
        compiler_params=_cparams(("parallel",)),
        name=name,
    )(*args, state)


def _hgrn_sample(z, hg_lb, g_norm, state, l):
    w = BRANCH_WIDTH
    in_specs = [_zrow(w, 2), _zrow(w, 3), _zrow(w, 4), _zrow(w, 5),
                pl.BlockSpec((DEPTH, w), lambda i: (0, 0)),
                pl.BlockSpec((1, HG_DV), lambda i: (0, 0))]
    args = [z, z, z, z, hg_lb, g_norm[l].reshape(1, HG_DV)]
    return _rec_sample_call(functools.partial(_hgrn_sample_kernel, l=l), in_specs, args, state, l,
                            HG_HEADS, HG_DK, HG_DV, "hgrn_sample")


def _gla_sample(z, z_glr, w_up, b_up, g_norm, state, l):
    w = BRANCH_WIDTH
    kw = GLA_DK_TOTAL
    in_specs = [_zrow(kw, 6 * w // kw), _zrow(kw, 6 * w // kw + 1), _zrow(w, 7), _zrow(w, 8),
                _zrow(LANES, 0),
                pl.BlockSpec((None, GLA_RANK, kw), lambda i: (l, 0, 0)),
                pl.BlockSpec((1, kw), lambda i: (0, 0)),
                pl.BlockSpec((1, GLA_DV), lambda i: (0, 0))]
    args = [z, z, z, z, z_glr, w_up, b_up[l].reshape(1, kw), g_norm[l].reshape(1, GLA_DV)]
    return _rec_sample_call(_gla_sample_kernel, in_specs, args, state, l,
                            GLA_HEADS, GLA_DK, GLA_DV, "gla_sample")


def kernel(x_prompt, x_sample, state_hgrn, state_gla, c_prompt, c_sample, w_ada, b_ada, g_pre_mix, g_post_mix, g_pre_mlp, g_post_mlp, w_in, sgu_ln_g, sgu_ln_b, sgu_w_s, sgu_b_s, hg_lb, hg_norm_g, gla_w_up, gla_b_up, gla_norm_g, w_branch, w_out, w_mlp_up, w_mlp_down):
    x = jnp.concatenate([x_prompt.reshape(N_PROMPT, D_MODEL), x_sample.reshape(DEC_BATCH, D_MODEL)], axis=0)
    c_all = jnp.concatenate([c_sample, c_prompt, jnp.zeros((MOD_ROWS - DEC_BATCH - BATCH, D_MODEL), F32)], axis=0)
    mod = _modulation(c_all, w_ada, b_ada)
    w_gate = w_in[:, :, OFF_GATE:]
    b_s_t = jnp.swapaxes(sgu_b_s, 1, 2)

    hg_p, gla_p, hg_s, gla_s, v_rows = [], [], [], [], []
    h = _prenorm(x, g_pre_mix[0], mod, 0, 0, 1)
    for l in range(DEPTH):
        z = _matmul(h, w_in, l, n_out=Z_MAIN, name="in_proj")
        z_glr = _matmul(h, w_in, l, n_out=LANES, col_off=OFF_GLR, tn=LANES, name="in_proj_lowrank")
        z_gate = _matmul(h, w_gate, l, n_out=N_BRANCH * D_MODEL, name="in_proj_gate")

        y_a, v_s = _sgu(z, sgu_ln_g, sgu_ln_b, sgu_w_s, b_s_t, l)
        yb_p, s_hg_p = _hgrn_prompt(z, hg_lb, hg_norm_g, l)
        yb_s, s_hg_s = _hgrn_sample(z, hg_lb, hg_norm_g, state_hgrn, l)
        yc_p, s_gla_p = _gla_prompt(z, z_glr, gla_w_up, gla_b_up, gla_norm_g, l)
        yc_s, s_gla_s = _gla_sample(z, z_glr, gla_w_up, gla_b_up, gla_norm_g, state_gla, l)
        y_b = jnp.concatenate([yb_p, yb_s.astype(BF16)], axis=0)
        y_c = jnp.concatenate([yc_p, yc_s.astype(BF16)], axis=0)

        merged = _merge(y_a, y_b, y_c, w_branch, z_gate, l)
        out = _matmul(merged, w_out, l, n_out=D_MODEL, name="out_proj")
        x, h2 = _resid(x, out, g_post_mix[l], mod, l, 2, nxt=(g_pre_mlp[l], l, 3, 4))

        up = _matmul(h2, w_mlp_up, l, n_out=D_FF, act=_act_relu2, out_dtype=BF16, name="mlp_up")
        y2 = _matmul(up, w_mlp_down, l, n_out=D_MODEL, tk=2048, name="mlp_down")
        if l + 1 < DEPTH:
            x, h = _resid(x, y2, g_post_mlp[l], mod, l, 5, nxt=(g_pre_mix[l + 1], l + 1, 0, 1))
        else:
            (x,) = _resid(x, y2, g_post_mlp[l], mod, l, 5)

        hg_p.append(s_hg_p)
        gla_p.append(s_gla_p)
        hg_s.append(s_hg_s)
        gla_s.append(s_gla_s)
        v_rows.append(v_s.reshape(DEC_BATCH, 1, BRANCH_WIDTH))

    y_prompt = x[:N_PROMPT].reshape(BATCH, SEQ, D_MODEL)
    y_sample = x[N_PROMPT:].reshape(DEC_BATCH, 1, D_MODEL)
    return (y_prompt, y_sample, jnp.stack(hg_p), jnp.stack(gla_p), jnp.stack(hg_s), jnp.stack(gla_s),
            jnp.stack(v_rows))
```

```python
import functools

import jax
import jax.numpy as jnp
from jax import lax
from jax.experimental import pallas as pl
from jax.experimental.pallas import tpu as pltpu

F32 = jnp.float32
BF16 = jnp.bfloat16

D_MODEL = 2048
BATCH = 4
SEQ = 2048
DEPTH = 4
DEC_BATCH = 128
BRANCH_WIDTH = D_MODEL // 2
N_BRANCH = 3
SGU_GROUPS = 8
SGU_CHUNK = 128
HG_HEADS = 8
HG_DK = BRANCH_WIDTH // HG_HEADS
HG_DV = BRANCH_WIDTH // HG_HEADS
GLA_HEADS = 4
GLA_DK_TOTAL = BRANCH_WIDTH // 2
GLA_DK = GLA_DK_TOTAL // GLA_HEADS
GLA_DV = BRANCH_WIDTH // GLA_HEADS
GLA_RANK = 16
GLA_TAU = 16.0
D_FF = 4 * D_MODEL
EPS = 1e-6

N_PROMPT = BATCH * SEQ
N_ALL = N_PROMPT + DEC_BATCH
MOD_ROWS = DEC_BATCH + 8

OFF_GLR = 2 * BRANCH_WIDTH + 4 * BRANCH_WIDTH + 2 * GLA_DK_TOTAL + 2 * BRANCH_WIDTH
OFF_GATE = OFF_GLR + GLA_RANK
Z_MAIN = OFF_GLR

LANES = 128
ROW_TILE = 128
MM_TM = 4160
MM_ROW_SUB = 832
MM_TM_HALF = 2080
REC_CHUNK = 128
HALF = REC_CHUNK // 2
SAMPLE_NB = 8
VMEM_LIMIT = 56 * 1024 * 1024


def _cparams(sem):
    return pltpu.CompilerParams(dimension_semantics=sem, vmem_limit_bytes=VMEM_LIMIT)


def _sigmoid(x):
    return 1.0 / (1.0 + jnp.exp(-x))


def _silu(x):
    return x * _sigmoid(x)


def _log_sigmoid(x):
    return jnp.minimum(x, 0.0) - jnp.log(1.0 + jnp.exp(-jnp.abs(x)))


def _rms(x, g):
    return x * lax.rsqrt(jnp.mean(x * x, axis=-1, keepdims=True) + EPS) * g


def _mod_kernel(c_ref, w_ref, b_ref, o_ref):
    a = _silu(c_ref[...]).astype(BF16)
    o_ref[...] = jnp.dot(a, w_ref[...].astype(BF16), preferred_element_type=F32) + b_ref[...]


def _modulation(c_all, w_ada, b_ada):
    tn = 1024
    n = 6 * D_MODEL
    return pl.pallas_call(
        _mod_kernel,
        grid=(DEPTH, n // tn),
        in_specs=[
            pl.BlockSpec((MOD_ROWS, D_MODEL), lambda l, j: (0, 0)),
            pl.BlockSpec((None, D_MODEL, tn), lambda l, j: (l, 0, j)),
            pl.BlockSpec((None, 1, tn), lambda l, j: (l, 0, j)),
        ],
        out_specs=pl.BlockSpec((None, MOD_ROWS, tn), lambda l, j: (l, 0, j)),
        out_shape=jax.ShapeDtypeStruct((DEPTH, MOD_ROWS, n), F32),
        compiler_params=_cparams(("parallel", "parallel")),
        name="adaln_mod",
    )(c_all, w_ada, b_ada.reshape(DEPTH, 1, n))


def _mod_specs(l, j):
    return [
        pl.BlockSpec((None, DEC_BATCH, D_MODEL), lambda i: (l, 0, j)),
        pl.BlockSpec((None, 8, D_MODEL), lambda i: (l, DEC_BATCH // 8, j)),
    ]


def _pick_mod(s_ref, p_ref):
    i = pl.program_id(0)
    tiles_per_seq = SEQ // ROW_TILE
    b = jnp.minimum(i // tiles_per_seq, BATCH - 1)
    is_sample = i >= N_PROMPT // ROW_TILE
    return jnp.where(is_sample, s_ref[...], p_ref[pl.ds(b, 1), :])


def _prenorm_kernel(x_ref, g_ref, sc_s, sc_p, sh_s, sh_p, h_ref):
    h = _rms(x_ref[...], g_ref[...]) * (1.0 + _pick_mod(sc_s, sc_p)) + _pick_mod(sh_s, sh_p)
    h_ref[...] = h.astype(BF16)


def _prenorm(x, g, mod, l, j_shift, j_scale):
    row = pl.BlockSpec((ROW_TILE, D_MODEL), lambda i: (i, 0))
    vec = pl.BlockSpec((1, D_MODEL), lambda i: (0, 0))
    return pl.pallas_call(
        _prenorm_kernel,
        grid=(N_ALL // ROW_TILE,),
        in_specs=[row, vec] + _mod_specs(l, j_scale) + _mod_specs(l, j_shift),
        out_specs=row,
        out_shape=jax.ShapeDtypeStruct((N_ALL, D_MODEL), BF16),
        compiler_params=_cparams(("parallel",)),
        name="prenorm",
    )(x, g.reshape(1, D_MODEL), mod, mod, mod, mod)


def _resid_kernel(x_ref, y_ref, gpost_ref, gt_s, gt_p, *rest, with_next):
    x_new = x_ref[...] + _pick_mod(gt_s, gt_p) * _rms(y_ref[...], gpost_ref[...])
    if with_next:
        gpre_ref, sc_s, sc_p, sh_s, sh_p, xo_ref, h_ref = rest
        xo_ref[...] = x_new
        h = _rms(x_new, gpre_ref[...]) * (1.0 + _pick_mod(sc_s, sc_p)) + _pick_mod(sh_s, sh_p)
        h_ref[...] = h.astype(BF16)
    else:
        (xo_ref,) = rest
        xo_ref[...] = x_new


def _resid(x, y, g_post, mod, l_gate, j_gate, nxt=None):
    row = pl.BlockSpec((ROW_TILE, D_MODEL), lambda i: (i, 0))
    vec = pl.BlockSpec((1, D_MODEL), lambda i: (0, 0))
    in_specs = [row, row, vec] + _mod_specs(l_gate, j_gate)
    args = [x, y, g_post.reshape(1, D_MODEL), mod, mod]
    out_specs = [row]
    out_shape = [jax.ShapeDtypeStruct((N_ALL, D_MODEL), F32)]
    if nxt is not None:
        g_pre, l_n, j_shift, j_scale = nxt
        in_specs += [vec] + _mod_specs(l_n, j_scale) + _mod_specs(l_n, j_shift)
        args += [g_pre.reshape(1, D_MODEL), mod, mod, mod, mod]
        out_specs.append(row)
        out_shape.append(jax.ShapeDtypeStruct((N_ALL, D_MODEL), BF16))
    return pl.pallas_call(
        functools.partial(_resid_kernel, with_next=nxt is not None),
        grid=(N_ALL // ROW_TILE,),
        in_specs=in_specs,
        out_specs=out_specs,
        out_shape=out_shape,
        compiler_params=_cparams(("parallel",)),
        name="resid_norm",
    )(*args)


def _act_none(x):
    return x


def _act_relu2(x):
    return jnp.square(jnp.maximum(x, 0.0))


def _nt(a, b):
    return lax.dot_general(a, b, (((1,), (1,)), ((), ())), preferred_element_type=F32)


def _mm_kernel(a_ref, w_ref, o_ref, *, nk, act, w_transposed, row_sub):
    w = w_ref[...].astype(BF16)
    if nk > 1:
        @pl.when(pl.program_id(2) == 0)
        def _():
            o_ref[...] = jnp.zeros_like(o_ref)

    for r in range(0, a_ref.shape[0], row_sub):
        rows = slice(r, r + row_sub)
        a = a_ref[rows, :]
        part = _nt(a, w) if w_transposed else jnp.dot(a, w, preferred_element_type=F32)
        if nk == 1:
            o_ref[rows, :] = act(part).astype(o_ref.dtype)
        else:
            o_ref[rows, :] += part


def _matmul(a, w, l, *, n_out, col_off=0, tm=MM_TM, tn=256, tk=None, row_sub=MM_ROW_SUB,
            act=_act_none, out_dtype=F32, w_transposed=False, name):
    k_dim = a.shape[1]
    tk = k_dim if tk is None else tk
    nk = k_dim // tk
    assert nk == 1 or (act is _act_none and out_dtype == F32)
    if nk == 1:
        a_spec = pl.BlockSpec((tm, tk), lambda i, j, k: (i, k), pipeline_mode=pl.Buffered(1))
    else:
        a_spec = pl.BlockSpec((tm, tk), lambda i, j, k: (i, k))
    if w_transposed:
        assert col_off % 8 == 0 and tn % 8 == 0 and tk % LANES == 0
        w_spec = pl.BlockSpec((None, pl.Element(tn), pl.Element(tk)),
                              lambda i, j, k: (l, pl.multiple_of(col_off + j * tn, 8),
                                               pl.multiple_of(k * tk, LANES)))
    else:
        w_spec = pl.BlockSpec((None, tk, tn), lambda i, j, k: (l, k, col_off // tn + j))
    return pl.pallas_call(
        functools.partial(_mm_kernel, nk=nk, act=act, w_transposed=w_transposed, row_sub=row_sub),
        grid=(N_ALL // tm, n_out // tn, nk),
        in_specs=[a_spec, w_spec],
        out_specs=pl.BlockSpec((tm, tn), lambda i, j, k: (i, j)),
        out_shape=jax.ShapeDtypeStruct((N_ALL, n_out), out_dtype),
        compiler_params=_cparams(("parallel", "parallel", "arbitrary")),
        name=name,
    )(a, w)


def _merge_kernel(ya_ref, yb_ref, yc_ref, wa_ref, wb_ref, wc_ref, ga_ref, gb_ref, gc_ref, o_ref):
    wa = wa_ref[...].astype(BF16)
    wb = wb_ref[...].astype(BF16)
    wc = wc_ref[...].astype(BF16)
    sub = MM_TM_HALF // 2
    for r in range(0, MM_TM_HALF, sub):
        rows = slice(r, r + sub)
        merged = ga_ref[rows, :] * jnp.dot(ya_ref[rows, :], wa, preferred_element_type=F32)
        merged = merged + gb_ref[rows, :] * jnp.dot(yb_ref[rows, :], wb, preferred_element_type=F32)
        merged = merged + gc_ref[rows, :] * jnp.dot(yc_ref[rows, :], wc, preferred_element_type=F32)
        o_ref[rows, :] = merged.astype(BF16)


def _merge(ya, yb, yc, w_branch, gates, l):
    tn = 256
    tm = MM_TM_HALF
    nb = D_MODEL // tn
    y_spec = pl.BlockSpec((tm, BRANCH_WIDTH), lambda i, j: (i, 0), pipeline_mode=pl.Buffered(1))

    def w_spec(b):
        return pl.BlockSpec((None, None, BRANCH_WIDTH, tn), lambda i, j: (l, b, 0, j))

    def g_spec(b):
        return pl.BlockSpec((tm, tn), lambda i, j: (i, b * nb + j))

    return pl.pallas_call(
        _merge_kernel,
        grid=(N_ALL // tm, nb),
        in_specs=[y_spec, y_spec, y_spec, w_spec(0), w_spec(1), w_spec(2),
                  g_spec(0), g_spec(1), g_spec(2)],
        out_specs=pl.BlockSpec((tm, tn), lambda i, j: (i, j)),
        out_shape=jax.ShapeDtypeStruct((N_ALL, D_MODEL), BF16),
        compiler_params=_cparams(("parallel", "parallel")),
        name="branch_merge",
    )(ya, yb, yc, w_branch, w_branch, w_branch, gates, gates, gates)


def _sgu_kernel(zu_ref, zv_ref, lg_ref, lb_ref, ws_ref, bst_ref, y_ref, vs_ref):
    i = pl.program_id(0)
    u = jax.nn.gelu(zu_ref[...], approximate=True)
    v = jax.nn.gelu(zv_ref[...], approximate=True)
    mu = jnp.mean(v, axis=-1, keepdims=True)
    vc = v - mu
    v = vc * lax.rsqrt(jnp.mean(vc * vc, axis=-1, keepdims=True) + EPS) * lg_ref[...] + lb_ref[...]
    is_sample = i == N_PROMPT // ROW_TILE

    @pl.when(jnp.logical_not(is_sample))
    def _():
        n_idx = lax.broadcasted_iota(jnp.int32, (SGU_CHUNK, SGU_CHUNK), 0)
        m_idx = lax.broadcasted_iota(jnp.int32, (SGU_CHUNK, SGU_CHUNK), 1)
        causal = m_idx <= n_idx
        for g in range(SGU_GROUPS):
            sl = slice(g * LANES, (g + 1) * LANES)
            wm = jnp.where(causal, ws_ref[g], 0.0).astype(BF16)
            s = jnp.dot(wm, v[:, sl].astype(BF16), preferred_element_type=F32) + bst_ref[:, g:g + 1]
            y_ref[:, sl] = (u[:, sl] * s).astype(BF16)

    @pl.when(is_sample)
    def _():
        vs_ref[...] = v
        for g in range(SGU_GROUPS):
            sl = slice(g * LANES, (g + 1) * LANES)
            s = v[:, sl] * ws_ref[g][0:1, 0:1] + bst_ref[0:1, g:g + 1]
            y_ref[:, sl] = (u[:, sl] * s).astype(BF16)


def _sgu(z, ln_g, ln_b, w_s, b_s_t, l):
    vec = pl.BlockSpec((1, BRANCH_WIDTH), lambda i: (0, 0))
    return pl.pallas_call(
        _sgu_kernel,
        grid=(N_ALL // ROW_TILE,),
        in_specs=[
            pl.BlockSpec((ROW_TILE, BRANCH_WIDTH), lambda i: (i, 0)),
            pl.BlockSpec((ROW_TILE, BRANCH_WIDTH), lambda i: (i, 1)),
            vec, vec,
            pl.BlockSpec((None, SGU_GROUPS, SGU_CHUNK, SGU_CHUNK), lambda i: (l, 0, 0, 0)),
            pl.BlockSpec((None, SGU_CHUNK, SGU_GROUPS), lambda i: (l, 0, 0)),
        ],
        out_specs=[
            pl.BlockSpec((ROW_TILE, BRANCH_WIDTH), lambda i: (i, 0)),
            pl.BlockSpec((DEC_BATCH, BRANCH_WIDTH), lambda i: (0, 0)),
        ],
        out_shape=[
            jax.ShapeDtypeStruct((N_ALL, BRANCH_WIDTH), BF16),
            jax.ShapeDtypeStruct((DEC_BATCH, BRANCH_WIDTH), F32),
        ],
        compiler_params=_cparams(("arbitrary",)),
        name="sgu",
    )(z, z, ln_g[l].reshape(1, BRANCH_WIDTH), ln_b[l].reshape(1, BRANCH_WIDTH), w_s, b_s_t)


def _hgrn_lower_bound(lb_ref, l):
    raw = lb_ref[...]
    e = jnp.exp(raw - jnp.max(raw, axis=0, keepdims=True))
    p = e / jnp.sum(e, axis=0, keepdims=True)
    acc = p[0:1, :]
    for j in range(1, l + 1):
        acc = acc + p[j:j + 1, :]
    return jnp.maximum(acc - p[0:1, :], 0.0)


def _hgrn_prep(zq, zf, lb_ref, l):
    e = jnp.exp(-jnp.abs(zf))
    ope = 1.0 + e
    sig_neg = jnp.where(zf >= 0.0, e, 1.0) / ope
    log_sig = jnp.minimum(zf, 0.0) - jnp.log(ope)
    if l == 0:
        return _silu(zq), sig_neg, log_sig
    lb = _hgrn_lower_bound(lb_ref, l)
    a = jnp.log(lb)
    c = jnp.log1p(-lb) + log_sig
    log_f = jnp.maximum(a, c) + jnp.log(1.0 + jnp.exp(-jnp.abs(a - c)))
    return _silu(zq), (1.0 - lb) * sig_neg, log_f


def _gla_prep(zq, zk, glr, wup_ref, bup_ref):
    x = jnp.dot(glr[:, :GLA_RANK].astype(BF16), wup_ref[...].astype(BF16),
                preferred_element_type=F32) + bup_ref[...]
    log_a = _log_sigmoid(x) / GLA_TAU
    return zq, zk * (GLA_DK ** -0.5), log_a


def _lane_bcast_col(row):
    return jnp.broadcast_to(row, (LANES, LANES)).T


def _head_out(o, g_row, gate):
    o = o * lax.rsqrt(jnp.mean(o * o, axis=-1, keepdims=True) + EPS) * g_row
    return o * _silu(gate)


def _prompt_chunk(q, k, v, log_f, gate, gnorm_ref, s_ref, y_ref, so_ref, *, heads, dk, dv):
    c = pl.program_id(1)
    n_chunks = pl.num_programs(1)

    @pl.when(c == 0)
    def _():
        s_ref[...] = jnp.zeros_like(s_ref)

    t_idx = lax.broadcasted_iota(jnp.int32, (REC_CHUNK, REC_CHUNK), 0)
    s_idx = lax.broadcasted_iota(jnp.int32, (REC_CHUNK, REC_CHUNK), 1)
    tril = jnp.where(s_idx <= t_idx, 1.0, 0.0).astype(F32)
    b_all = jnp.dot(tril, log_f, precision=lax.Precision.HIGHEST, preferred_element_type=F32)
    causal_half = (lax.broadcasted_iota(jnp.int32, (HALF, HALF), 1)
                   <= lax.broadcasted_iota(jnp.int32, (HALF, HALF), 0))

    for h in range(heads):
        ks = slice(h * dk, (h + 1) * dk)
        vs = slice(h * dv, (h + 1) * dv)
        qh, kh, bh = q[:, ks], k[:, ks], b_all[:, ks]
        vh = v[:, vs].astype(BF16)
        q_a, q_b = qh[:HALF], qh[HALF:]
        k_a, k_b = kh[:HALF], kh[HALF:]
        b_a, b_b = bh[:HALF], bh[HALF:]
        v_a, v_b = vh[:HALF], vh[HALF:]
        r_a = b_a[HALF // 2 - 1:HALF // 2]
        r_b = b_b[HALF // 2 - 1:HALF // 2]
        r_m = b_a[HALF - 1:HALF]
        att_aa = _nt((q_a * jnp.exp(b_a - r_a)).astype(BF16), (k_a * jnp.exp(r_a - b_a)).astype(BF16))
        att_bb = _nt((q_b * jnp.exp(b_b - r_b)).astype(BF16), (k_b * jnp.exp(r_b - b_b)).astype(BF16))
        att_ba = _nt((q_b * jnp.exp(b_b - r_m)).astype(BF16), (k_a * jnp.exp(r_m - b_a)).astype(BF16))
        att_aa = jnp.where(causal_half, att_aa, 0.0).astype(BF16)
        att_bb = jnp.where(causal_half, att_bb, 0.0).astype(BF16)
        o_a = jnp.dot(att_aa, v_a, preferred_element_type=F32)
        o_b = (jnp.dot(att_ba.astype(BF16), v_a, preferred_element_type=F32)
               + jnp.dot(att_bb, v_b, preferred_element_type=F32))
        s_old = s_ref[h]
        o = jnp.concatenate([o_a, o_b], axis=0) + jnp.dot(
            (qh * jnp.exp(bh)).astype(BF16), s_old.astype(BF16), preferred_element_type=F32)
        y_ref[:, vs] = _head_out(o, gnorm_ref[...], gate[:, vs]).astype(BF16)

        b_last = bh[REC_CHUNK - 1:REC_CHUNK]
        k_dec_t = (kh * jnp.exp(b_last - bh)).T.astype(BF16)
        upd = jnp.dot(k_dec_t, vh, preferred_element_type=F32)
        decay = _lane_bcast_col(jnp.exp(b_last))
        for j in range(dv // LANES):
            ls = slice(j * LANES, (j + 1) * LANES)
            s_ref[h, :, ls] = decay * s_old[:, ls] + upd[:, ls]

    @pl.when(c == n_chunks - 1)
    def _():
        so_ref[...] = s_ref[...]


def _hgrn_prompt_kernel(zq_ref, zf_ref, zi_ref, zg_ref, lb_ref, gnorm_ref, y_ref, so_ref, s_ref, *, l):
    q, k, log_f = _hgrn_prep(zq_ref[...], zf_ref[...], lb_ref, l)
    _prompt_chunk(q, k, zi_ref[...], log_f, zg_ref[...], gnorm_ref, s_ref, y_ref, so_ref,
                  heads=HG_HEADS, dk=HG_DK, dv=HG_DV)


def _gla_prompt_kernel(zq_ref, zk_ref, zv_ref, zr_ref, glr_ref, wup_ref, bup_ref, gnorm_ref,
                       y_ref, so_ref, s_ref):
    q, k, log_a = _gla_prep(zq_ref[...], zk_ref[...], glr_ref[...], wup_ref, bup_ref)
    _prompt_chunk(q, k, zv_ref[...], log_a, zr_ref[...], gnorm_ref, s_ref, y_ref, so_ref,
                  heads=GLA_HEADS, dk=GLA_DK, dv=GLA_DV)


def _zcol(width, block):
    n_c = SEQ // REC_CHUNK
    return pl.BlockSpec((REC_CHUNK, width), lambda b, c: (b * n_c + c, block))


def _rec_prompt_call(kernel, in_specs, args, heads, dk, dv, name):
    n_c = SEQ // REC_CHUNK
    return pl.pallas_call(
        kernel,
        grid=(BATCH, n_c),
        in_specs=in_specs,
        out_specs=[
            pl.BlockSpec((REC_CHUNK, BRANCH_WIDTH), lambda b, c: (b * n_c + c, 0)),
            pl.BlockSpec((None, heads, dk, dv), lambda b, c: (b, 0, 0, 0)),
        ],
        out_shape=[
            jax.ShapeDtypeStruct((N_PROMPT, BRANCH_WIDTH), BF16),
            jax.ShapeDtypeStruct((BATCH, heads, dk, dv), F32),
        ],
        scratch_shapes=[pltpu.VMEM((heads, dk, dv), F32)],
        compiler_params=_cparams(("parallel", "arbitrary")),
        name=name,
    )(*args)


def _hgrn_prompt(z, hg_lb, g_norm, l):
    w = BRANCH_WIDTH
    in_specs = [_zcol(w, 2), _zcol(w, 3), _zcol(w, 4), _zcol(w, 5),
                pl.BlockSpec((DEPTH, w), lambda b, c: (0, 0)),
                pl.BlockSpec((1, HG_DV), lambda b, c: (0, 0))]
    args = [z, z, z, z, hg_lb, g_norm[l].reshape(1, HG_DV)]
    return _rec_prompt_call(functools.partial(_hgrn_prompt_kernel, l=l), in_specs, args,
                            HG_HEADS, HG_DK, HG_DV, "hgrn_prompt")


def _gla_prompt(z, z_glr, w_up, b_up, g_norm, l):
    w = BRANCH_WIDTH
    kw = GLA_DK_TOTAL
    in_specs = [_zcol(kw, 6 * w // kw), _zcol(kw, 6 * w // kw + 1), _zcol(w, 7), _zcol(w, 8),
                _zcol(LANES, 0),
                pl.BlockSpec((None, GLA_RANK, kw), lambda b, c: (l, 0, 0)),
                pl.BlockSpec((1, kw), lambda b, c: (0, 0)),
                pl.BlockSpec((1, GLA_DV), lambda b, c: (0, 0))]
    args = [z, z, z, z, z_glr, w_up, b_up[l].reshape(1, kw), g_norm[l].reshape(1, GLA_DV)]
    return _rec_prompt_call(_gla_prompt_kernel, in_specs, args,
                            GLA_HEADS, GLA_DK, GLA_DV, "gla_prompt")


def _sample_step(q, k, v, log_f, gate, gnorm_ref, s_ref, so_ref, i, *, heads, dk, dv):
    outs = []
    for h in range(heads):
        ks = slice(h * dk, (h + 1) * dk)
        f_col = _lane_bcast_col(jnp.exp(log_f[:, ks]))
        k_col = _lane_bcast_col(k[:, ks])
        q16 = jnp.broadcast_to(q[:, ks], (16, dk)).astype(BF16)
        parts = []
        for j in range(dv // LANES):
            ls = slice(j * LANES, (j + 1) * LANES)
            v_row = v[:, h * dv + j * LANES:h * dv + (j + 1) * LANES]
            s_new = f_col * s_ref[i, h, :, ls] + k_col * v_row
            so_ref[i, h, :, ls] = s_new
            parts.append(jnp.dot(q16, s_new.astype(BF16), preferred_element_type=F32)[0:1])
        o = parts[0] if len(parts) == 1 else jnp.concatenate(parts, axis=1)
        outs.append(_head_out(o, gnorm_ref[...], gate[:, h * dv:(h + 1) * dv]))
    return jnp.concatenate(outs, axis=1)


def _hgrn_sample_kernel(zq_ref, zf_ref, zi_ref, zg_ref, lb_ref, gnorm_ref, s_ref, *rest, l):
    y_ref, so_ref = rest[-2:]

    def body(i, carry):
        row = pl.ds(i, 1)
        q, k, log_f = _hgrn_prep(zq_ref[row, :], zf_ref[row, :], lb_ref, l)
        y_ref[row, :] = _sample_step(q, k, zi_ref[row, :], log_f, zg_ref[row, :], gnorm_ref,
                                     s_ref, so_ref, i, heads=HG_HEADS, dk=HG_DK, dv=HG_DV)
        return carry

    lax.fori_loop(0, SAMPLE_NB, body, 0)


def _gla_sample_kernel(zq_ref, zk_ref, zv_ref, zr_ref, glr_ref, wup_ref, bup_ref, gnorm_ref, s_ref,
                       *rest):
    y_ref, so_ref = rest[-2:]

    def body(i, carry):
        row = pl.ds(i, 1)
        glr = jnp.broadcast_to(glr_ref[row, :], (16, LANES))
        q, k, log_a = _gla_prep(zq_ref[row, :], zk_ref[row, :], glr, wup_ref, bup_ref)
        y_ref[row, :] = _sample_step(q, k, zv_ref[row, :], log_a[0:1], zr_ref[row, :],
                                     gnorm_ref, s_ref, so_ref, i,
                                     heads=GLA_HEADS, dk=GLA_DK, dv=GLA_DV)
        return carry

    lax.fori_loop(0, SAMPLE_NB, body, 0)


def _zrow(width, block):
    return pl.BlockSpec((SAMPLE_NB, width), lambda i: (N_PROMPT // SAMPLE_NB + i, block))


def _rec_sample_call(kernel, in_specs, args, state, new_state, l, heads, dk, dv, name):
    st = pl.BlockSpec((None, SAMPLE_NB, heads, dk, dv), lambda i: (l, i, 0, 0, 0))
    in_specs = in_specs + [st]
    args = list(args) + [state]
    aliases = {}
    if new_state is not None:
        aliases = {len(args): 1}
        in_specs = in_specs + [pl.BlockSpec(memory_space=pl.ANY)]
        args = args + [new_state]
    return pl.pallas_call(
        kernel,
        grid=(DEC_BATCH // SAMPLE_NB,),
        in_specs=in_specs,
        out_specs=[pl.BlockSpec((SAMPLE_NB, BRANCH_WIDTH), lambda i: (i, 0)), st],
        out_shape=[
            jax.ShapeDtypeStruct((DEC_BATCH, BRANCH_WIDTH), F32),
            jax.ShapeDtypeStruct((DEPTH, DEC_BATCH, heads, dk, dv), F32),
        ],
        input_output_aliases=aliases,
        compiler_params=_cparams(("parallel",)),
        name=name,
    )(*args)


def _hgrn_sample(z, hg_lb, g_norm, state, new_state, l):
    w = BRANCH_WIDTH
    in_specs = [_zrow(w, 2), _zrow(w, 3), _zrow(w, 4), _zrow(w, 5),
                pl.BlockSpec((DEPTH, w), lambda i: (0, 0)),
                pl.BlockSpec((1, HG_DV), lambda i: (0, 0))]
    args = [z, z, z, z, hg_lb, g_norm[l].reshape(1, HG_DV)]
    return _rec_sample_call(functools.partial(_hgrn_sample_kernel, l=l), in_specs, args, state,
                            new_state, l, HG_HEADS, HG_DK, HG_DV, "hgrn_sample")


def _gla_sample(z, z_glr, w_up, b_up, g_norm, state, new_state, l):
    w = BRANCH_WIDTH
    kw = GLA_DK_TOTAL
    in_specs = [_zrow(kw, 6 * w // kw), _zrow(kw, 6 * w // kw + 1), _zrow(w, 7), _zrow(w, 8),
                _zrow(LANES, 0),
                pl.BlockSpec((None, GLA_RANK, kw), lambda i: (l, 0, 0)),
                pl.BlockSpec((1, kw), lambda i: (0, 0)),
                pl.BlockSpec((1, GLA_DV), lambda i: (0, 0))]
    args = [z, z, z, z, z_glr, w_up, b_up[l].reshape(1, kw), g_norm[l].reshape(1, GLA_DV)]
    return _rec_sample_call(_gla_sample_kernel, in_specs, args, state, new_state, l,
                            GLA_HEADS, GLA_DK, GLA_DV, "gla_sample")


def kernel(x_prompt, x_sample, state_hgrn, state_gla, c_prompt, c_sample, w_ada, b_ada, g_pre_mix, g_post_mix, g_pre_mlp, g_post_mlp, w_in, sgu_ln_g, sgu_ln_b, sgu_w_s, sgu_b_s, hg_lb, hg_norm_g, gla_w_up, gla_b_up, gla_norm_g, w_branch, w_out, w_mlp_up, w_mlp_down):
    x = jnp.concatenate([x_prompt.reshape(N_PROMPT, D_MODEL), x_sample.reshape(DEC_BATCH, D_MODEL)], axis=0)
    c_all = jnp.concatenate([c_sample, c_prompt, jnp.zeros((MOD_ROWS - DEC_BATCH - BATCH, D_MODEL), F32)], axis=0)
    mod = _modulation(c_all, w_ada, b_ada)
    w_in_t = jnp.swapaxes(w_in, 1, 2)
    b_s_t = jnp.swapaxes(sgu_b_s, 1, 2)

    hg_p, gla_p, v_rows = [], [], []
    hg_s = gla_s = None
    h = _prenorm(x, g_pre_mix[0], mod, 0, 0, 1)
    for l in range(DEPTH):
        z = _matmul(h, w_in_t, l, n_out=Z_MAIN, w_transposed=True, name="in_proj")
        z_glr = _matmul(h, w_in_t, l, n_out=LANES, col_off=OFF_GLR, tn=LANES, w_transposed=True,
                        name="in_proj_lowrank")
        gates = _matmul(h, w_in_t, l, n_out=N_BRANCH * D_MODEL, col_off=OFF_GATE, w_transposed=True,
                        act=_sigmoid, out_dtype=BF16, name="in_proj_gate")

        y_a, v_s = _sgu(z, sgu_ln_g, sgu_ln_b, sgu_w_s, b_s_t, l)
        yb_p, s_hg_p = _hgrn_prompt(z, hg_lb, hg_norm_g, l)
        yb_s, hg_s = _hgrn_sample(z, hg_lb, hg_norm_g, state_hgrn, hg_s, l)
        yc_p, s_gla_p = _gla_prompt(z, z_glr, gla_w_up, gla_b_up, gla_norm_g, l)
        yc_s, gla_s = _gla_sample(z, z_glr, gla_w_up, gla_b_up, gla_norm_g, state_gla, gla_s, l)
        y_b = jnp.concatenate([yb_p, yb_s.astype(BF16)], axis=0)
        y_c = jnp.concatenate([yc_p, yc_s.astype(BF16)], axis=0)

        merged = _merge(y_a, y_b, y_c, w_branch, gates, l)
        out = _matmul(merged, w_out, l, n_out=D_MODEL, name="out_proj")
        x, h2 = _resid(x, out, g_post_mix[l], mod, l, 2, nxt=(g_pre_mlp[l], l, 3, 4))

        up = _matmul(h2, w_mlp_up, l, n_out=D_FF, act=_act_relu2, out_dtype=BF16, name="mlp_up")
        y2 = _matmul(up, w_mlp_down, l, n_out=D_MODEL, tm=MM_TM_HALF, tn=1024, tk=1024,
                     row_sub=MM_TM_HALF // 2, name="mlp_down")
        if l + 1 < DEPTH:
            x, h = _resid(x, y2, g_post_mlp[l], mod, l, 5, nxt=(g_pre_mix[l + 1], l + 1, 0, 1))
        else:
            (x,) = _resid(x, y2, g_post_mlp[l], mod, l, 5)

        hg_p.append(s_hg_p)
        gla_p.append(s_gla_p)
        v_rows.append(v_s.reshape(DEC_BATCH, 1, BRANCH_WIDTH))

    y_prompt = x[:N_PROMPT].reshape(BATCH, SEQ, D_MODEL)
    y_sample = x[N_PROMPT:].reshape(DEC_BATCH, 1, D_MODEL)
    return (y_prompt, y_sample, jnp.stack(hg_p), jnp.stack(gla_p), hg_s, gla_s, jnp.stack(v_rows))
```

```python
import functools

import jax
import jax.numpy as jnp
from jax import lax
from jax.experimental import pallas as pl
from jax.experimental.pallas import tpu as pltpu

F32 = jnp.float32
BF16 = jnp.bfloat16

D_MODEL = 2048
BATCH = 4
SEQ = 2048
DEPTH = 4
DEC_BATCH = 128
BRANCH_WIDTH = D_MODEL // 2
N_BRANCH = 3
SGU_GROUPS = 8
SGU_CHUNK = 128
HG_HEADS = 8
HG_DK = BRANCH_WIDTH // HG_HEADS
HG_DV = BRANCH_WIDTH // HG_HEADS
GLA_HEADS = 4
GLA_DK_TOTAL = BRANCH_WIDTH // 2
GLA_DK = GLA_DK_TOTAL // GLA_HEADS
GLA_DV = BRANCH_WIDTH // GLA_HEADS
GLA_RANK = 16
GLA_TAU = 16.0
D_FF = 4 * D_MODEL
EPS = 1e-6

N_PROMPT = BATCH * SEQ
N_ALL = N_PROMPT + DEC_BATCH
MOD_ROWS = DEC_BATCH + 8

OFF_GLR = 2 * BRANCH_WIDTH + 4 * BRANCH_WIDTH + 2 * GLA_DK_TOTAL + 2 * BRANCH_WIDTH
OFF_GATE = OFF_GLR + GLA_RANK
Z_MAIN = OFF_GLR

LANES = 128
ROW_TILE = 128
NORM_TILE = 640
MAX_SPLIT_EXPONENT = 60.0
MM_TM = 4160
MM_ROW_SUB = 832
MM_TM_HALF = 2080
REC_CHUNK = 128
HALF = REC_CHUNK // 2
SAMPLE_NB = 8
VMEM_LIMIT = 56 * 1024 * 1024


def _cparams(sem):
    return pltpu.CompilerParams(dimension_semantics=sem, vmem_limit_bytes=VMEM_LIMIT)


def _sigmoid(x):
    return 1.0 / (1.0 + jnp.exp(-x))


def _silu(x):
    return x * _sigmoid(x)


def _log_sigmoid(x):
    return jnp.minimum(x, 0.0) - jnp.log(1.0 + jnp.exp(-jnp.abs(x)))


def _rms(x, g):
    return x * lax.rsqrt(jnp.mean(x * x, axis=-1, keepdims=True) + EPS) * g


def _mod_kernel(c_ref, w_ref, b_ref, o_ref):
    a = _silu(c_ref[...]).astype(BF16)
    o_ref[...] = jnp.dot(a, w_ref[...].astype(BF16), preferred_element_type=F32) + b_ref[...]


def _modulation(c_all, w_ada, b_ada):
    tn = 1024
    n = 6 * D_MODEL
    return pl.pallas_call(
        _mod_kernel,
        grid=(DEPTH, n // tn),
        in_specs=[
            pl.BlockSpec((MOD_ROWS, D_MODEL), lambda l, j: (0, 0)),
            pl.BlockSpec((None, D_MODEL, tn), lambda l, j: (l, 0, j)),
            pl.BlockSpec((None, 1, tn), lambda l, j: (l, 0, j)),
        ],
        out_specs=pl.BlockSpec((None, MOD_ROWS, tn), lambda l, j: (l, 0, j)),
        out_shape=jax.ShapeDtypeStruct((DEPTH, MOD_ROWS, n), F32),
        compiler_params=_cparams(("parallel", "parallel")),
        name="adaln_mod",
    )(c_all, w_ada, b_ada.reshape(DEPTH, 1, n))


def _mod_specs(l, j):
    return [
        pl.BlockSpec((None, DEC_BATCH, D_MODEL), lambda i: (l, 0, j)),
        pl.BlockSpec((None, 8, D_MODEL), lambda i: (l, DEC_BATCH // 8, j)),
    ]


def _pick_mod(s_ref, p_ref, sub):
    g = pl.program_id(0) * (NORM_TILE // ROW_TILE) + sub
    b = jnp.minimum(g // (SEQ // ROW_TILE), BATCH - 1)
    is_sample = g >= N_PROMPT // ROW_TILE
    return jnp.where(is_sample, s_ref[...], p_ref[pl.ds(b, 1), :])


def _prenorm_kernel(x_ref, g_ref, sc_s, sc_p, sh_s, sh_p, h_ref):
    for sub in range(NORM_TILE // ROW_TILE):
        rows = slice(sub * ROW_TILE, (sub + 1) * ROW_TILE)
        h = (_rms(x_ref[rows, :], g_ref[...]) * (1.0 + _pick_mod(sc_s, sc_p, sub))
             + _pick_mod(sh_s, sh_p, sub))
        h_ref[rows, :] = h.astype(BF16)


def _prenorm(x, g, mod, l, j_shift, j_scale):
    row = pl.BlockSpec((NORM_TILE, D_MODEL), lambda i: (i, 0))
    vec = pl.BlockSpec((1, D_MODEL), lambda i: (0, 0))
    return pl.pallas_call(
        _prenorm_kernel,
        grid=(N_ALL // NORM_TILE,),
        in_specs=[row, vec] + _mod_specs(l, j_scale) + _mod_specs(l, j_shift),
        out_specs=row,
        out_shape=jax.ShapeDtypeStruct((N_ALL, D_MODEL), BF16),
        compiler_params=_cparams(("parallel",)),
        name="prenorm",
    )(x, g.reshape(1, D_MODEL), mod, mod, mod, mod)


def _resid_kernel(x_ref, y_ref, gpost_ref, gt_s, gt_p, *rest, with_next):
    for sub in range(NORM_TILE // ROW_TILE):
        rows = slice(sub * ROW_TILE, (sub + 1) * ROW_TILE)
        x_new = x_ref[rows, :] + _pick_mod(gt_s, gt_p, sub) * _rms(y_ref[rows, :], gpost_ref[...])
        if with_next:
            gpre_ref, sc_s, sc_p, sh_s, sh_p, xo_ref, h_ref = rest
            xo_ref[rows, :] = x_new
            h = (_rms(x_new, gpre_ref[...]) * (1.0 + _pick_mod(sc_s, sc_p, sub))
                 + _pick_mod(sh_s, sh_p, sub))
            h_ref[rows, :] = h.astype(BF16)
        else:
            (xo_ref,) = rest
            xo_ref[rows, :] = x_new


def _resid(x, y, g_post, mod, l_gate, j_gate, nxt=None):
    row = pl.BlockSpec((NORM_TILE, D_MODEL), lambda i: (i, 0))
    vec = pl.BlockSpec((1, D_MODEL), lambda i: (0, 0))
    in_specs = [row, row, vec] + _mod_specs(l_gate, j_gate)
    args = [x, y, g_post.reshape(1, D_MODEL), mod, mod]
    out_specs = [row]
    out_shape = [jax.ShapeDtypeStruct((N_ALL, D_MODEL), F32)]
    if nxt is not None:
        g_pre, l_n, j_shift, j_scale = nxt
        in_specs += [vec] + _mod_specs(l_n, j_scale) + _mod_specs(l_n, j_shift)
        args += [g_pre.reshape(1, D_MODEL), mod, mod, mod, mod]
        out_specs.append(row)
        out_shape.append(jax.ShapeDtypeStruct((N_ALL, D_MODEL), BF16))
    return pl.pallas_call(
        functools.partial(_resid_kernel, with_next=nxt is not None),
        grid=(N_ALL // NORM_TILE,),
        in_specs=in_specs,
        out_specs=out_specs,
        out_shape=out_shape,
        compiler_params=_cparams(("parallel",)),
        name="resid_norm",
    )(*args)


def _act_none(x):
    return x


def _act_relu2(x):
    return jnp.square(jnp.maximum(x, 0.0))


def _nt(a, b):
    return lax.dot_general(a, b, (((1,), (1,)), ((), ())), preferred_element_type=F32)


def _mm_kernel(a_ref, w_ref, o_ref, *, nk, act, w_transposed, row_sub):
    w = w_ref[...].astype(BF16)
    if nk > 1:
        @pl.when(pl.program_id(2) == 0)
        def _():
            o_ref[...] = jnp.zeros_like(o_ref)

    for r in range(0, a_ref.shape[0], row_sub):
        rows = slice(r, r + row_sub)
        a = a_ref[rows, :]
        part = _nt(a, w) if w_transposed else jnp.dot(a, w, preferred_element_type=F32)
        if nk == 1:
            o_ref[rows, :] = act(part).astype(o_ref.dtype)
        else:
            o_ref[rows, :] += part


def _matmul(a, w, l, *, n_out, col_off=0, tm=MM_TM, tn=256, tk=None, row_sub=MM_ROW_SUB,
            act=_act_none, out_dtype=F32, w_transposed=False, name):
    k_dim = a.shape[1]
    tk = k_dim if tk is None else tk
    nk = k_dim // tk
    assert nk == 1 or (act is _act_none and out_dtype == F32)
    a_spec = pl.BlockSpec((tm, tk), lambda i, j, k: (i, k))
    if w_transposed:
        assert col_off % 8 == 0 and tn % 8 == 0 and tk % LANES == 0
        w_spec = pl.BlockSpec((None, pl.Element(tn), pl.Element(tk)),
                              lambda i, j, k: (l, pl.multiple_of(col_off + j * tn, 8),
                                               pl.multiple_of(k * tk, LANES)))
    else:
        w_spec = pl.BlockSpec((None, tk, tn), lambda i, j, k: (l, k, col_off // tn + j))
    return pl.pallas_call(
        functools.partial(_mm_kernel, nk=nk, act=act, w_transposed=w_transposed, row_sub=row_sub),
        grid=(N_ALL // tm, n_out // tn, nk),
        in_specs=[a_spec, w_spec],
        out_specs=pl.BlockSpec((tm, tn), lambda i, j, k: (i, j)),
        out_shape=jax.ShapeDtypeStruct((N_ALL, n_out), out_dtype),
        compiler_params=_cparams(("parallel", "parallel", "arbitrary")),
        name=name,
    )(a, w)


def _merge_kernel(ya_ref, yb_ref, yc_ref, wa_ref, wb_ref, wc_ref, ga_ref, gb_ref, gc_ref, o_ref):
    wa = wa_ref[...].astype(BF16)
    wb = wb_ref[...].astype(BF16)
    wc = wc_ref[...].astype(BF16)
    sub = MM_TM_HALF // 2
    for r in range(0, MM_TM_HALF, sub):
        rows = slice(r, r + sub)
        merged = ga_ref[rows, :] * jnp.dot(ya_ref[rows, :], wa, preferred_element_type=F32)
        merged = merged + gb_ref[rows, :] * jnp.dot(yb_ref[rows, :], wb, preferred_element_type=F32)
        merged = merged + gc_ref[rows, :] * jnp.dot(yc_ref[rows, :], wc, preferred_element_type=F32)
        o_ref[rows, :] = merged.astype(BF16)


def _merge(ya, yb, yc, w_branch, gates, l):
    tn = 256
    tm = MM_TM_HALF
    nb = D_MODEL // tn
    y_spec = pl.BlockSpec((tm, BRANCH_WIDTH), lambda i, j: (i, 0), pipeline_mode=pl.Buffered(1))

    def w_spec(b):
        return pl.BlockSpec((None, None, BRANCH_WIDTH, tn), lambda i, j: (l, b, 0, j))

    def g_spec(b):
        return pl.BlockSpec((tm, tn), lambda i, j: (i, b * nb + j))

    return pl.pallas_call(
        _merge_kernel,
        grid=(N_ALL // tm, nb),
        in_specs=[y_spec, y_spec, y_spec, w_spec(0), w_spec(1), w_spec(2),
                  g_spec(0), g_spec(1), g_spec(2)],
        out_specs=pl.BlockSpec((tm, tn), lambda i, j: (i, j)),
        out_shape=jax.ShapeDtypeStruct((N_ALL, D_MODEL), BF16),
        compiler_params=_cparams(("parallel", "parallel")),
        name="branch_merge",
    )(ya, yb, yc, w_branch, w_branch, w_branch, gates, gates, gates)


def _sgu_kernel(zu_ref, zv_ref, lg_ref, lb_ref, ws_ref, bst_ref, y_ref, vs_ref):
    i = pl.program_id(0)
    u = jax.nn.gelu(zu_ref[...], approximate=True)
    v = jax.nn.gelu(zv_ref[...], approximate=True)
    mu = jnp.mean(v, axis=-1, keepdims=True)
    vc = v - mu
    v = vc * lax.rsqrt(jnp.mean(vc * vc, axis=-1, keepdims=True) + EPS) * lg_ref[...] + lb_ref[...]
    is_sample = i == N_PROMPT // ROW_TILE

    @pl.when(jnp.logical_not(is_sample))
    def _():
        n_idx = lax.broadcasted_iota(jnp.int32, (SGU_CHUNK, SGU_CHUNK), 0)
        m_idx = lax.broadcasted_iota(jnp.int32, (SGU_CHUNK, SGU_CHUNK), 1)
        causal = m_idx <= n_idx
        for g in range(SGU_GROUPS):
            sl = slice(g * LANES, (g + 1) * LANES)
            wm = jnp.where(causal, ws_ref[g], 0.0).astype(BF16)
            s = jnp.dot(wm, v[:, sl].astype(BF16), preferred_element_type=F32) + bst_ref[:, g:g + 1]
            y_ref[:, sl] = (u[:, sl] * s).astype(BF16)

    @pl.when(is_sample)
    def _():
        vs_ref[...] = v
        for g in range(SGU_GROUPS):
            sl = slice(g * LANES, (g + 1) * LANES)
            s = v[:, sl] * ws_ref[g][0:1, 0:1] + bst_ref[0:1, g:g + 1]
            y_ref[:, sl] = (u[:, sl] * s).astype(BF16)


def _sgu(z, ln_g, ln_b, w_s, b_s_t, l):
    vec = pl.BlockSpec((1, BRANCH_WIDTH), lambda i: (0, 0))
    return pl.pallas_call(
        _sgu_kernel,
        grid=(N_ALL // ROW_TILE,),
        in_specs=[
            pl.BlockSpec((ROW_TILE, BRANCH_WIDTH), lambda i: (i, 0)),
            pl.BlockSpec((ROW_TILE, BRANCH_WIDTH), lambda i: (i, 1)),
            vec, vec,
            pl.BlockSpec((None, SGU_GROUPS, SGU_CHUNK, SGU_CHUNK), lambda i: (l, 0, 0, 0)),
            pl.BlockSpec((None, SGU_CHUNK, SGU_GROUPS), lambda i: (l, 0, 0)),
        ],
        out_specs=[
            pl.BlockSpec((ROW_TILE, BRANCH_WIDTH), lambda i: (i, 0)),
            pl.BlockSpec((DEC_BATCH, BRANCH_WIDTH), lambda i: (0, 0)),
        ],
        out_shape=[
            jax.ShapeDtypeStruct((N_ALL, BRANCH_WIDTH), BF16),
            jax.ShapeDtypeStruct((DEC_BATCH, BRANCH_WIDTH), F32),
        ],
        compiler_params=_cparams(("arbitrary",)),
        name="sgu",
    )(z, z, ln_g[l].reshape(1, BRANCH_WIDTH), ln_b[l].reshape(1, BRANCH_WIDTH), w_s, b_s_t)


def _hgrn_lower_bound(lb_ref, l):
    raw = lb_ref[...]
    e = jnp.exp(raw - jnp.max(raw, axis=0, keepdims=True))
    p = e / jnp.sum(e, axis=0, keepdims=True)
    acc = p[0:1, :]
    for j in range(1, l + 1):
        acc = acc + p[j:j + 1, :]
    return jnp.maximum(acc - p[0:1, :], 0.0)


def _hgrn_prep(zq, zf, lb_ref, l):
    e = jnp.exp(-jnp.abs(zf))
    ope = 1.0 + e
    sig_neg = jnp.where(zf >= 0.0, e, 1.0) / ope
    log_sig = jnp.minimum(zf, 0.0) - jnp.log(ope)
    if l == 0:
        return _silu(zq), sig_neg, log_sig
    lb = _hgrn_lower_bound(lb_ref, l)
    a = jnp.log(lb)
    c = jnp.log1p(-lb) + log_sig
    log_f = jnp.maximum(a, c) + jnp.log(1.0 + jnp.exp(-jnp.abs(a - c)))
    return _silu(zq), (1.0 - lb) * sig_neg, log_f


def _gla_prep(zq, zk, glr, wup_ref, bup_ref):
    x = jnp.dot(glr[:, :GLA_RANK].astype(BF16), wup_ref[...].astype(BF16),
                preferred_element_type=F32) + bup_ref[...]
    log_a = _log_sigmoid(x) / GLA_TAU
    return zq, zk * (GLA_DK ** -0.5), log_a


def _lane_bcast_col(row):
    return jnp.broadcast_to(row, (LANES, LANES)).T


def _head_out(o, g_row, gate):
    o = o * lax.rsqrt(jnp.mean(o * o, axis=-1, keepdims=True) + EPS) * g_row
    return o * _silu(gate)


def _finish_head(h, o_intra, q_dec, k_dec, b_last, vh, gate, gnorm_ref, s_old, s_ref, y_ref, dv):
    vs = slice(h * dv, (h + 1) * dv)
    o = o_intra + jnp.dot(q_dec.astype(BF16), s_old.astype(BF16), preferred_element_type=F32)
    y_ref[:, vs] = _head_out(o, gnorm_ref[...], gate[:, vs]).astype(BF16)
    upd = jnp.dot(k_dec.T.astype(BF16), vh, preferred_element_type=F32)
    decay = _lane_bcast_col(jnp.exp(b_last))
    for j in range(dv // LANES):
        ls = slice(j * LANES, (j + 1) * LANES)
        s_ref[h, :, ls] = decay * s_old[:, ls] + upd[:, ls]


def _prompt_chunk(q, k, v, log_f, gate, gnorm_ref, y_ref, so_ref, s_ref, sp_ref, q_ref, b_ref, oi_ref,
                  *, heads, dk, dv):
    c = pl.program_id(1)
    n_chunks = pl.num_programs(1)

    @pl.when(c == 0)
    def _():
        s_ref[...] = jnp.zeros_like(s_ref)

    t_idx = lax.broadcasted_iota(jnp.int32, (REC_CHUNK, REC_CHUNK), 0)
    s_idx = lax.broadcasted_iota(jnp.int32, (REC_CHUNK, REC_CHUNK), 1)
    tril = jnp.where(s_idx <= t_idx, 1.0, 0.0).astype(F32)
    b_all = jnp.dot(tril, log_f, precision=lax.Precision.HIGHEST, preferred_element_type=F32)
    v16 = v.astype(BF16)

    mid = HALF // 2 - 1

    def drop(lo, hi):
        return b_all[lo:lo + 1] - b_all[hi:hi + 1]

    worst = jnp.maximum(jnp.maximum(drop(0, mid), drop(mid, HALF - 1)),
                        jnp.maximum(drop(HALF, HALF + mid), drop(HALF + mid, REC_CHUNK - 1)))
    unsafe = jnp.max(worst) > MAX_SPLIT_EXPONENT

    causal_half = (lax.broadcasted_iota(jnp.int32, (HALF, HALF), 1)
                   <= lax.broadcasted_iota(jnp.int32, (HALF, HALF), 0))
    for h in range(heads):
        ks = slice(h * dk, (h + 1) * dk)
        qh, kh, bh = q[:, ks], k[:, ks], b_all[:, ks]
        vh = v16[:, h * dv:(h + 1) * dv]
        b_a, b_b = bh[:HALF], bh[HALF:]
        v_a, v_b = vh[:HALF], vh[HALF:]
        r_a = b_a[mid:mid + 1]
        r_b = b_b[mid:mid + 1]
        r_m = b_a[HALF - 1:HALF]
        b_last = b_b[HALF - 1:HALF]
        q_a = qh[:HALF] * jnp.exp(b_a - r_a)
        q_b = qh[HALF:] * jnp.exp(b_b - r_b)
        k_a = kh[:HALF] * jnp.exp(r_a - b_a)
        k_b = kh[HALF:] * jnp.exp(r_b - b_b)
        att_aa = jnp.where(causal_half, _nt(q_a.astype(BF16), k_a.astype(BF16)), 0.0).astype(BF16)
        att_bb = jnp.where(causal_half, _nt(q_b.astype(BF16), k_b.astype(BF16)), 0.0).astype(BF16)
        att_ba = _nt((q_b * jnp.exp(r_b - r_m)).astype(BF16),
                     (k_a * jnp.exp(r_m - r_a)).astype(BF16)).astype(BF16)
        o_a = jnp.dot(att_aa, v_a, preferred_element_type=F32)
        o_b = (jnp.dot(att_ba, v_a, preferred_element_type=F32)
               + jnp.dot(att_bb, v_b, preferred_element_type=F32))
        q_dec = jnp.concatenate([q_a * jnp.exp(r_a), q_b * jnp.exp(r_b)], axis=0)
        k_dec = jnp.concatenate([k_a * jnp.exp(b_last - r_a), k_b * jnp.exp(b_last - r_b)], axis=0)
        s_old = s_ref[h]
        sp_ref[h] = s_old
        _finish_head(h, jnp.concatenate([o_a, o_b], axis=0), q_dec, k_dec, b_last, vh, gate,
                     gnorm_ref, s_old, s_ref, y_ref, dv)

    @pl.when(unsafe)
    def _():
        q_ref[...] = q
        b_ref[...] = b_all
        s_pos = lax.broadcasted_iota(jnp.int32, (REC_CHUNK, 1), 0)
        for h in range(heads):
            ks = slice(h * dk, (h + 1) * dk)
            vs = slice(h * dv, (h + 1) * dv)
            qh, kh, bh = q[:, ks], k[:, ks], b_all[:, ks]
            vh = v[:, vs]

            def row(t, carry):
                b_t = b_ref[pl.ds(t, 1), :][:, ks]
                q_t = q_ref[pl.ds(t, 1), :][:, ks]
                p = jnp.exp(jnp.minimum(b_t - bh, 0.0)) * kh * q_t
                w = jnp.where(s_pos <= t, jnp.sum(p, axis=1, keepdims=True), 0.0)
                oi_ref[h, pl.ds(t, 1), :] = jnp.sum(w * vh, axis=0, keepdims=True)
                return carry

            lax.fori_loop(0, REC_CHUNK, row, 0)
            b_last = bh[REC_CHUNK - 1:REC_CHUNK]
            _finish_head(h, oi_ref[h], qh * jnp.exp(bh), kh * jnp.exp(b_last - bh), b_last,
                         v16[:, vs], gate, gnorm_ref, sp_ref[h], s_ref, y_ref, dv)

    @pl.when(c == n_chunks - 1)
    def _():
        so_ref[...] = s_ref[...]


def _hgrn_prompt_kernel(zq_ref, zf_ref, zi_ref, zg_ref, lb_ref, gnorm_ref, *out_and_scratch, l):
    q, k, log_f = _hgrn_prep(zq_ref[...], zf_ref[...], lb_ref, l)
    _prompt_chunk(q, k, zi_ref[...], log_f, zg_ref[...], gnorm_ref, *out_and_scratch,
                  heads=HG_HEADS, dk=HG_DK, dv=HG_DV)


def _gla_prompt_kernel(zq_ref, zk_ref, zv_ref, zr_ref, glr_ref, wup_ref, bup_ref, gnorm_ref,
                       *out_and_scratch):
    q, k, log_a = _gla_prep(zq_ref[...], zk_ref[...], glr_ref[...], wup_ref, bup_ref)
    _prompt_chunk(q, k, zv_ref[...], log_a, zr_ref[...], gnorm_ref, *out_and_scratch,
                  heads=GLA_HEADS, dk=GLA_DK, dv=GLA_DV)


def _zcol(width, block):
    n_c = SEQ // REC_CHUNK
    return pl.BlockSpec((REC_CHUNK, width), lambda b, c: (b * n_c + c, block))


def _rec_prompt_call(kernel, in_specs, args, heads, dk, dv, name):
    n_c = SEQ // REC_CHUNK
    return pl.pallas_call(
        kernel,
        grid=(BATCH, n_c),
        in_specs=in_specs,
        out_specs=[
            pl.BlockSpec((REC_CHUNK, BRANCH_WIDTH), lambda b, c: (b * n_c + c, 0)),
            pl.BlockSpec((None, heads, dk, dv), lambda b, c: (b, 0, 0, 0)),
        ],
        out_shape=[
            jax.ShapeDtypeStruct((N_ALL, BRANCH_WIDTH), BF16),
            jax.ShapeDtypeStruct((BATCH, heads, dk, dv), F32),
        ],
        scratch_shapes=[
            pltpu.VMEM((heads, dk, dv), F32),
            pltpu.VMEM((heads, dk, dv), F32),
            pltpu.VMEM((REC_CHUNK, heads * dk), F32),
            pltpu.VMEM((REC_CHUNK, heads * dk), F32),
            pltpu.VMEM((heads, REC_CHUNK, dv), F32),
        ],
        compiler_params=_cparams(("parallel", "arbitrary")),
        name=name,
    )(*args)


def _hgrn_prompt(z, hg_lb, g_norm, l):
    w = BRANCH_WIDTH
    in_specs = [_zcol(w, 2), _zcol(w, 3), _zcol(w, 4), _zcol(w, 5),
                pl.BlockSpec((DEPTH, w), lambda b, c: (0, 0)),
                pl.BlockSpec((1, HG_DV), lambda b, c: (0, 0))]
    args = [z, z, z, z, hg_lb, g_norm[l].reshape(1, HG_DV)]
    return _rec_prompt_call(functools.partial(_hgrn_prompt_kernel, l=l), in_specs, args,
                            HG_HEADS, HG_DK, HG_DV, "hgrn_prompt")


def _gla_prompt(z, z_glr, w_up, b_up, g_norm, l):
    w = BRANCH_WIDTH
    kw = GLA_DK_TOTAL
    in_specs = [_zcol(kw, 6 * w // kw), _zcol(kw, 6 * w // kw + 1), _zcol(w, 7), _zcol(w, 8),
                _zcol(LANES, 0),
                pl.BlockSpec((None, GLA_RANK, kw), lambda b, c: (l, 0, 0)),
                pl.BlockSpec((1, kw), lambda b, c: (0, 0)),
                pl.BlockSpec((1, GLA_DV), lambda b, c: (0, 0))]
    args = [z, z, z, z, z_glr, w_up, b_up[l].reshape(1, kw), g_norm[l].reshape(1, GLA_DV)]
    return _rec_prompt_call(_gla_prompt_kernel, in_specs, args,
                            GLA_HEADS, GLA_DK, GLA_DV, "gla_prompt")


def _sample_step(q, k, v, log_f, gate, gnorm_ref, s_ref, so_ref, i, *, heads, dk, dv):
    outs = []
    for h in range(heads):
        ks = slice(h * dk, (h + 1) * dk)
        f_col = _lane_bcast_col(jnp.exp(log_f[:, ks]))
        k_col = _lane_bcast_col(k[:, ks])
        q16 = jnp.broadcast_to(q[:, ks], (16, dk)).astype(BF16)
        parts = []
        for j in range(dv // LANES):
            ls = slice(j * LANES, (j + 1) * LANES)
            v_row = v[:, h * dv + j * LANES:h * dv + (j + 1) * LANES]
            s_new = f_col * s_ref[i, h, :, ls] + k_col * v_row
            so_ref[i, h, :, ls] = s_new
            parts.append(jnp.dot(q16, s_new.astype(BF16), preferred_element_type=F32)[0:1])
        o = parts[0] if len(parts) == 1 else jnp.concatenate(parts, axis=1)
        outs.append(_head_out(o, gnorm_ref[...], gate[:, h * dv:(h + 1) * dv]))
    return jnp.concatenate(outs, axis=1)


def _flush_sample_rows(y_ref, yacc_ref):
    @pl.when(pl.program_id(0) == pl.num_programs(0) - 1)
    def _():
        y_ref[...] = yacc_ref[...].astype(BF16)


def _hgrn_sample_kernel(zq_ref, zf_ref, zi_ref, zg_ref, lb_ref, gnorm_ref, s_ref, *rest, l):
    y_ref, so_ref, yacc_ref = rest[-3:]
    base = pl.program_id(0) * SAMPLE_NB

    def body(i, carry):
        row = pl.ds(i, 1)
        q, k, log_f = _hgrn_prep(zq_ref[row, :], zf_ref[row, :], lb_ref, l)
        yacc_ref[pl.ds(base + i, 1), :] = _sample_step(
            q, k, zi_ref[row, :], log_f, zg_ref[row, :], gnorm_ref, s_ref, so_ref, i,
            heads=HG_HEADS, dk=HG_DK, dv=HG_DV)
        return carry

    lax.fori_loop(0, SAMPLE_NB, body, 0)
    _flush_sample_rows(y_ref, yacc_ref)


def _gla_sample_kernel(zq_ref, zk_ref, zv_ref, zr_ref, glr_ref, wup_ref, bup_ref, gnorm_ref, s_ref,
                       *rest):
    y_ref, so_ref, yacc_ref = rest[-3:]
    base = pl.program_id(0) * SAMPLE_NB

    def body(i, carry):
        row = pl.ds(i, 1)
        glr = jnp.broadcast_to(glr_ref[row, :], (16, LANES))
        q, k, log_a = _gla_prep(zq_ref[row, :], zk_ref[row, :], glr, wup_ref, bup_ref)
        yacc_ref[pl.ds(base + i, 1), :] = _sample_step(
            q, k, zv_ref[row, :], log_a[0:1], zr_ref[row, :], gnorm_ref, s_ref, so_ref, i,
            heads=GLA_HEADS, dk=GLA_DK, dv=GLA_DV)
        return carry

    lax.fori_loop(0, SAMPLE_NB, body, 0)
    _flush_sample_rows(y_ref, yacc_ref)


def _zrow(width, block):
    return pl.BlockSpec((SAMPLE_NB, width), lambda i: (N_PROMPT // SAMPLE_NB + i, block))


def _rec_sample_call(kernel, in_specs, args, state, y_all, new_state, l, heads, dk, dv, name):
    st = pl.BlockSpec((None, SAMPLE_NB, heads, dk, dv), lambda i: (l, i, 0, 0, 0))
    in_specs = in_specs + [st, pl.BlockSpec(memory_space=pl.ANY)]
    args = list(args) + [state, y_all]
    aliases = {len(args) - 1: 0}
    if new_state is not None:
        aliases[len(args)] = 1
        in_specs = in_specs + [pl.BlockSpec(memory_space=pl.ANY)]
        args = args + [new_state]
    return pl.pallas_call(
        kernel,
        grid=(DEC_BATCH // SAMPLE_NB,),
        in_specs=in_specs,
        out_specs=[pl.BlockSpec((DEC_BATCH, BRANCH_WIDTH), lambda i: (N_PROMPT // DEC_BATCH, 0)), st],
        out_shape=[
            jax.ShapeDtypeStruct((N_ALL, BRANCH_WIDTH), BF16),
            jax.ShapeDtypeStruct((DEPTH, DEC_BATCH, heads, dk, dv), F32),
        ],
        scratch_shapes=[pltpu.VMEM((DEC_BATCH, BRANCH_WIDTH), F32)],
        input_output_aliases=aliases,
        compiler_params=_cparams(("arbitrary",)),
        name=name,
    )(*args)


def _hgrn_sample(z, hg_lb, g_norm, state, y_all, new_state, l):
    w = BRANCH_WIDTH
    in_specs = [_zrow(w, 2), _zrow(w, 3), _zrow(w, 4), _zrow(w, 5),
                pl.BlockSpec((DEPTH, w), lambda i: (0, 0)),
                pl.BlockSpec((1, HG_DV), lambda i: (0, 0))]
    args = [z, z, z, z, hg_lb, g_norm[l].reshape(1, HG_DV)]
    return _rec_sample_call(functools.partial(_hgrn_sample_kernel, l=l), in_specs, args, state,
                            y_all, new_state, l, HG_HEADS, HG_DK, HG_DV, "hgrn_sample")


def _gla_sample(z, z_glr, w_up, b_up, g_norm, state, y_all, new_state, l):
    w = BRANCH_WIDTH
    kw = GLA_DK_TOTAL
    in_specs = [_zrow(kw, 6 * w // kw), _zrow(kw, 6 * w // kw + 1), _zrow(w, 7), _zrow(w, 8),
                _zrow(LANES, 0),
                pl.BlockSpec((None, GLA_RANK, kw), lambda i: (l, 0, 0)),
                pl.BlockSpec((1, kw), lambda i: (0, 0)),
                pl.BlockSpec((1, GLA_DV), lambda i: (0, 0))]
    args = [z, z, z, z, z_glr, w_up, b_up[l].reshape(1, kw), g_norm[l].reshape(1, GLA_DV)]
    return _rec_sample_call(_gla_sample_kernel, in_specs, args, state, y_all, new_state, l,
                            GLA_HEADS, GLA_DK, GLA_DV, "gla_sample")


def kernel(x_prompt, x_sample, state_hgrn, state_gla, c_prompt, c_sample, w_ada, b_ada, g_pre_mix, g_post_mix, g_pre_mlp, g_post_mlp, w_in, sgu_ln_g, sgu_ln_b, sgu_w_s, sgu_b_s, hg_lb, hg_norm_g, gla_w_up, gla_b_up, gla_norm_g, w_branch, w_out, w_mlp_up, w_mlp_down):
    x = jnp.concatenate([x_prompt.reshape(N_PROMPT, D_MODEL), x_sample.reshape(DEC_BATCH, D_MODEL)], axis=0)
    c_all = jnp.concatenate([c_sample, c_prompt, jnp.zeros((MOD_ROWS - DEC_BATCH - BATCH, D_MODEL), F32)], axis=0)
    mod = _modulation(c_all, w_ada, b_ada)
    w_in_t = jnp.swapaxes(w_in, 1, 2)
    b_s_t = jnp.swapaxes(sgu_b_s, 1, 2)

    hg_p, gla_p, v_rows = [], [], []
    hg_s = gla_s = None
    h = _prenorm(x, g_pre_mix[0], mod, 0, 0, 1)
    for l in range(DEPTH):
        z = _matmul(h, w_in_t, l, n_out=Z_MAIN, w_transposed=True, name="in_proj")
        z_glr = _matmul(h, w_in_t, l, n_out=LANES, col_off=OFF_GLR, tn=LANES, w_transposed=True,
                        name="in_proj_lowrank")
        gates = _matmul(h, w_in_t, l, n_out=N_BRANCH * D_MODEL, col_off=OFF_GATE, w_transposed=True,
                        act=_sigmoid, out_dtype=BF16, name="in_proj_gate")

        y_a, v_s = _sgu(z, sgu_ln_g, sgu_ln_b, sgu_w_s, b_s_t, l)
        y_b, s_hg_p = _hgrn_prompt(z, hg_lb, hg_norm_g, l)
        y_b, hg_s = _hgrn_sample(z, hg_lb, hg_norm_g, state_hgrn, y_b, hg_s, l)
        y_c, s_gla_p = _gla_prompt(z, z_glr, gla_w_up, gla_b_up, gla_norm_g, l)
        y_c, gla_s = _gla_sample(z, z_glr, gla_w_up, gla_b_up, gla_norm_g, state_gla, y_c, gla_s, l)

        merged = _merge(y_a, y_b, y_c, w_branch, gates, l)
        out = _matmul(merged, w_out, l, n_out=D_MODEL, name="out_proj")
        x, h2 = _resid(x, out, g_post_mix[l], mod, l, 2, nxt=(g_pre_mlp[l], l, 3, 4))

        up = _matmul(h2, w_mlp_up, l, n_out=D_FF, act=_act_relu2, out_dtype=BF16, name="mlp_up")
        y2 = _matmul(up, w_mlp_down, l, n_out=D_MODEL, tm=MM_TM_HALF, tn=1024, tk=1024,
                     row_sub=MM_TM_HALF // 2, name="mlp_down")
        if l + 1 < DEPTH:
            x, h = _resid(x, y2, g_post_mlp[l], mod, l, 5, nxt=(g_pre_mix[l + 1], l + 1, 0, 1))
        else:
            (x,) = _resid(x, y2, g_post_mlp[l], mod, l, 5)

        hg_p.append(s_hg_p)
        gla_p.append(s_gla_p)
        v_rows.append(v_s.reshape(DEC_BATCH, 1, BRANCH_WIDTH))

    y_prompt = x[:N_PROMPT].reshape(BATCH, SEQ, D_MODEL)
    y_sample = x[N_PROMPT:].reshape(DEC_BATCH, 1, D_MODEL)
    return (y_prompt, y_sample, jnp.stack(hg_p), jnp.stack(gla_p), hg_s, gla_s, jnp.stack(v_rows))
```

```python
import functools

import jax
import jax.numpy as jnp
from jax import lax
from jax.experimental import pallas as pl
from jax.experimental.pallas import tpu as pltpu

F32 = jnp.float32
BF16 = jnp.bfloat16

D_MODEL = 2048
BATCH = 4
SEQ = 2048
DEPTH = 4
DEC_BATCH = 128
BRANCH_WIDTH = D_MODEL // 2
N_BRANCH = 3
SGU_GROUPS = 8
SGU_CHUNK = 128
HG_HEADS = 8
HG_DK = BRANCH_WIDTH // HG_HEADS
HG_DV = BRANCH_WIDTH // HG_HEADS
GLA_HEADS = 4
GLA_DK_TOTAL = BRANCH_WIDTH // 2
GLA_DK = GLA_DK_TOTAL // GLA_HEADS
GLA_DV = BRANCH_WIDTH // GLA_HEADS
GLA_RANK = 16
GLA_TAU = 16.0
D_FF = 4 * D_MODEL
EPS = 1e-6

N_PROMPT = BATCH * SEQ
N_ALL = N_PROMPT + DEC_BATCH
MOD_ROWS = DEC_BATCH + 8

OFF_GLR = 2 * BRANCH_WIDTH + 4 * BRANCH_WIDTH + 2 * GLA_DK_TOTAL + 2 * BRANCH_WIDTH
OFF_GATE = OFF_GLR + GLA_RANK
Z_MAIN = OFF_GLR

LANES = 128
ROW_TILE = 128
NORM_TILE = 640
MAX_SPLIT_EXPONENT = 60.0
MM_TM = 4160
MM_ROW_SUB = 832
MM_TM_HALF = 2080
REC_CHUNK = 128
HALF = REC_CHUNK // 2
SAMPLE_NB = 8
GATE_TN = 256
VMEM_LIMIT = 56 * 1024 * 1024


def _cparams(sem):
    return pltpu.CompilerParams(dimension_semantics=sem, vmem_limit_bytes=VMEM_LIMIT)


def _sigmoid(x):
    return 1.0 / (1.0 + jnp.exp(-x))


def _silu(x):
    return x * _sigmoid(x)


def _log_sigmoid(x):
    return jnp.minimum(x, 0.0) - jnp.log(1.0 + jnp.exp(-jnp.abs(x)))


def _rms(x, g):
    return x * lax.rsqrt(jnp.mean(x * x, axis=-1, keepdims=True) + EPS) * g


def _mod_kernel(c_ref, w_ref, b_ref, o_ref):
    a = _silu(c_ref[...]).astype(BF16)
    o_ref[...] = jnp.dot(a, w_ref[...].astype(BF16), preferred_element_type=F32) + b_ref[...]


def _modulation(c_all, w_ada, b_ada):
    tn = 1024
    n = 6 * D_MODEL
    return pl.pallas_call(
        _mod_kernel,
        grid=(DEPTH, n // tn),
        in_specs=[
            pl.BlockSpec((MOD_ROWS, D_MODEL), lambda l, j: (0, 0)),
            pl.BlockSpec((None, D_MODEL, tn), lambda l, j: (l, 0, j)),
            pl.BlockSpec((None, 1, tn), lambda l, j: (l, 0, j)),
        ],
        out_specs=pl.BlockSpec((None, MOD_ROWS, tn), lambda l, j: (l, 0, j)),
        out_shape=jax.ShapeDtypeStruct((DEPTH, MOD_ROWS, n), F32),
        compiler_params=_cparams(("parallel", "parallel")),
        name="adaln_mod",
    )(c_all, w_ada, b_ada.reshape(DEPTH, 1, n))


def _mod_specs(l, j):
    return [
        pl.BlockSpec((None, DEC_BATCH, D_MODEL), lambda i: (l, 0, j)),
        pl.BlockSpec((None, 8, D_MODEL), lambda i: (l, DEC_BATCH // 8, j)),
    ]


def _pick_mod(s_ref, p_ref, sub):
    g = pl.program_id(0) * (NORM_TILE // ROW_TILE) + sub
    b = jnp.minimum(g // (SEQ // ROW_TILE), BATCH - 1)
    is_sample = g >= N_PROMPT // ROW_TILE
    return jnp.where(is_sample, s_ref[...], p_ref[pl.ds(b, 1), :])


def _prenorm_kernel(x_ref, g_ref, sc_s, sc_p, sh_s, sh_p, h_ref):
    for sub in range(NORM_TILE // ROW_TILE):
        rows = slice(sub * ROW_TILE, (sub + 1) * ROW_TILE)
        h = (_rms(x_ref[rows, :], g_ref[...]) * (1.0 + _pick_mod(sc_s, sc_p, sub))
             + _pick_mod(sh_s, sh_p, sub))
        h_ref[rows, :] = h.astype(BF16)


def _prenorm(x, g, mod, l, j_shift, j_scale):
    row = pl.BlockSpec((NORM_TILE, D_MODEL), lambda i: (i, 0))
    vec = pl.BlockSpec((1, D_MODEL), lambda i: (0, 0))
    return pl.pallas_call(
        _prenorm_kernel,
        grid=(N_ALL // NORM_TILE,),
        in_specs=[row, vec] + _mod_specs(l, j_scale) + _mod_specs(l, j_shift),
        out_specs=row,
        out_shape=jax.ShapeDtypeStruct((N_ALL, D_MODEL), BF16),
        compiler_params=_cparams(("parallel",)),
        name="prenorm",
    )(x, g.reshape(1, D_MODEL), mod, mod, mod, mod)


def _resid_kernel(x_ref, y_ref, gpost_ref, gt_s, gt_p, *rest, with_next):
    for sub in range(NORM_TILE // ROW_TILE):
        rows = slice(sub * ROW_TILE, (sub + 1) * ROW_TILE)
        x_new = x_ref[rows, :] + _pick_mod(gt_s, gt_p, sub) * _rms(y_ref[rows, :], gpost_ref[...])
        if with_next:
            gpre_ref, sc_s, sc_p, sh_s, sh_p, xo_ref, h_ref = rest
            xo_ref[rows, :] = x_new
            h = (_rms(x_new, gpre_ref[...]) * (1.0 + _pick_mod(sc_s, sc_p, sub))
                 + _pick_mod(sh_s, sh_p, sub))
            h_ref[rows, :] = h.astype(BF16)
        else:
            (xo_ref,) = rest
            xo_ref[rows, :] = x_new


def _resid(x, y, g_post, mod, l_gate, j_gate, nxt=None):
    row = pl.BlockSpec((NORM_TILE, D_MODEL), lambda i: (i, 0))
    vec = pl.BlockSpec((1, D_MODEL), lambda i: (0, 0))
    in_specs = [row, row, vec] + _mod_specs(l_gate, j_gate)
    args = [x, y, g_post.reshape(1, D_MODEL), mod, mod]
    out_specs = [row]
    out_shape = [jax.ShapeDtypeStruct((N_ALL, D_MODEL), F32)]
    if nxt is not None:
        g_pre, l_n, j_shift, j_scale = nxt
        in_specs += [vec] + _mod_specs(l_n, j_scale) + _mod_specs(l_n, j_shift)
        args += [g_pre.reshape(1, D_MODEL), mod, mod, mod, mod]
        out_specs.append(row)
        out_shape.append(jax.ShapeDtypeStruct((N_ALL, D_MODEL), BF16))
    return pl.pallas_call(
        functools.partial(_resid_kernel, with_next=nxt is not None),
        grid=(N_ALL // NORM_TILE,),
        in_specs=in_specs,
        out_specs=out_specs,
        out_shape=out_shape,
        compiler_params=_cparams(("parallel",)),
        name="resid_norm",
    )(*args)


def _act_none(x):
    return x


def _act_relu2(x):
    return jnp.square(jnp.maximum(x, 0.0))


def _nt(a, b):
    return lax.dot_general(a, b, (((1,), (1,)), ((), ())), preferred_element_type=F32)


def _mm_kernel(a_ref, w_ref, o_ref, *, nk, act, w_transposed, row_sub):
    w = w_ref[...].astype(BF16)
    if nk > 1:
        @pl.when(pl.program_id(2) == 0)
        def _():
            o_ref[...] = jnp.zeros_like(o_ref)

    for r in range(0, a_ref.shape[0], row_sub):
        rows = slice(r, r + row_sub)
        a = a_ref[rows, :]
        part = _nt(a, w) if w_transposed else jnp.dot(a, w, preferred_element_type=F32)
        if nk == 1:
            o_ref[rows, :] = act(part).astype(o_ref.dtype)
        else:
            o_ref[rows, :] += part


def _matmul(a, w, l, *, n_out, col_off=0, tm=MM_TM, tn=256, tk=None, row_sub=MM_ROW_SUB,
            act=_act_none, out_dtype=F32, w_transposed=False, name):
    k_dim = a.shape[1]
    tk = k_dim if tk is None else tk
    nk = k_dim // tk
    assert nk == 1 or (act is _act_none and out_dtype == F32)
    a_spec = pl.BlockSpec((tm, tk), lambda i, j, k: (i, k))
    if w_transposed:
        assert col_off % 8 == 0 and tn % 8 == 0 and tk % LANES == 0
        w_spec = pl.BlockSpec((None, pl.Element(tn), pl.Element(tk)),
                              lambda i, j, k: (l, pl.multiple_of(col_off + j * tn, 8),
                                               pl.multiple_of(k * tk, LANES)))
    else:
        w_spec = pl.BlockSpec((None, tk, tn), lambda i, j, k: (l, k, col_off // tn + j))
    return pl.pallas_call(
        functools.partial(_mm_kernel, nk=nk, act=act, w_transposed=w_transposed, row_sub=row_sub),
        grid=(N_ALL // tm, n_out // tn, nk),
        in_specs=[a_spec, w_spec],
        out_specs=pl.BlockSpec((tm, tn), lambda i, j, k: (i, j)),
        out_shape=jax.ShapeDtypeStruct((N_ALL, n_out), out_dtype),
        compiler_params=_cparams(("parallel", "parallel", "arbitrary")),
        name=name,
    )(a, w)


def _merge_kernel(ya_ref, yb_ref, yc_ref, wa_ref, wb_ref, wc_ref, ga_ref, gb_ref, gc_ref, o_ref):
    wa = wa_ref[...].astype(BF16)
    wb = wb_ref[...].astype(BF16)
    wc = wc_ref[...].astype(BF16)
    sub = MM_TM_HALF // 2
    for r in range(0, MM_TM_HALF, sub):
        rows = slice(r, r + sub)
        merged = ga_ref[rows, :] * jnp.dot(ya_ref[rows, :], wa, preferred_element_type=F32)
        merged = merged + gb_ref[rows, :] * jnp.dot(yb_ref[rows, :], wb, preferred_element_type=F32)
        merged = merged + gc_ref[rows, :] * jnp.dot(yc_ref[rows, :], wc, preferred_element_type=F32)
        o_ref[rows, :] = merged.astype(BF16)


def _merge(ya, yb, yc, w_branch, gates, l):
    tn = 256
    tm = MM_TM_HALF
    nb = D_MODEL // tn
    y_spec = pl.BlockSpec((tm, BRANCH_WIDTH), lambda i, j: (i, 0), pipeline_mode=pl.Buffered(1))

    def w_spec(b):
        return pl.BlockSpec((None, None, BRANCH_WIDTH, tn), lambda i, j: (l, b, 0, j))

    def g_spec(b):
        return pl.BlockSpec((tm, tn), lambda i, j: (i, b * nb + j))

    return pl.pallas_call(
        _merge_kernel,
        grid=(N_ALL // tm, nb),
        in_specs=[y_spec, y_spec, y_spec, w_spec(0), w_spec(1), w_spec(2),
                  g_spec(0), g_spec(1), g_spec(2)],
        out_specs=pl.BlockSpec((tm, tn), lambda i, j: (i, j)),
        out_shape=jax.ShapeDtypeStruct((N_ALL, D_MODEL), BF16),
        compiler_params=_cparams(("parallel", "parallel")),
        name="branch_merge",
    )(ya, yb, yc, w_branch, w_branch, w_branch, gates, gates, gates)


def _sgu_kernel(zu_ref, zv_ref, lg_ref, lb_ref, ws_ref, bst_ref, y_ref, vs_ref):
    i = pl.program_id(0)
    u = jax.nn.gelu(zu_ref[...], approximate=True)
    v = jax.nn.gelu(zv_ref[...], approximate=True)
    mu = jnp.mean(v, axis=-1, keepdims=True)
    vc = v - mu
    v = vc * lax.rsqrt(jnp.mean(vc * vc, axis=-1, keepdims=True) + EPS) * lg_ref[...] + lb_ref[...]
    is_sample = i == N_PROMPT // ROW_TILE

    @pl.when(jnp.logical_not(is_sample))
    def _():
        n_idx = lax.broadcasted_iota(jnp.int32, (SGU_CHUNK, SGU_CHUNK), 0)
        m_idx = lax.broadcasted_iota(jnp.int32, (SGU_CHUNK, SGU_CHUNK), 1)
        causal = m_idx <= n_idx
        for g in range(SGU_GROUPS):
            sl = slice(g * LANES, (g + 1) * LANES)
            wm = jnp.where(causal, ws_ref[g], 0.0).astype(BF16)
            s = jnp.dot(wm, v[:, sl].astype(BF16), preferred_element_type=F32) + bst_ref[:, g:g + 1]
            y_ref[:, sl] = (u[:, sl] * s).astype(BF16)

    @pl.when(is_sample)
    def _():
        vs_ref[...] = v
        for g in range(SGU_GROUPS):
            sl = slice(g * LANES, (g + 1) * LANES)
            s = v[:, sl] * ws_ref[g][0:1, 0:1] + bst_ref[0:1, g:g + 1]
            y_ref[:, sl] = (u[:, sl] * s).astype(BF16)


def _sgu(z, ln_g, ln_b, w_s, b_s_t, l):
    vec = pl.BlockSpec((1, BRANCH_WIDTH), lambda i: (0, 0))
    return pl.pallas_call(
        _sgu_kernel,
        grid=(N_ALL // ROW_TILE,),
        in_specs=[
            pl.BlockSpec((ROW_TILE, BRANCH_WIDTH), lambda i: (i, 0)),
            pl.BlockSpec((ROW_TILE, BRANCH_WIDTH), lambda i: (i, 1)),
            vec, vec,
            pl.BlockSpec((None, SGU_GROUPS, SGU_CHUNK, SGU_CHUNK), lambda i: (l, 0, 0, 0)),
            pl.BlockSpec((None, SGU_CHUNK, SGU_GROUPS), lambda i: (l, 0, 0)),
        ],
        out_specs=[
            pl.BlockSpec((ROW_TILE, BRANCH_WIDTH), lambda i: (i, 0)),
            pl.BlockSpec((DEC_BATCH, BRANCH_WIDTH), lambda i: (0, 0)),
        ],
        out_shape=[
            jax.ShapeDtypeStruct((N_ALL, BRANCH_WIDTH), BF16),
            jax.ShapeDtypeStruct((DEC_BATCH, BRANCH_WIDTH), F32),
        ],
        compiler_params=_cparams(("arbitrary",)),
        name="sgu",
    )(z, z, ln_g[l].reshape(1, BRANCH_WIDTH), ln_b[l].reshape(1, BRANCH_WIDTH), w_s, b_s_t)


def _hgrn_lower_bound(lb_ref, l):
    raw = lb_ref[...]
    e = jnp.exp(raw - jnp.max(raw, axis=0, keepdims=True))
    p = e / jnp.sum(e, axis=0, keepdims=True)
    acc = p[0:1, :]
    for j in range(1, l + 1):
        acc = acc + p[j:j + 1, :]
    return jnp.maximum(acc - p[0:1, :], 0.0)


def _hgrn_prep(zq, zf, lb_ref, l):
    e = jnp.exp(-jnp.abs(zf))
    ope = 1.0 + e
    sig_neg = jnp.where(zf >= 0.0, e, 1.0) / ope
    log_sig = jnp.minimum(zf, 0.0) - jnp.log(ope)
    if l == 0:
        return _silu(zq), sig_neg, log_sig
    lb = _hgrn_lower_bound(lb_ref, l)
    a = jnp.log(lb)
    c = jnp.log1p(-lb) + log_sig
    log_f = jnp.maximum(a, c) + jnp.log(1.0 + jnp.exp(-jnp.abs(a - c)))
    return _silu(zq), (1.0 - lb) * sig_neg, log_f


def _gla_prep(zq, zk, glr, wup_ref, bup_ref):
    x = jnp.dot(glr[:, :GLA_RANK].astype(BF16), wup_ref[...].astype(BF16),
                preferred_element_type=F32) + bup_ref[...]
    log_a = _log_sigmoid(x) / GLA_TAU
    return zq, zk * (GLA_DK ** -0.5), log_a


def _lane_bcast_col(row):
    return jnp.broadcast_to(row, (LANES, LANES)).T


def _head_out(o, g_row, gate):
    o = o * lax.rsqrt(jnp.mean(o * o, axis=-1, keepdims=True) + EPS) * g_row
    return o * _silu(gate)


def _finish_head(h, o_intra, q_dec, k_dec, b_last, vh, gate, gnorm_ref, s_old, s_ref, y_ref, dv):
    vs = slice(h * dv, (h + 1) * dv)
    o = o_intra + jnp.dot(q_dec.astype(BF16), s_old.astype(BF16), preferred_element_type=F32)
    y_ref[:, vs] = _head_out(o, gnorm_ref[...], gate[:, vs]).astype(BF16)
    upd = jnp.dot(k_dec.T.astype(BF16), vh, preferred_element_type=F32)
    decay = _lane_bcast_col(jnp.exp(b_last))
    for j in range(dv // LANES):
        ls = slice(j * LANES, (j + 1) * LANES)
        s_ref[h, :, ls] = decay * s_old[:, ls] + upd[:, ls]


def _prompt_chunk(q, k, v, log_f, gate, gnorm_ref, y_ref, so_ref, s_ref, sp_ref, q_ref, b_ref, oi_ref,
                  *, heads, dk, dv):
    c = pl.program_id(1)
    n_chunks = pl.num_programs(1)

    @pl.when(c == 0)
    def _():
        s_ref[...] = jnp.zeros_like(s_ref)

    t_idx = lax.broadcasted_iota(jnp.int32, (REC_CHUNK, REC_CHUNK), 0)
    s_idx = lax.broadcasted_iota(jnp.int32, (REC_CHUNK, REC_CHUNK), 1)
    tril = jnp.where(s_idx <= t_idx, 1.0, 0.0).astype(F32)
    b_all = jnp.dot(tril, log_f, precision=lax.Precision.HIGHEST, preferred_element_type=F32)
    v16 = v.astype(BF16)

    mid = HALF // 2 - 1

    def drop(lo, hi):
        return b_all[lo:lo + 1] - b_all[hi:hi + 1]

    worst = jnp.maximum(jnp.maximum(drop(0, mid), drop(mid, HALF - 1)),
                        jnp.maximum(drop(HALF, HALF + mid), drop(HALF + mid, REC_CHUNK - 1)))
    unsafe = jnp.max(worst) > MAX_SPLIT_EXPONENT

    causal_half = (lax.broadcasted_iota(jnp.int32, (HALF, HALF), 1)
                   <= lax.broadcasted_iota(jnp.int32, (HALF, HALF), 0))
    for h in range(heads):
        ks = slice(h * dk, (h + 1) * dk)
        qh, kh, bh = q[:, ks], k[:, ks], b_all[:, ks]
        vh = v16[:, h * dv:(h + 1) * dv]
        b_a, b_b = bh[:HALF], bh[HALF:]
        v_a, v_b = vh[:HALF], vh[HALF:]
        r_a = b_a[mid:mid + 1]
        r_b = b_b[mid:mid + 1]
        r_m = b_a[HALF - 1:HALF]
        b_last = b_b[HALF - 1:HALF]
        q_a = qh[:HALF] * jnp.exp(b_a - r_a)
        q_b = qh[HALF:] * jnp.exp(b_b - r_b)
        k_a = kh[:HALF] * jnp.exp(r_a - b_a)
        k_b = kh[HALF:] * jnp.exp(r_b - b_b)
        att_aa = jnp.where(causal_half, _nt(q_a.astype(BF16), k_a.astype(BF16)), 0.0).astype(BF16)
        att_bb = jnp.where(causal_half, _nt(q_b.astype(BF16), k_b.astype(BF16)), 0.0).astype(BF16)
        att_ba = _nt((q_b * jnp.exp(r_b - r_m)).astype(BF16),
                     (k_a * jnp.exp(r_m - r_a)).astype(BF16)).astype(BF16)
        o_a = jnp.dot(att_aa, v_a, preferred_element_type=F32)
        o_b = (jnp.dot(att_ba, v_a, preferred_element_type=F32)
               + jnp.dot(att_bb, v_b, preferred_element_type=F32))
        q_dec = jnp.concatenate([q_a * jnp.exp(r_a), q_b * jnp.exp(r_b)], axis=0)
        k_dec = jnp.concatenate([k_a * jnp.exp(b_last - r_a), k_b * jnp.exp(b_last - r_b)], axis=0)
        s_old = s_ref[h]
        sp_ref[h] = s_old
        _finish_head(h, jnp.concatenate([o_a, o_b], axis=0), q_dec, k_dec, b_last, vh, gate,
                     gnorm_ref, s_old, s_ref, y_ref, dv)

    @pl.when(unsafe)
    def _():
        q_ref[...] = q
        b_ref[...] = b_all
        s_pos = lax.broadcasted_iota(jnp.int32, (REC_CHUNK, 1), 0)
        for h in range(heads):
            ks = slice(h * dk, (h + 1) * dk)
            vs = slice(h * dv, (h + 1) * dv)
            qh, kh, bh = q[:, ks], k[:, ks], b_all[:, ks]
            vh = v[:, vs]

            def row(t, carry):
                b_t = b_ref[pl.ds(t, 1), :][:, ks]
                q_t = q_ref[pl.ds(t, 1), :][:, ks]
                p = jnp.exp(jnp.minimum(b_t - bh, 0.0)) * kh * q_t
                w = jnp.where(s_pos <= t, jnp.sum(p, axis=1, keepdims=True), 0.0)
                oi_ref[h, pl.ds(t, 1), :] = jnp.sum(w * vh, axis=0, keepdims=True)
                return carry

            lax.fori_loop(0, REC_CHUNK, row, 0)
            b_last = bh[REC_CHUNK - 1:REC_CHUNK]
            _finish_head(h, oi_ref[h], qh * jnp.exp(bh), kh * jnp.exp(b_last - bh), b_last,
                         v16[:, vs], gate, gnorm_ref, sp_ref[h], s_ref, y_ref, dv)

    @pl.when(c == n_chunks - 1)
    def _():
        so_ref[...] = s_ref[...]


def _hgrn_prompt_kernel(zq_ref, zf_ref, zi_ref, zg_ref, lb_ref, gnorm_ref, *out_and_scratch, l):
    q, k, log_f = _hgrn_prep(zq_ref[...], zf_ref[...], lb_ref, l)
    _prompt_chunk(q, k, zi_ref[...], log_f, zg_ref[...], gnorm_ref, *out_and_scratch,
                  heads=HG_HEADS, dk=HG_DK, dv=HG_DV)


def _gla_prompt_kernel(zq_ref, zk_ref, zv_ref, zr_ref, glr_ref, wup_ref, bup_ref, gnorm_ref,
                       *out_and_scratch):
    q, k, log_a = _gla_prep(zq_ref[...], zk_ref[...], glr_ref[...], wup_ref, bup_ref)
    _prompt_chunk(q, k, zv_ref[...], log_a, zr_ref[...], gnorm_ref, *out_and_scratch,
                  heads=GLA_HEADS, dk=GLA_DK, dv=GLA_DV)


def _zcol(width, block):
    n_c = SEQ // REC_CHUNK
    return pl.BlockSpec((REC_CHUNK, width), lambda b, c: (b * n_c + c, block))


def _rec_prompt_call(kernel, in_specs, args, heads, dk, dv, name):
    n_c = SEQ // REC_CHUNK
    return pl.pallas_call(
        kernel,
        grid=(BATCH, n_c),
        in_specs=in_specs,
        out_specs=[
            pl.BlockSpec((REC_CHUNK, BRANCH_WIDTH), lambda b, c: (b * n_c + c, 0)),
            pl.BlockSpec((None, heads, dk, dv), lambda b, c: (b, 0, 0, 0)),
        ],
        out_shape=[
            jax.ShapeDtypeStruct((N_ALL, BRANCH_WIDTH), BF16),
            jax.ShapeDtypeStruct((BATCH, heads, dk, dv), F32),
        ],
        scratch_shapes=[
            pltpu.VMEM((heads, dk, dv), F32),
            pltpu.VMEM((heads, dk, dv), F32),
            pltpu.VMEM((REC_CHUNK, heads * dk), F32),
            pltpu.VMEM((REC_CHUNK, heads * dk), F32),
            pltpu.VMEM((heads, REC_CHUNK, dv), F32),
        ],
        compiler_params=_cparams(("parallel", "arbitrary")),
        name=name,
    )(*args)


def _hgrn_prompt(z, hg_lb, g_norm, l):
    w = BRANCH_WIDTH
    in_specs = [_zcol(w, 2), _zcol(w, 3), _zcol(w, 4), _zcol(w, 5),
                pl.BlockSpec((DEPTH, w), lambda b, c: (0, 0)),
                pl.BlockSpec((1, HG_DV), lambda b, c: (0, 0))]
    args = [z, z, z, z, hg_lb, g_norm[l].reshape(1, HG_DV)]
    return _rec_prompt_call(functools.partial(_hgrn_prompt_kernel, l=l), in_specs, args,
                            HG_HEADS, HG_DK, HG_DV, "hgrn_prompt")


def _gla_prompt(z, z_glr, w_up, b_up, g_norm, l):
    w = BRANCH_WIDTH
    kw = GLA_DK_TOTAL
    in_specs = [_zcol(kw, 6 * w // kw), _zcol(kw, 6 * w // kw + 1), _zcol(w, 7), _zcol(w, 8),
                _zcol(LANES, 0),
                pl.BlockSpec((None, GLA_RANK, kw), lambda b, c: (l, 0, 0)),
                pl.BlockSpec((1, kw), lambda b, c: (0, 0)),
                pl.BlockSpec((1, GLA_DV), lambda b, c: (0, 0))]
    args = [z, z, z, z, z_glr, w_up, b_up[l].reshape(1, kw), g_norm[l].reshape(1, GLA_DV)]
    return _rec_prompt_call(_gla_prompt_kernel, in_specs, args,
                            GLA_HEADS, GLA_DK, GLA_DV, "gla_prompt")


def _sample_step(q, k, v, log_f, gate, gnorm_ref, s_ref, so_ref, i, *, heads, dk, dv):
    outs = []
    for h in range(heads):
        ks = slice(h * dk, (h + 1) * dk)
        f_col = _lane_bcast_col(jnp.exp(log_f[:, ks]))
        k_col = _lane_bcast_col(k[:, ks])
        q16 = jnp.broadcast_to(q[:, ks], (16, dk)).astype(BF16)
        parts = []
        for j in range(dv // LANES):
            ls = slice(j * LANES, (j + 1) * LANES)
            v_row = v[:, h * dv + j * LANES:h * dv + (j + 1) * LANES]
            s_new = f_col * s_ref[i, h, :, ls] + k_col * v_row
            so_ref[i, h, :, ls] = s_new
            parts.append(jnp.dot(q16, s_new.astype(BF16), preferred_element_type=F32)[0:1])
        o = parts[0] if len(parts) == 1 else jnp.concatenate(parts, axis=1)
        outs.append(_head_out(o, gnorm_ref[...], gate[:, h * dv:(h + 1) * dv]))
    return jnp.concatenate(outs, axis=1)


def _flush_sample_rows(y_ref, yacc_ref):
    @pl.when(pl.program_id(0) == pl.num_programs(0) - 1)
    def _():
        y_ref[...] = yacc_ref[...].astype(BF16)


def _gate_tile(h_ref, w_ref, g_ref):
    w = w_ref[...].astype(BF16)
    for r in range(0, MM_TM, MM_ROW_SUB):
        rows = slice(r, r + MM_ROW_SUB)
        g_ref[rows, :] = _sigmoid(_nt(h_ref[rows, :], w)).astype(BF16)


def _unit_index(step):
    return jnp.minimum(step, DEC_BATCH // SAMPLE_NB - 1)


def _hgrn_sample_kernel(h_ref, w_ref, zq_ref, zf_ref, zi_ref, zg_ref, lb_ref, gnorm_ref, s_ref, *rest, l):
    g_ref, y_ref, so_ref, yacc_ref = rest[-4:]
    _gate_tile(h_ref, w_ref, g_ref)
    base = _unit_index(pl.program_id(0)) * SAMPLE_NB
    q, k, log_f = _hgrn_prep(zq_ref[...], zf_ref[...], lb_ref, l)
    v, gate = zi_ref[...], zg_ref[...]
    for i in range(SAMPLE_NB):
        row = slice(i, i + 1)
        yacc_ref[pl.ds(base + i, 1), :] = _sample_step(
            q[row], k[row], v[row], log_f[row], gate[row], gnorm_ref, s_ref, so_ref, i,
            heads=HG_HEADS, dk=HG_DK, dv=HG_DV)
    _flush_sample_rows(y_ref, yacc_ref)


def _gla_sample_kernel(h_ref, w_ref, zq_ref, zk_ref, zv_ref, zr_ref, glr_ref, wup_ref, bup_ref,
                       gnorm_ref, s_ref, *rest):
    g_ref, y_ref, so_ref, yacc_ref = rest[-4:]
    _gate_tile(h_ref, w_ref, g_ref)
    base = _unit_index(pl.program_id(0)) * SAMPLE_NB
    glr = jnp.concatenate([glr_ref[...], jnp.zeros((16 - SAMPLE_NB, LANES), F32)], axis=0)
    q, k, log_a = _gla_prep(zq_ref[...], zk_ref[...], glr, wup_ref, bup_ref)
    v, gate = zv_ref[...], zr_ref[...]
    for i in range(SAMPLE_NB):
        row = slice(i, i + 1)
        yacc_ref[pl.ds(base + i, 1), :] = _sample_step(
            q[row], k[row], v[row], log_a[row], gate[row], gnorm_ref, s_ref, so_ref, i,
            heads=GLA_HEADS, dk=GLA_DK, dv=GLA_DV)
    _flush_sample_rows(y_ref, yacc_ref)


def _zrow(width, block):
    return pl.BlockSpec((SAMPLE_NB, width), lambda j: (N_PROMPT // SAMPLE_NB + _unit_index(j), block))


def _gate_sample_call(kernel, in_specs, args, h, w_in_t, state, y_all, new_state, gates, row_tile, l,
                      heads, dk, dv, name):
    tn = GATE_TN
    n_steps = N_BRANCH * D_MODEL // tn
    assert n_steps >= DEC_BATCH // SAMPLE_NB
    h_spec = pl.BlockSpec((MM_TM, D_MODEL), lambda j: (row_tile, 0), pipeline_mode=pl.Buffered(1))
    w_spec = pl.BlockSpec((None, pl.Element(tn), pl.Element(D_MODEL)),
                          lambda j: (l, pl.multiple_of(OFF_GATE + j * tn, 8), 0))
    st = pl.BlockSpec((None, SAMPLE_NB, heads, dk, dv), lambda j: (l, _unit_index(j), 0, 0, 0))
    any_spec = pl.BlockSpec(memory_space=pl.ANY)
    in_specs = [h_spec, w_spec] + in_specs + [st, any_spec]
    args = [h, w_in_t] + list(args) + [state, y_all]
    aliases = {len(args) - 1: 1}
    for buf, out_idx in ((new_state, 2), (gates, 0)):
        if buf is not None:
            aliases[len(args)] = out_idx
            in_specs.append(any_spec)
            args.append(buf)
    return pl.pallas_call(
        kernel,
        grid=(n_steps,),
        in_specs=in_specs,
        out_specs=[
            pl.BlockSpec((MM_TM, tn), lambda j: (row_tile, j)),
            pl.BlockSpec((DEC_BATCH, BRANCH_WIDTH), lambda j: (N_PROMPT // DEC_BATCH, 0)),
            st,
        ],
        out_shape=[
            jax.ShapeDtypeStruct((N_ALL, N_BRANCH * D_MODEL), BF16),
            jax.ShapeDtypeStruct((N_ALL, BRANCH_WIDTH), BF16),
            jax.ShapeDtypeStruct((DEPTH, DEC_BATCH, heads, dk, dv), F32),
        ],
        scratch_shapes=[pltpu.VMEM((DEC_BATCH, BRANCH_WIDTH), F32)],
        input_output_aliases=aliases,
        compiler_params=_cparams(("arbitrary",)),
        name=name,
    )(*args)


def _hgrn_sample(h, w_in_t, z, hg_lb, g_norm, state, y_all, new_state, gates, row_tile, l):
    w = BRANCH_WIDTH
    in_specs = [_zrow(w, 2), _zrow(w, 3), _zrow(w, 4), _zrow(w, 5),
                pl.BlockSpec((DEPTH, w), lambda j: (0, 0)),
                pl.BlockSpec((1, HG_DV), lambda j: (0, 0))]
    args = [z, z, z, z, hg_lb, g_norm[l].reshape(1, HG_DV)]
    return _gate_sample_call(functools.partial(_hgrn_sample_kernel, l=l), in_specs, args, h, w_in_t,
                             state, y_all, new_state, gates, row_tile, l,
                             HG_HEADS, HG_DK, HG_DV, "gate_hgrn_sample")


def _gla_sample(h, w_in_t, z, z_glr, w_up, b_up, g_norm, state, y_all, new_state, gates, row_tile, l):
    w = BRANCH_WIDTH
    kw = GLA_DK_TOTAL
    in_specs = [_zrow(kw, 6 * w // kw), _zrow(kw, 6 * w // kw + 1), _zrow(w, 7), _zrow(w, 8),
                _zrow(LANES, 0),
                pl.BlockSpec((None, GLA_RANK, kw), lambda j: (l, 0, 0)),
                pl.BlockSpec((1, kw), lambda j: (0, 0)),
                pl.BlockSpec((1, GLA_DV), lambda j: (0, 0))]
    args = [z, z, z, z, z_glr, w_up, b_up[l].reshape(1, kw), g_norm[l].reshape(1, GLA_DV)]
    return _gate_sample_call(_gla_sample_kernel, in_specs, args, h, w_in_t, state, y_all, new_state,
                             gates, row_tile, l, GLA_HEADS, GLA_DK, GLA_DV, "gate_gla_sample")


def kernel(x_prompt, x_sample, state_hgrn, state_gla, c_prompt, c_sample, w_ada, b_ada, g_pre_mix, g_post_mix, g_pre_mlp, g_post_mlp, w_in, sgu_ln_g, sgu_ln_b, sgu_w_s, sgu_b_s, hg_lb, hg_norm_g, gla_w_up, gla_b_up, gla_norm_g, w_branch, w_out, w_mlp_up, w_mlp_down):
    x = jnp.concatenate([x_prompt.reshape(N_PROMPT, D_MODEL), x_sample.reshape(DEC_BATCH, D_MODEL)], axis=0)
    c_all = jnp.concatenate([c_sample, c_prompt, jnp.zeros((MOD_ROWS - DEC_BATCH - BATCH, D_MODEL), F32)], axis=0)
    mod = _modulation(c_all, w_ada, b_ada)
    w_in_t = jnp.swapaxes(w_in, 1, 2)
    b_s_t = jnp.swapaxes(sgu_b_s, 1, 2)

    hg_p, gla_p, v_rows = [], [], []
    hg_s = gla_s = None
    h = _prenorm(x, g_pre_mix[0], mod, 0, 0, 1)
    for l in range(DEPTH):
        z = _matmul(h, w_in_t, l, n_out=Z_MAIN, w_transposed=True, name="in_proj")
        z_glr = _matmul(h, w_in_t, l, n_out=LANES, col_off=OFF_GLR, tn=LANES, w_transposed=True,
                        name="in_proj_lowrank")

        y_a, v_s = _sgu(z, sgu_ln_g, sgu_ln_b, sgu_w_s, b_s_t, l)
        y_b, s_hg_p = _hgrn_prompt(z, hg_lb, hg_norm_g, l)
        y_c, s_gla_p = _gla_prompt(z, z_glr, gla_w_up, gla_b_up, gla_norm_g, l)
        gates, y_b, hg_s = _hgrn_sample(h, w_in_t, z, hg_lb, hg_norm_g, state_hgrn, y_b, hg_s,
                                        None, 0, l)
        gates, y_c, gla_s = _gla_sample(h, w_in_t, z, z_glr, gla_w_up, gla_b_up, gla_norm_g,
                                        state_gla, y_c, gla_s, gates, 1, l)

        merged = _merge(y_a, y_b, y_c, w_branch, gates, l)
        out = _matmul(merged, w_out, l, n_out=D_MODEL, name="out_proj")
        x, h2 = _resid(x, out, g_post_mix[l], mod, l, 2, nxt=(g_pre_mlp[l], l, 3, 4))

        up = _matmul(h2, w_mlp_up, l, n_out=D_FF, act=_act_relu2, out_dtype=BF16, name="mlp_up")
        y2 = _matmul(up, w_mlp_down, l, n_out=D_MODEL, tm=MM_TM_HALF, tn=1024, tk=1024,
                     row_sub=MM_TM_HALF // 2, name="mlp_down")
        if l + 1 < DEPTH:
            x, h = _resid(x, y2, g_post_mlp[l], mod, l, 5, nxt=(g_pre_mix[l + 1], l + 1, 0, 1))
        else:
            (x,) = _resid(x, y2, g_post_mlp[l], mod, l, 5)

        hg_p.append(s_hg_p)
        gla_p.append(s_gla_p)
        v_rows.append(v_s.reshape(DEC_BATCH, 1, BRANCH_WIDTH))

    y_prompt = x[:N_PROMPT].reshape(BATCH, SEQ, D_MODEL)
    y_sample = x[N_PROMPT:].reshape(DEC_BATCH, 1, D_MODEL)
    return (y_prompt, y_sample, jnp.stack(hg_p), jnp.stack(gla_p), hg_s, gla_s, jnp.stack(v_rows))
```

```python
import functools
from typing import NamedTuple

import jax
import jax.numpy as jnp
from jax import lax
from jax.experimental import pallas as pl
from jax.experimental.pallas import tpu as pltpu

F32 = jnp.float32
BF16 = jnp.bfloat16

D_MODEL = 2048
BATCH = 4
SEQ = 2048
DEPTH = 4
DEC_BATCH = 128
BRANCH_WIDTH = D_MODEL // 2
N_BRANCH = 3
SGU_GROUPS = 8
SGU_CHUNK = 128
HG_HEADS = 8
HG_DK = BRANCH_WIDTH // HG_HEADS
HG_DV = BRANCH_WIDTH // HG_HEADS
GLA_HEADS = 4
GLA_DK_TOTAL = BRANCH_WIDTH // 2
GLA_DK = GLA_DK_TOTAL // GLA_HEADS
GLA_DV = BRANCH_WIDTH // GLA_HEADS
GLA_RANK = 16
GLA_TAU = 16.0
D_FF = 4 * D_MODEL
EPS = 1e-6

MOD_ROWS = DEC_BATCH + 8

OFF_GLR = 2 * BRANCH_WIDTH + 4 * BRANCH_WIDTH + 2 * GLA_DK_TOTAL + 2 * BRANCH_WIDTH
OFF_GATE = OFF_GLR + GLA_RANK
Z_MAIN = OFF_GLR

LANES = 128
ROW_TILE = 128
REC_CHUNK = 128
HALF = REC_CHUNK // 2
CHUNKS_PER_SEQ = SEQ // REC_CHUNK
MAX_SPLIT_EXPONENT = 60.0
SAMPLE_NB = 8
MM_TN = 256
VMEM_LIMIT = 56 * 1024 * 1024


class _Stream(NamedTuple):
    rows: int
    batch0: int
    n_batch: int
    sample: bool
    norm_tile: int
    row_sub: int

    @property
    def prompt_rows(self):
        return self.n_batch * SEQ


STREAM_A = _Stream(rows=2 * SEQ, batch0=0, n_batch=2, sample=False, norm_tile=512, row_sub=1024)
STREAM_B = _Stream(rows=2 * SEQ + DEC_BATCH, batch0=2, n_batch=2, sample=True, norm_tile=384,
                   row_sub=1056)


def _cparams(sem):
    return pltpu.CompilerParams(dimension_semantics=sem, vmem_limit_bytes=VMEM_LIMIT)


def _sigmoid(x):
    return 1.0 / (1.0 + jnp.exp(-x))


def _silu(x):
    return x * _sigmoid(x)


def _log_sigmoid(x):
    return jnp.minimum(x, 0.0) - jnp.log(1.0 + jnp.exp(-jnp.abs(x)))


def _rms(x, g):
    return x * lax.rsqrt(jnp.mean(x * x, axis=-1, keepdims=True) + EPS) * g


def _nt(a, b):
    return lax.dot_general(a, b, (((1,), (1,)), ((), ())), preferred_element_type=F32)


def _mod_kernel(c_ref, w_ref, b_ref, o_ref):
    a = _silu(c_ref[...]).astype(BF16)
    o_ref[...] = jnp.dot(a, w_ref[...].astype(BF16), preferred_element_type=F32) + b_ref[...]


def _modulation(c_all, w_ada, b_ada):
    tn = 1024
    n = 6 * D_MODEL
    return pl.pallas_call(
        _mod_kernel,
        grid=(DEPTH, n // tn),
        in_specs=[
            pl.BlockSpec((MOD_ROWS, D_MODEL), lambda l, j: (0, 0)),
            pl.BlockSpec((None, D_MODEL, tn), lambda l, j: (l, 0, j)),
            pl.BlockSpec((None, 1, tn), lambda l, j: (l, 0, j)),
        ],
        out_specs=pl.BlockSpec((None, MOD_ROWS, tn), lambda l, j: (l, 0, j)),
        out_shape=jax.ShapeDtypeStruct((DEPTH, MOD_ROWS, n), F32),
        compiler_params=_cparams(("parallel", "parallel")),
        name="adaln_mod",
    )(c_all, w_ada, b_ada.reshape(DEPTH, 1, n))


def _mod_specs(l, j):
    return [
        pl.BlockSpec((None, DEC_BATCH, D_MODEL), lambda i: (l, 0, j)),
        pl.BlockSpec((None, 8, D_MODEL), lambda i: (l, DEC_BATCH // 8, j)),
    ]


def _pick_mod(st, s_ref, p_ref, sub):
    g = pl.program_id(0) * (st.norm_tile // ROW_TILE) + sub
    b = st.batch0 + jnp.minimum(g // (SEQ // ROW_TILE), st.n_batch - 1)
    prompt = p_ref[pl.ds(b, 1), :]
    if not st.sample:
        return prompt
    return jnp.where(g >= st.prompt_rows // ROW_TILE, s_ref[...], prompt)


def _prenorm_kernel(x_ref, g_ref, sc_s, sc_p, sh_s, sh_p, h_ref, *, st):
    for sub in range(st.norm_tile // ROW_TILE):
        rows = slice(sub * ROW_TILE, (sub + 1) * ROW_TILE)
        h = (_rms(x_ref[rows, :], g_ref[...]) * (1.0 + _pick_mod(st, sc_s, sc_p, sub))
             + _pick_mod(st, sh_s, sh_p, sub))
        h_ref[rows, :] = h.astype(BF16)


def _prenorm(st, x, g, mod, l, j_shift, j_scale):
    row = pl.BlockSpec((st.norm_tile, D_MODEL), lambda i: (i, 0))
    vec = pl.BlockSpec((1, D_MODEL), lambda i: (0, 0))
    return pl.pallas_call(
        functools.partial(_prenorm_kernel, st=st),
        grid=(st.rows // st.norm_tile,),
        in_specs=[row, vec] + _mod_specs(l, j_scale) + _mod_specs(l, j_shift),
        out_specs=row,
        out_shape=jax.ShapeDtypeStruct((st.rows, D_MODEL), BF16),
        compiler_params=_cparams(("parallel",)),
        name="prenorm",
    )(x, g.reshape(1, D_MODEL), mod, mod, mod, mod)


def _resid_kernel(x_ref, y_ref, gpost_ref, gt_s, gt_p, *rest, st, with_next):
    for sub in range(st.norm_tile // ROW_TILE):
        rows = slice(sub * ROW_TILE, (sub + 1) * ROW_TILE)
        x_new = x_ref[rows, :] + _pick_mod(st, gt_s, gt_p, sub) * _rms(y_ref[rows, :], gpost_ref[...])
        if with_next:
            gpre_ref, sc_s, sc_p, sh_s, sh_p, xo_ref, h_ref = rest
            xo_ref[rows, :] = x_new
            h = (_rms(x_new, gpre_ref[...]) * (1.0 + _pick_mod(st, sc_s, sc_p, sub))
                 + _pick_mod(st, sh_s, sh_p, sub))
            h_ref[rows, :] = h.astype(BF16)
        else:
            (xo_ref,) = rest
            xo_ref[rows, :] = x_new


def _resid(st, x, y, g_post, mod, l_gate, j_gate, nxt=None):
    row = pl.BlockSpec((st.norm_tile, D_MODEL), lambda i: (i, 0))
    vec = pl.BlockSpec((1, D_MODEL), lambda i: (0, 0))
    in_specs = [row, row, vec] + _mod_specs(l_gate, j_gate)
    args = [x, y, g_post.reshape(1, D_MODEL), mod, mod]
    out_specs = [row]
    out_shape = [jax.ShapeDtypeStruct((st.rows, D_MODEL), F32)]
    if nxt is not None:
        g_pre, l_n, j_shift, j_scale = nxt
        in_specs += [vec] + _mod_specs(l_n, j_scale) + _mod_specs(l_n, j_shift)
        args += [g_pre.reshape(1, D_MODEL), mod, mod, mod, mod]
        out_specs.append(row)
        out_shape.append(jax.ShapeDtypeStruct((st.rows, D_MODEL), BF16))
    return pl.pallas_call(
        functools.partial(_resid_kernel, st=st, with_next=nxt is not None),
        grid=(st.rows // st.norm_tile,),
        in_specs=in_specs,
        out_specs=out_specs,
        out_shape=out_shape,
        compiler_params=_cparams(("parallel",)),
        name="resid_norm",
    )(*args)


def _hgrn_lower_bound(lb_ref, l):
    raw = lb_ref[...]
    e = jnp.exp(raw - jnp.max(raw, axis=0, keepdims=True))
    p = e / jnp.sum(e, axis=0, keepdims=True)
    acc = p[0:1, :]
    for j in range(1, l + 1):
        acc = acc + p[j:j + 1, :]
    return jnp.maximum(acc - p[0:1, :], 0.0)


def _hgrn_prep(zq, zf, lb_ref, l):
    e = jnp.exp(-jnp.abs(zf))
    ope = 1.0 + e
    sig_neg = jnp.where(zf >= 0.0, e, 1.0) / ope
    log_sig = jnp.minimum(zf, 0.0) - jnp.log(ope)
    if l == 0:
        return _silu(zq), sig_neg, log_sig
    lb = _hgrn_lower_bound(lb_ref, l)
    a = jnp.log(lb)
    c = jnp.log1p(-lb) + log_sig
    log_f = jnp.maximum(a, c) + jnp.log(1.0 + jnp.exp(-jnp.abs(a - c)))
    return _silu(zq), (1.0 - lb) * sig_neg, log_f


def _gla_prep(zq, zk, glr, wup_ref, bup_ref):
    x = jnp.dot(glr[:, :GLA_RANK].astype(BF16), wup_ref[...].astype(BF16),
                preferred_element_type=F32) + bup_ref[...]
    log_a = _log_sigmoid(x) / GLA_TAU
    return zq, zk * (GLA_DK ** -0.5), log_a


def _lane_bcast_col(row):
    return jnp.broadcast_to(row, (LANES, LANES)).T


def _head_out(o, g_row, gate):
    o = o * lax.rsqrt(jnp.mean(o * o, axis=-1, keepdims=True) + EPS) * g_row
    return o * _silu(gate)


class _RecRefs(NamedTuple):
    gnorm: object
    y: object
    s_out: object
    s: object
    s_prev: object
    q: object
    b: object
    o_intra: object


def _rec_scratch(heads, dk, dv):
    return [
        pltpu.VMEM((heads, dk, dv), F32),
        pltpu.VMEM((heads, dk, dv), F32),
        pltpu.VMEM((REC_CHUNK, heads * dk), F32),
        pltpu.VMEM((REC_CHUNK, heads * dk), F32),
        pltpu.VMEM((heads, REC_CHUNK, dv), F32),
    ]


def _finish_head(h, o_intra, q_dec, k_dec, b_last, vh, gate, s_old, refs, dv, active):
    vs = slice(h * dv, (h + 1) * dv)
    o = o_intra + jnp.dot(q_dec.astype(BF16), s_old.astype(BF16), preferred_element_type=F32)
    y_new = _head_out(o, refs.gnorm[...], gate[:, vs]).astype(BF16)
    upd = jnp.dot(k_dec.T.astype(BF16), vh, preferred_element_type=F32)
    decay = _lane_bcast_col(jnp.exp(b_last))
    if active is not None:
        y_new = jnp.where(active, y_new, refs.y[:, vs])
    refs.y[:, vs] = y_new
    for j in range(dv // LANES):
        ls = slice(j * LANES, (j + 1) * LANES)
        s_new = decay * s_old[:, ls] + upd[:, ls]
        if active is not None:
            s_new = jnp.where(active, s_new, s_old[:, ls])
        refs.s[h, :, ls] = s_new


def _chunk_start(c, refs):
    @pl.when(c == 0)
    def _():
        refs.s[...] = jnp.zeros_like(refs.s)


def _chunk_main(q, k, v, log_f, gate, refs, active, *, heads, dk, dv):
    t_idx = lax.broadcasted_iota(jnp.int32, (REC_CHUNK, REC_CHUNK), 0)
    s_idx = lax.broadcasted_iota(jnp.int32, (REC_CHUNK, REC_CHUNK), 1)
    tril = jnp.where(s_idx <= t_idx, 1.0, 0.0).astype(F32)
    b_all = jnp.dot(tril, log_f, precision=lax.Precision.HIGHEST, preferred_element_type=F32)
    v16 = v.astype(BF16)

    mid = HALF // 2 - 1

    def drop(lo, hi):
        return b_all[lo:lo + 1] - b_all[hi:hi + 1]

    worst = jnp.maximum(jnp.maximum(drop(0, mid), drop(mid, HALF - 1)),
                        jnp.maximum(drop(HALF, HALF + mid), drop(HALF + mid, REC_CHUNK - 1)))
    unsafe = jnp.max(worst) > MAX_SPLIT_EXPONENT
    if active is not None:
        unsafe = jnp.logical_and(unsafe, active)

    causal_half = (lax.broadcasted_iota(jnp.int32, (HALF, HALF), 1)
                   <= lax.broadcasted_iota(jnp.int32, (HALF, HALF), 0))
    for h in range(heads):
        ks = slice(h * dk, (h + 1) * dk)
        qh, kh, bh = q[:, ks], k[:, ks], b_all[:, ks]
        vh = v16[:, h * dv:(h + 1) * dv]
        b_a, b_b = bh[:HALF], bh[HALF:]
        v_a, v_b = vh[:HALF], vh[HALF:]
        r_a = b_a[mid:mid + 1]
        r_b = b_b[mid:mid + 1]
        r_m = b_a[HALF - 1:HALF]
        b_last = b_b[HALF - 1:HALF]
        q_a = qh[:HALF] * jnp.exp(b_a - r_a)
        q_b = qh[HALF:] * jnp.exp(b_b - r_b)
        k_a = kh[:HALF] * jnp.exp(r_a - b_a)
        k_b = kh[HALF:] * jnp.exp(r_b - b_b)
        att_aa = jnp.where(causal_half, _nt(q_a.astype(BF16), k_a.astype(BF16)), 0.0).astype(BF16)
        att_bb = jnp.where(causal_half, _nt(q_b.astype(BF16), k_b.astype(BF16)), 0.0).astype(BF16)
        att_ba = _nt((q_b * jnp.exp(r_b - r_m)).astype(BF16),
                     (k_a * jnp.exp(r_m - r_a)).astype(BF16)).astype(BF16)
        o_a = jnp.dot(att_aa, v_a, preferred_element_type=F32)
        o_b = (jnp.dot(att_ba, v_a, preferred_element_type=F32)
               + jnp.dot(att_bb, v_b, preferred_element_type=F32))
        q_dec = jnp.concatenate([q_a * jnp.exp(r_a), q_b * jnp.exp(r_b)], axis=0)
        k_dec = jnp.concatenate([k_a * jnp.exp(b_last - r_a), k_b * jnp.exp(b_last - r_b)], axis=0)
        s_old = refs.s[h]
        refs.s_prev[h] = s_old
        _finish_head(h, jnp.concatenate([o_a, o_b], axis=0), q_dec, k_dec, b_last, vh, gate, s_old,
                     refs, dv, active)
    return unsafe, b_all, v16


def _chunk_redo(unsafe, q, k, v, b_all, v16, gate, refs, *, heads, dk, dv):
    @pl.when(unsafe)
    def _():
        refs.q[...] = q
        refs.b[...] = b_all
        s_pos = lax.broadcasted_iota(jnp.int32, (REC_CHUNK, 1), 0)
        for h in range(heads):
            ks = slice(h * dk, (h + 1) * dk)
            vs = slice(h * dv, (h + 1) * dv)
            qh, kh, bh = q[:, ks], k[:, ks], b_all[:, ks]
            vh = v[:, vs]

            def row(t, carry):
                b_t = refs.b[pl.ds(t, 1), :][:, ks]
                q_t = refs.q[pl.ds(t, 1), :][:, ks]
                p = jnp.exp(jnp.minimum(b_t - bh, 0.0)) * kh * q_t
                w = jnp.where(s_pos <= t, jnp.sum(p, axis=1, keepdims=True), 0.0)
                refs.o_intra[h, pl.ds(t, 1), :] = jnp.sum(w * vh, axis=0, keepdims=True)
                return carry

            lax.fori_loop(0, REC_CHUNK, row, 0)
            b_last = bh[REC_CHUNK - 1:REC_CHUNK]
            _finish_head(h, refs.o_intra[h], qh * jnp.exp(bh), kh * jnp.exp(b_last - bh), b_last,
                         v16[:, vs], gate, refs.s_prev[h], refs, dv, None)


def _chunk_end(last, refs):
    @pl.when(last)
    def _():
        refs.s_out[...] = refs.s[...]


def _act_none(x):
    return x


def _act_relu2(x):
    return jnp.square(jnp.maximum(x, 0.0))


def _project_rows(a_ref, w, o_ref, row_sub, act, w_transposed, accumulate):
    for r in range(0, a_ref.shape[0], row_sub):
        rows = slice(r, r + row_sub)
        a = a_ref[rows, :]
        part = _nt(a, w) if w_transposed else jnp.dot(a, w, preferred_element_type=F32)
        if accumulate:
            o_ref[rows, :] += part
        else:
            o_ref[rows, :] = act(part).astype(o_ref.dtype)


def _mm_kernel(a_ref, w_ref, o_ref, *, nk, act, w_transposed, row_sub):
    w = w_ref[...].astype(BF16)
    if nk > 1:
        @pl.when(pl.program_id(2) == 0)
        def _():
            o_ref[...] = jnp.zeros_like(o_ref)

    _project_rows(a_ref, w, o_ref, row_sub, act, w_transposed, nk > 1)


def _w_spec(l, col_off, tn, tk, w_transposed, grid_rank):
    def jk(idx):
        return (idx[0], 0) if grid_rank == 1 else (idx[1], idx[2])

    if w_transposed:
        assert col_off % 8 == 0 and tn % 8 == 0 and tk % LANES == 0
        return pl.BlockSpec(
            (None, pl.Element(tn), pl.Element(tk)),
            lambda *idx: (l, pl.multiple_of(col_off + jk(idx)[0] * tn, 8),
                          pl.multiple_of(jk(idx)[1] * tk, LANES)))
    assert col_off % tn == 0
    return pl.BlockSpec((None, tk, tn), lambda *idx: (l, jk(idx)[1], col_off // tn + jk(idx)[0]))


def _matmul(st, a, w, l, *, n_out, col_off=0, tm=None, tn=MM_TN, tk=None, row_sub=None,
            act=_act_none, out_dtype=F32, w_transposed=False, name):
    k_dim = a.shape[1]
    tm = st.rows if tm is None else tm
    tk = k_dim if tk is None else tk
    row_sub = st.row_sub if row_sub is None else row_sub
    nk = k_dim // tk
    assert nk == 1 or (act is _act_none and out_dtype == F32)
    return pl.pallas_call(
        functools.partial(_mm_kernel, nk=nk, act=act, w_transposed=w_transposed, row_sub=row_sub),
        grid=(st.rows // tm, n_out // tn, nk),
        in_specs=[pl.BlockSpec((tm, tk), lambda i, j, k: (i, k)),
                  _w_spec(l, col_off, tn, tk, w_transposed, 3)],
        out_specs=pl.BlockSpec((tm, tn), lambda i, j, k: (i, j)),
        out_shape=jax.ShapeDtypeStruct((st.rows, n_out), out_dtype),
        compiler_params=_cparams(("parallel", "parallel", "arbitrary")),
        name=name,
    )(a, w)


def _mm_rec_kernel(a_ref, w_ref, hq_ref, hf_ref, hi_ref, hg_ref, lb_ref, hgn_ref,
                   gq_ref, gk_ref, gv_ref, gr_ref, glr_ref, wup_ref, bup_ref, glan_ref, *rest,
                   l, act, w_transposed, row_sub, n_steps, n_rec_steps, n_aliased):
    o_ref, yb_ref, shg_ref, yc_ref, sgla_ref = rest[n_aliased:n_aliased + 5]
    scratch = rest[n_aliased + 5:]
    hg = _RecRefs(hgn_ref, yb_ref, shg_ref, *scratch[:5])
    gla = _RecRefs(glan_ref, yc_ref, sgla_ref, *scratch[5:])
    hg_dims = dict(heads=HG_HEADS, dk=HG_DK, dv=HG_DV)
    gla_dims = dict(heads=GLA_HEADS, dk=GLA_DK, dv=GLA_DV)

    j = pl.program_id(0)
    active = None if n_steps == n_rec_steps else j < n_rec_steps
    c = jnp.minimum(j, n_rec_steps - 1) % CHUNKS_PER_SEQ
    last = c == CHUNKS_PER_SEQ - 1
    if active is not None:
        last = jnp.logical_and(last, active)
    _chunk_start(c, hg)
    _chunk_start(c, gla)

    _project_rows(a_ref, w_ref[...].astype(BF16), o_ref, row_sub, act, w_transposed, False)

    hq, hk, hlf = _hgrn_prep(hq_ref[...], hf_ref[...], lb_ref, l)
    hv, hgate = hi_ref[...], hg_ref[...]
    h_unsafe, h_b, h_v16 = _chunk_main(hq, hk, hv, hlf, hgate, hg, active, **hg_dims)
    gq, gk, gla_la = _gla_prep(gq_ref[...], gk_ref[...], glr_ref[...], wup_ref, bup_ref)
    gv, ggate = gv_ref[...], gr_ref[...]
    g_unsafe, g_b, g_v16 = _chunk_main(gq, gk, gv, gla_la, ggate, gla, active, **gla_dims)

    _chunk_redo(h_unsafe, hq, hk, hv, h_b, h_v16, hgate, hg, **hg_dims)
    _chunk_redo(g_unsafe, gq, gk, gv, g_b, g_v16, ggate, gla, **gla_dims)
    _chunk_end(last, hg)
    _chunk_end(last, gla)


def _matmul_with_recurrence(st, a, w, l, *, n_out, act=_act_none, out_dtype=F32, w_transposed=False,
                            rst, z, z_glr, hg_lb, hg_norm_g, w_up, b_up, gla_norm_g, y_b=None, y_c=None,
                            name):
    tn = MM_TN
    n_steps = n_out // tn
    n_rec = rst.prompt_rows // REC_CHUNK
    assert n_steps >= n_rec

    def chunk(j):
        return jnp.minimum(j, n_rec - 1)

    def zcol(width, block):
        return pl.BlockSpec((REC_CHUNK, width), lambda j: (chunk(j), block))

    def const(shape):
        return pl.BlockSpec(shape, lambda j: (0,) * len(shape))

    def state_spec(heads, dk, dv):
        return pl.BlockSpec((None, heads, dk, dv), lambda j: (chunk(j) // CHUNKS_PER_SEQ, 0, 0, 0))

    w_ = BRANCH_WIDTH
    kw = GLA_DK_TOTAL
    y_spec = pl.BlockSpec((REC_CHUNK, w_), lambda j: (chunk(j), 0))
    in_specs = [
        pl.BlockSpec((st.rows, a.shape[1]), lambda j: (0, 0), pipeline_mode=pl.Buffered(1)),
        _w_spec(l, 0, tn, a.shape[1], w_transposed, 1),
        zcol(w_, 2), zcol(w_, 3), zcol(w_, 4), zcol(w_, 5), const((DEPTH, w_)), const((1, HG_DV)),
        zcol(kw, 6 * w_ // kw), zcol(kw, 6 * w_ // kw + 1), zcol(w_, 7), zcol(w_, 8), zcol(LANES, 0),
        pl.BlockSpec((None, GLA_RANK, kw), lambda j: (l, 0, 0)), const((1, kw)), const((1, GLA_DV)),
    ]
    args = [a, w, z, z, z, z, hg_lb, hg_norm_g[l].reshape(1, HG_DV),
            z, z, z, z, z_glr, w_up, b_up[l].reshape(1, kw), gla_norm_g[l].reshape(1, GLA_DV)]
    aliases = {}
    for buf, out_idx in ((y_b, 1), (y_c, 3)):
        if buf is not None:
            aliases[len(args)] = out_idx
            in_specs.append(pl.BlockSpec(memory_space=pl.ANY))
            args.append(buf)
    return pl.pallas_call(
        functools.partial(_mm_rec_kernel, l=l, act=act, w_transposed=w_transposed,
                          row_sub=st.row_sub, n_steps=n_steps, n_rec_steps=n_rec,
                          n_aliased=len(aliases)),
        grid=(n_steps,),
        in_specs=in_specs,
        out_specs=[
            pl.BlockSpec((st.rows, tn), lambda j: (0, j)),
            y_spec, state_spec(HG_HEADS, HG_DK, HG_DV),
            y_spec, state_spec(GLA_HEADS, GLA_DK, GLA_DV),
        ],
        out_shape=[
            jax.ShapeDtypeStruct((st.rows, n_out), out_dtype),
            jax.ShapeDtypeStruct((rst.rows, w_), BF16),
            jax.ShapeDtypeStruct((rst.n_batch, HG_HEADS, HG_DK, HG_DV), F32),
            jax.ShapeDtypeStruct((rst.rows, w_), BF16),
            jax.ShapeDtypeStruct((rst.n_batch, GLA_HEADS, GLA_DK, GLA_DV), F32),
        ],
        scratch_shapes=_rec_scratch(HG_HEADS, HG_DK, HG_DV) + _rec_scratch(GLA_HEADS, GLA_DK, GLA_DV),
        input_output_aliases=aliases,
        compiler_params=_cparams(("arbitrary",)),
        name=name,
    )(*args)


def _merge_kernel(ya_ref, yb_ref, yc_ref, wa_ref, wb_ref, wc_ref, ga_ref, gb_ref, gc_ref, o_ref):
    wa = wa_ref[...].astype(BF16)
    wb = wb_ref[...].astype(BF16)
    wc = wc_ref[...].astype(BF16)
    tm = o_ref.shape[0]
    for r in range(0, tm, tm // 2):
        rows = slice(r, r + tm // 2)
        merged = ga_ref[rows, :] * jnp.dot(ya_ref[rows, :], wa, preferred_element_type=F32)
        merged = merged + gb_ref[rows, :] * jnp.dot(yb_ref[rows, :], wb, preferred_element_type=F32)
        merged = merged + gc_ref[rows, :] * jnp.dot(yc_ref[rows, :], wc, preferred_element_type=F32)
        o_ref[rows, :] = merged.astype(BF16)


def _merge(st, ya, yb, yc, w_branch, gates, l):
    tn = MM_TN
    tm = st.rows // 2
    nb = D_MODEL // tn
    y_spec = pl.BlockSpec((tm, BRANCH_WIDTH), lambda i, j: (i, 0), pipeline_mode=pl.Buffered(1))

    def w_spec(b):
        return pl.BlockSpec((None, None, BRANCH_WIDTH, tn), lambda i, j: (l, b, 0, j))

    def g_spec(b):
        return pl.BlockSpec((tm, tn), lambda i, j: (i, b * nb + j))

    return pl.pallas_call(
        _merge_kernel,
        grid=(st.rows // tm, nb),
        in_specs=[y_spec, y_spec, y_spec, w_spec(0), w_spec(1), w_spec(2),
                  g_spec(0), g_spec(1), g_spec(2)],
        out_specs=pl.BlockSpec((tm, tn), lambda i, j: (i, j)),
        out_shape=jax.ShapeDtypeStruct((st.rows, D_MODEL), BF16),
        compiler_params=_cparams(("parallel", "parallel")),
        name="branch_merge",
    )(ya, yb, yc, w_branch, w_branch, w_branch, gates, gates, gates)


def _sgu_kernel(zu_ref, zv_ref, lg_ref, lb_ref, ws_ref, bst_ref, y_ref, *maybe_vs_ref, st):
    u = jax.nn.gelu(zu_ref[...], approximate=True)
    v = jax.nn.gelu(zv_ref[...], approximate=True)
    mu = jnp.mean(v, axis=-1, keepdims=True)
    vc = v - mu
    v = vc * lax.rsqrt(jnp.mean(vc * vc, axis=-1, keepdims=True) + EPS) * lg_ref[...] + lb_ref[...]

    def prompt_chunk():
        n_idx = lax.broadcasted_iota(jnp.int32, (SGU_CHUNK, SGU_CHUNK), 0)
        m_idx = lax.broadcasted_iota(jnp.int32, (SGU_CHUNK, SGU_CHUNK), 1)
        causal = m_idx <= n_idx
        for g in range(SGU_GROUPS):
            sl = slice(g * LANES, (g + 1) * LANES)
            wm = jnp.where(causal, ws_ref[g], 0.0).astype(BF16)
            s = jnp.dot(wm, v[:, sl].astype(BF16), preferred_element_type=F32) + bst_ref[:, g:g + 1]
            y_ref[:, sl] = (u[:, sl] * s).astype(BF16)

    if not st.sample:
        prompt_chunk()
        return
    (vs_ref,) = maybe_vs_ref
    is_sample = pl.program_id(0) == st.prompt_rows // ROW_TILE
    pl.when(jnp.logical_not(is_sample))(prompt_chunk)

    @pl.when(is_sample)
    def _():
        vs_ref[...] = v
        for g in range(SGU_GROUPS):
            sl = slice(g * LANES, (g + 1) * LANES)
            s = v[:, sl] * ws_ref[g][0:1, 0:1] + bst_ref[0:1, g:g + 1]
            y_ref[:, sl] = (u[:, sl] * s).astype(BF16)


def _sgu(st, z, ln_g, ln_b, w_s, b_s_t, l):
    vec = pl.BlockSpec((1, BRANCH_WIDTH), lambda i: (0, 0))
    tile = pl.BlockSpec((ROW_TILE, BRANCH_WIDTH), lambda i: (i, 0))
    out_specs = [tile]
    out_shape = [jax.ShapeDtypeStruct((st.rows, BRANCH_WIDTH), BF16)]
    if st.sample:
        out_specs.append(pl.BlockSpec((DEC_BATCH, BRANCH_WIDTH), lambda i: (0, 0)))
        out_shape.append(jax.ShapeDtypeStruct((DEC_BATCH, BRANCH_WIDTH), F32))
    return pl.pallas_call(
        functools.partial(_sgu_kernel, st=st),
        grid=(st.rows // ROW_TILE,),
        in_specs=[
            tile,
            pl.BlockSpec((ROW_TILE, BRANCH_WIDTH), lambda i: (i, 1)),
            vec, vec,
            pl.BlockSpec((None, SGU_GROUPS, SGU_CHUNK, SGU_CHUNK), lambda i: (l, 0, 0, 0)),
            pl.BlockSpec((None, SGU_CHUNK, SGU_GROUPS), lambda i: (l, 0, 0)),
        ],
        out_specs=out_specs,
        out_shape=out_shape,
        compiler_params=_cparams(("arbitrary",)),
        name="sgu",
    )(z, z, ln_g[l].reshape(1, BRANCH_WIDTH), ln_b[l].reshape(1, BRANCH_WIDTH), w_s, b_s_t)


def _sample_step(q, k, v, log_f, gate, gnorm_ref, s_ref, so_ref, i, *, heads, dk, dv):
    outs = []
    for h in range(heads):
        ks = slice(h * dk, (h + 1) * dk)
        f_col = _lane_bcast_col(jnp.exp(log_f[:, ks]))
        k_col = _lane_bcast_col(k[:, ks])
        q16 = jnp.broadcast_to(q[:, ks], (16, dk)).astype(BF16)
        parts = []
        for j in range(dv // LANES):
            ls = slice(j * LANES, (j + 1) * LANES)
            v_row = v[:, h * dv + j * LANES:h * dv + (j + 1) * LANES]
            s_new = f_col * s_ref[i, h, :, ls] + k_col * v_row
            so_ref[i, h, :, ls] = s_new
            parts.append(jnp.dot(q16, s_new.astype(BF16), preferred_element_type=F32)[0:1])
        o = parts[0] if len(parts) == 1 else jnp.concatenate(parts, axis=1)
        outs.append(_head_out(o, gnorm_ref[...], gate[:, h * dv:(h + 1) * dv]))
    return jnp.concatenate(outs, axis=1)


def _flush_sample_rows(y_ref, yacc_ref):
    @pl.when(pl.program_id(0) == pl.num_programs(0) - 1)
    def _():
        y_ref[...] = yacc_ref[...].astype(BF16)


def _gate_tile(h_ref, w_ref, g_ref, row_sub):
    _project_rows(h_ref, w_ref[...].astype(BF16), g_ref, row_sub, _sigmoid, True, False)


def _unit_index(step):
    return jnp.minimum(step, DEC_BATCH // SAMPLE_NB - 1)


def _hgrn_sample_kernel(h_ref, w_ref, zq_ref, zf_ref, zi_ref, zg_ref, lb_ref, gnorm_ref, s_ref, *rest,
                        l, row_sub):
    g_ref, y_ref, so_ref, yacc_ref = rest[-4:]
    _gate_tile(h_ref, w_ref, g_ref, row_sub)
    base = _unit_index(pl.program_id(0)) * SAMPLE_NB
    q, k, log_f = _hgrn_prep(zq_ref[...], zf_ref[...], lb_ref, l)
    v, gate = zi_ref[...], zg_ref[...]
    for i in range(SAMPLE_NB):
        row = slice(i, i + 1)
        yacc_ref[pl.ds(base + i, 1), :] = _sample_step(
            q[row], k[row], v[row], log_f[row], gate[row], gnorm_ref, s_ref, so_ref, i,
            heads=HG_HEADS, dk=HG_DK, dv=HG_DV)
    _flush_sample_rows(y_ref, yacc_ref)


def _gla_sample_kernel(h_ref, w_ref, zq_ref, zk_ref, zv_ref, zr_ref, glr_ref, wup_ref, bup_ref,
                       gnorm_ref, s_ref, *rest, row_sub):
    g_ref, y_ref, so_ref, yacc_ref = rest[-4:]
    _gate_tile(h_ref, w_ref, g_ref, row_sub)
    base = _unit_index(pl.program_id(0)) * SAMPLE_NB
    glr = jnp.concatenate([glr_ref[...], jnp.zeros((16 - SAMPLE_NB, LANES), F32)], axis=0)
    q, k, log_a = _gla_prep(zq_ref[...], zk_ref[...], glr, wup_ref, bup_ref)
    v, gate = zv_ref[...], zr_ref[...]
    for i in range(SAMPLE_NB):
        row = slice(i, i + 1)
        yacc_ref[pl.ds(base + i, 1), :] = _sample_step(
            q[row], k[row], v[row], log_a[row], gate[row], gnorm_ref, s_ref, so_ref, i,
            heads=GLA_HEADS, dk=GLA_DK, dv=GLA_DV)
    _flush_sample_rows(y_ref, yacc_ref)


def _zrow(width, block):
    first = STREAM_B.prompt_rows // SAMPLE_NB
    return pl.BlockSpec((SAMPLE_NB, width), lambda j: (first + _unit_index(j), block))


def _gate_sample_call(kernel, in_specs, args, st, h, w_in_t, state, new_state, l, heads, dk, dv, name):
    tn = MM_TN
    n_steps = N_BRANCH * D_MODEL // tn
    assert n_steps >= DEC_BATCH // SAMPLE_NB
    h_spec = pl.BlockSpec((st.rows, D_MODEL), lambda j: (0, 0), pipeline_mode=pl.Buffered(1))
    sb = pl.BlockSpec((None, SAMPLE_NB, heads, dk, dv), lambda j: (l, _unit_index(j), 0, 0, 0))
    in_specs = [h_spec, _w_spec(l, OFF_GATE, tn, D_MODEL, True, 1)] + in_specs + [sb]
    args = [h, w_in_t] + list(args) + [state]
    aliases = {}
    if new_state is not None:
        aliases[len(args)] = 2
        in_specs.append(pl.BlockSpec(memory_space=pl.ANY))
        args.append(new_state)
    return pl.pallas_call(
        kernel,
        grid=(n_steps,),
        in_specs=in_specs,
        out_specs=[
            pl.BlockSpec((st.rows, tn), lambda j: (0, j)),
            pl.BlockSpec((DEC_BATCH, BRANCH_WIDTH), lambda j: (STREAM_B.prompt_rows // DEC_BATCH, 0)),
            sb,
        ],
        out_shape=[
            jax.ShapeDtypeStruct((st.rows, N_BRANCH * D_MODEL), BF16),
            jax.ShapeDtypeStruct((STREAM_B.rows, BRANCH_WIDTH), BF16),
            jax.ShapeDtypeStruct((DEPTH, DEC_BATCH, heads, dk, dv), F32),
        ],
        scratch_shapes=[pltpu.VMEM((DEC_BATCH, BRANCH_WIDTH), F32)],
        input_output_aliases=aliases,
        compiler_params=_cparams(("arbitrary",)),
        name=name,
    )(*args)


def _gates_with_hgrn_sample(st, h, w_in_t, z_b, hg_lb, g_norm, state, new_state, l):
    w = BRANCH_WIDTH
    in_specs = [_zrow(w, 2), _zrow(w, 3), _zrow(w, 4), _zrow(w, 5),
                pl.BlockSpec((DEPTH, w), lambda j: (0, 0)),
                pl.BlockSpec((1, HG_DV), lambda j: (0, 0))]
    args = [z_b, z_b, z_b, z_b, hg_lb, g_norm[l].reshape(1, HG_DV)]
    return _gate_sample_call(functools.partial(_hgrn_sample_kernel, l=l, row_sub=st.row_sub),
                             in_specs, args, st, h, w_in_t, state, new_state, l,
                             HG_HEADS, HG_DK, HG_DV, "gate_hgrn_sample")


def _gates_with_gla_sample(st, h, w_in_t, z_b, z_glr_b, w_up, b_up, g_norm, state, new_state, l):
    w = BRANCH_WIDTH
    kw = GLA_DK_TOTAL
    in_specs = [_zrow(kw, 6 * w // kw), _zrow(kw, 6 * w // kw + 1), _zrow(w, 7), _zrow(w, 8),
                _zrow(LANES, 0),
                pl.BlockSpec((None, GLA_RANK, kw), lambda j: (l, 0, 0)),
                pl.BlockSpec((1, kw), lambda j: (0, 0)),
                pl.BlockSpec((1, GLA_DV), lambda j: (0, 0))]
    args = [z_b, z_b, z_b, z_b, z_glr_b, w_up, b_up[l].reshape(1, kw), g_norm[l].reshape(1, GLA_DV)]
    return _gate_sample_call(functools.partial(_gla_sample_kernel, row_sub=st.row_sub),
                             in_specs, args, st, h, w_in_t, state, new_state, l,
                             GLA_HEADS, GLA_DK, GLA_DV, "gate_gla_sample")


def kernel(x_prompt, x_sample, state_hgrn, state_gla, c_prompt, c_sample, w_ada, b_ada, g_pre_mix, g_post_mix, g_pre_mlp, g_post_mlp, w_in, sgu_ln_g, sgu_ln_b, sgu_w_s, sgu_b_s, hg_lb, hg_norm_g, gla_w_up, gla_b_up, gla_norm_g, w_branch, w_out, w_mlp_up, w_mlp_down):
    sa, sb = STREAM_A, STREAM_B
    x_p = x_prompt.reshape(BATCH * SEQ, D_MODEL)
    x_a = x_p
    x_b = jnp.concatenate([x_p[sa.rows:], x_sample.reshape(DEC_BATCH, D_MODEL)], axis=0)
    c_all = jnp.concatenate([c_sample, c_prompt, jnp.zeros((MOD_ROWS - DEC_BATCH - BATCH, D_MODEL), F32)], axis=0)
    mod = _modulation(c_all, w_ada, b_ada)
    w_in_t = jnp.swapaxes(w_in, 1, 2)
    b_s_t = jnp.swapaxes(sgu_b_s, 1, 2)
    rec_params = dict(hg_lb=hg_lb, hg_norm_g=hg_norm_g, w_up=gla_w_up, b_up=gla_b_up,
                      gla_norm_g=gla_norm_g)

    def lowrank(st, h, l):
        return _matmul(st, h, w_in_t, l, n_out=LANES, col_off=OFF_GLR, tn=LANES, w_transposed=True,
                       name="in_proj_lowrank")

    hg_p, gla_p, v_rows = [], [], []
    hg_s = gla_s = None
    h_a = _prenorm(sa, x_a, g_pre_mix[0], mod, 0, 0, 1)
    h_b = _prenorm(sb, x_b, g_pre_mix[0], mod, 0, 0, 1)
    for l in range(DEPTH):
        z_a = _matmul(sa, h_a, w_in_t, l, n_out=Z_MAIN, w_transposed=True, name="in_proj")
        zg_a = lowrank(sa, h_a, l)
        z_b, yb_a, shg_a, yc_a, sgla_a = _matmul_with_recurrence(
            sb, h_b, w_in_t, l, n_out=Z_MAIN, w_transposed=True, rst=sa, z=z_a, z_glr=zg_a,
            name="in_proj_with_rec", **rec_params)
        zg_b = lowrank(sb, h_b, l)
        (ya_a,) = _sgu(sa, z_a, sgu_ln_g, sgu_ln_b, sgu_w_s, b_s_t, l)
        ya_b, v_s = _sgu(sb, z_b, sgu_ln_g, sgu_ln_b, sgu_w_s, b_s_t, l)
        gates_a, yb_b, hg_s = _gates_with_hgrn_sample(sa, h_a, w_in_t, z_b, hg_lb, hg_norm_g,
                                                      state_hgrn, hg_s, l)
        gates_b, yc_b, gla_s = _gates_with_gla_sample(sb, h_b, w_in_t, z_b, zg_b, gla_w_up, gla_b_up,
                                                      gla_norm_g, state_gla, gla_s, l)

        merged_a = _merge(sa, ya_a, yb_a, yc_a, w_branch, gates_a, l)
        out_a = _matmul(sa, merged_a, w_out, l, n_out=D_MODEL, name="out_proj")
        x_a, h2_a = _resid(sa, x_a, out_a, g_post_mix[l], mod, l, 2, nxt=(g_pre_mlp[l], l, 3, 4))

        up_a, yb_b, shg_b, yc_b, sgla_b = _matmul_with_recurrence(
            sa, h2_a, w_mlp_up, l, n_out=D_FF, act=_act_relu2, out_dtype=BF16, rst=sb, z=z_b,
            z_glr=zg_b, y_b=yb_b, y_c=yc_b, name="mlp_up_with_rec", **rec_params)

        merged_b = _merge(sb, ya_b, yb_b, yc_b, w_branch, gates_b, l)
        out_b = _matmul(sb, merged_b, w_out, l, n_out=D_MODEL, name="out_proj")
        x_b, h2_b = _resid(sb, x_b, out_b, g_post_mix[l], mod, l, 2, nxt=(g_pre_mlp[l], l, 3, 4))
        up_b = _matmul(sb, h2_b, w_mlp_up, l, n_out=D_FF, act=_act_relu2, out_dtype=BF16, name="mlp_up")

        nxt = (g_pre_mix[l + 1], l + 1, 0, 1) if l + 1 < DEPTH else None
        outs = []
        for st, x, up in ((sa, x_a, up_a), (sb, x_b, up_b)):
            y2 = _matmul(st, up, w_mlp_down, l, n_out=D_MODEL, tm=st.rows // 2, tn=1024, tk=1024,
                         row_sub=st.rows // 4, name="mlp_down")
            outs.append(_resid(st, x, y2, g_post_mlp[l], mod, l, 5, nxt=nxt))
        x_a, x_b = outs[0][0], outs[1][0]
        if nxt is not None:
            h_a, h_b = outs[0][1], outs[1][1]

        hg_p.append(jnp.concatenate([shg_a, shg_b], axis=0))
        gla_p.append(jnp.concatenate([sgla_a, sgla_b], axis=0))
        v_rows.append(v_s.reshape(DEC_BATCH, 1, BRANCH_WIDTH))

    y_prompt = jnp.concatenate([x_a, x_b[:sb.prompt_rows]], axis=0).reshape(BATCH, SEQ, D_MODEL)
    y_sample = x_b[sb.prompt_rows:].reshape(DEC_BATCH, 1, D_MODEL)
    return (y_prompt, y_sample, jnp.stack(hg_p), jnp.stack(gla_p), hg_s, gla_s, jnp.stack(v_rows))
```

```python
import functools
from typing import NamedTuple

import jax
import jax.numpy as jnp
from jax import lax
from jax.experimental import pallas as pl
from jax.experimental.pallas import tpu as pltpu

F32 = jnp.float32
BF16 = jnp.bfloat16

D_MODEL = 2048
BATCH = 4
SEQ = 2048
DEPTH = 4
DEC_BATCH = 128
BRANCH_WIDTH = D_MODEL // 2
N_BRANCH = 3
SGU_GROUPS = 8
SGU_CHUNK = 128
HG_HEADS = 8
HG_DK = BRANCH_WIDTH // HG_HEADS
HG_DV = BRANCH_WIDTH // HG_HEADS
GLA_HEADS = 4
GLA_DK_TOTAL = BRANCH_WIDTH // 2
GLA_DK = GLA_DK_TOTAL // GLA_HEADS
GLA_DV = BRANCH_WIDTH // GLA_HEADS
GLA_RANK = 16
GLA_TAU = 16.0
D_FF = 4 * D_MODEL
EPS = 1e-6

N_PROMPT = BATCH * SEQ
N_ALL = N_PROMPT + DEC_BATCH
MOD_ROWS = DEC_BATCH + 8

OFF_GLR = 2 * BRANCH_WIDTH + 4 * BRANCH_WIDTH + 2 * GLA_DK_TOTAL + 2 * BRANCH_WIDTH
OFF_GATE = OFF_GLR + GLA_RANK
Z_MAIN = OFF_GLR

LANES = 128
ROW_TILE = 128
NORM_TILE = 640
MAX_SPLIT_EXPONENT = 60.0
MM_TM = 4160
MM_ROW_SUB = 832
MM_TM_HALF = 2080
REC_CHUNK = 128
HALF = REC_CHUNK // 2
SAMPLE_NB = 8
GATE_TN = 256
WIDE_TN = 512
VMEM_LIMIT = 56 * 1024 * 1024


def _cparams(sem):
    return pltpu.CompilerParams(dimension_semantics=sem, vmem_limit_bytes=VMEM_LIMIT)


def _sigmoid(x):
    return 1.0 / (1.0 + jnp.exp(-x))


def _silu(x):
    return x * _sigmoid(x)


def _log_sigmoid(x):
    return jnp.minimum(x, 0.0) - jnp.log(1.0 + jnp.exp(-jnp.abs(x)))


def _rms(x, g):
    return x * lax.rsqrt(jnp.mean(x * x, axis=-1, keepdims=True) + EPS) * g


def _mod_kernel(c_ref, w_ref, b_ref, o_ref):
    a = _silu(c_ref[...]).astype(BF16)
    o_ref[...] = jnp.dot(a, w_ref[...].astype(BF16), preferred_element_type=F32) + b_ref[...]


def _modulation(c_all, w_ada, b_ada):
    tn = 1024
    n = 6 * D_MODEL
    return pl.pallas_call(
        _mod_kernel,
        grid=(DEPTH, n // tn),
        in_specs=[
            pl.BlockSpec((MOD_ROWS, D_MODEL), lambda l, j: (0, 0)),
            pl.BlockSpec((None, D_MODEL, tn), lambda l, j: (l, 0, j)),
            pl.BlockSpec((None, 1, tn), lambda l, j: (l, 0, j)),
        ],
        out_specs=pl.BlockSpec((None, MOD_ROWS, tn), lambda l, j: (l, 0, j)),
        out_shape=jax.ShapeDtypeStruct((DEPTH, MOD_ROWS, n), F32),
        compiler_params=_cparams(("parallel", "parallel")),
        name="adaln_mod",
    )(c_all, w_ada, b_ada.reshape(DEPTH, 1, n))


def _mod_specs(l, j):
    return [
        pl.BlockSpec((None, DEC_BATCH, D_MODEL), lambda i: (l, 0, j)),
        pl.BlockSpec((None, 8, D_MODEL), lambda i: (l, DEC_BATCH // 8, j)),
    ]


def _pick_mod(s_ref, p_ref, sub):
    g = pl.program_id(0) * (NORM_TILE // ROW_TILE) + sub
    b = jnp.minimum(g // (SEQ // ROW_TILE), BATCH - 1)
    is_sample = g >= N_PROMPT // ROW_TILE
    return jnp.where(is_sample, s_ref[...], p_ref[pl.ds(b, 1), :])


def _prenorm_kernel(x_ref, g_ref, sc_s, sc_p, sh_s, sh_p, h_ref):
    for sub in range(NORM_TILE // ROW_TILE):
        rows = slice(sub * ROW_TILE, (sub + 1) * ROW_TILE)
        h = (_rms(x_ref[rows, :], g_ref[...]) * (1.0 + _pick_mod(sc_s, sc_p, sub))
             + _pick_mod(sh_s, sh_p, sub))
        h_ref[rows, :] = h.astype(BF16)


def _prenorm(x, g, mod, l, j_shift, j_scale):
    row = pl.BlockSpec((NORM_TILE, D_MODEL), lambda i: (i, 0))
    vec = pl.BlockSpec((1, D_MODEL), lambda i: (0, 0))
    return pl.pallas_call(
        _prenorm_kernel,
        grid=(N_ALL // NORM_TILE,),
        in_specs=[row, vec] + _mod_specs(l, j_scale) + _mod_specs(l, j_shift),
        out_specs=row,
        out_shape=jax.ShapeDtypeStruct((N_ALL, D_MODEL), BF16),
        compiler_params=_cparams(("parallel",)),
        name="prenorm",
    )(x, g.reshape(1, D_MODEL), mod, mod, mod, mod)


def _resid_kernel(x_ref, y_ref, gpost_ref, gt_s, gt_p, *rest, with_next):
    for sub in range(NORM_TILE // ROW_TILE):
        rows = slice(sub * ROW_TILE, (sub + 1) * ROW_TILE)
        x_new = x_ref[rows, :] + _pick_mod(gt_s, gt_p, sub) * _rms(y_ref[rows, :], gpost_ref[...])
        if with_next:
            gpre_ref, sc_s, sc_p, sh_s, sh_p, xo_ref, h_ref = rest
            xo_ref[rows, :] = x_new
            h = (_rms(x_new, gpre_ref[...]) * (1.0 + _pick_mod(sc_s, sc_p, sub))
                 + _pick_mod(sh_s, sh_p, sub))
            h_ref[rows, :] = h.astype(BF16)
        else:
            (xo_ref,) = rest
            xo_ref[rows, :] = x_new


def _resid(x, y, g_post, mod, l_gate, j_gate, nxt=None):
    row = pl.BlockSpec((NORM_TILE, D_MODEL), lambda i: (i, 0))
    vec = pl.BlockSpec((1, D_MODEL), lambda i: (0, 0))
    in_specs = [row, row, vec] + _mod_specs(l_gate, j_gate)
    args = [x, y, g_post.reshape(1, D_MODEL), mod, mod]
    out_specs = [row]
    out_shape = [jax.ShapeDtypeStruct((N_ALL, D_MODEL), F32)]
    if nxt is not None:
        g_pre, l_n, j_shift, j_scale = nxt
        in_specs += [vec] + _mod_specs(l_n, j_scale) + _mod_specs(l_n, j_shift)
        args += [g_pre.reshape(1, D_MODEL), mod, mod, mod, mod]
        out_specs.append(row)
        out_shape.append(jax.ShapeDtypeStruct((N_ALL, D_MODEL), BF16))
    return pl.pallas_call(
        functools.partial(_resid_kernel, with_next=nxt is not None),
        grid=(N_ALL // NORM_TILE,),
        in_specs=in_specs,
        out_specs=out_specs,
        out_shape=out_shape,
        compiler_params=_cparams(("parallel",)),
        name="resid_norm",
    )(*args)


def _act_none(x):
    return x


def _act_relu2(x):
    return jnp.square(jnp.maximum(x, 0.0))


def _nt(a, b):
    return lax.dot_general(a, b, (((1,), (1,)), ((), ())), preferred_element_type=F32)


def _mm_kernel(a_ref, w_ref, o_ref, *, nk, act, w_transposed, row_sub):
    w = w_ref[...].astype(BF16)
    if nk > 1:
        @pl.when(pl.program_id(2) == 0)
        def _():
            o_ref[...] = jnp.zeros_like(o_ref)

    for r in range(0, a_ref.shape[0], row_sub):
        rows = slice(r, r + row_sub)
        a = a_ref[rows, :]
        part = _nt(a, w) if w_transposed else jnp.dot(a, w, preferred_element_type=F32)
        if nk == 1:
            o_ref[rows, :] = act(part).astype(o_ref.dtype)
        else:
            o_ref[rows, :] += part


def _matmul(a, w, l, *, n_out, col_off=0, tm=MM_TM, tn=256, tk=None, row_sub=MM_ROW_SUB,
            act=_act_none, out_dtype=F32, w_transposed=False, single_buffer_a=False, name):
    k_dim = a.shape[1]
    tk = k_dim if tk is None else tk
    nk = k_dim // tk
    assert nk == 1 or (act is _act_none and out_dtype == F32)
    if single_buffer_a:
        a_spec = pl.BlockSpec((tm, tk), lambda i, j, k: (i, k), pipeline_mode=pl.Buffered(1))
    else:
        a_spec = pl.BlockSpec((tm, tk), lambda i, j, k: (i, k))
    if w_transposed:
        assert col_off % 8 == 0 and tn % 8 == 0 and tk % LANES == 0
        w_spec = pl.BlockSpec((None, pl.Element(tn), pl.Element(tk)),
                              lambda i, j, k: (l, pl.multiple_of(col_off + j * tn, 8),
                                               pl.multiple_of(k * tk, LANES)))
    else:
        w_spec = pl.BlockSpec((None, tk, tn), lambda i, j, k: (l, k, col_off // tn + j))
    return pl.pallas_call(
        functools.partial(_mm_kernel, nk=nk, act=act, w_transposed=w_transposed, row_sub=row_sub),
        grid=(N_ALL // tm, n_out // tn, nk),
        in_specs=[a_spec, w_spec],
        out_specs=pl.BlockSpec((tm, tn), lambda i, j, k: (i, j)),
        out_shape=jax.ShapeDtypeStruct((N_ALL, n_out), out_dtype),
        compiler_params=_cparams(("parallel", "parallel", "arbitrary")),
        name=name,
    )(a, w)


def _merge_kernel(ya_ref, yb_ref, yc_ref, wa_ref, wb_ref, wc_ref, ga_ref, gb_ref, gc_ref, o_ref):
    wa = wa_ref[...].astype(BF16)
    wb = wb_ref[...].astype(BF16)
    wc = wc_ref[...].astype(BF16)
    sub = MM_TM_HALF // 2
    for r in range(0, MM_TM_HALF, sub):
        rows = slice(r, r + sub)
        merged = ga_ref[rows, :] * jnp.dot(ya_ref[rows, :], wa, preferred_element_type=F32)
        merged = merged + gb_ref[rows, :] * jnp.dot(yb_ref[rows, :], wb, preferred_element_type=F32)
        merged = merged + gc_ref[rows, :] * jnp.dot(yc_ref[rows, :], wc, preferred_element_type=F32)
        o_ref[rows, :] = merged.astype(BF16)


def _merge(ya, yb, yc, w_branch, gates, l):
    tn = 256
    tm = MM_TM_HALF
    nb = D_MODEL // tn
    y_spec = pl.BlockSpec((tm, BRANCH_WIDTH), lambda i, j: (i, 0), pipeline_mode=pl.Buffered(1))

    def w_spec(b):
        return pl.BlockSpec((None, None, BRANCH_WIDTH, tn), lambda i, j: (l, b, 0, j))

    def g_spec(b):
        return pl.BlockSpec((tm, tn), lambda i, j: (i, b * nb + j))

    return pl.pallas_call(
        _merge_kernel,
        grid=(N_ALL // tm, nb),
        in_specs=[y_spec, y_spec, y_spec, w_spec(0), w_spec(1), w_spec(2),
                  g_spec(0), g_spec(1), g_spec(2)],
        out_specs=pl.BlockSpec((tm, tn), lambda i, j: (i, j)),
        out_shape=jax.ShapeDtypeStruct((N_ALL, D_MODEL), BF16),
        compiler_params=_cparams(("parallel", "parallel")),
        name="branch_merge",
    )(ya, yb, yc, w_branch, w_branch, w_branch, gates, gates, gates)


def _sgu_kernel(zu_ref, zv_ref, lg_ref, lb_ref, ws_ref, bst_ref, y_ref, vs_ref):
    i = pl.program_id(0)
    u = jax.nn.gelu(zu_ref[...], approximate=True)
    v = jax.nn.gelu(zv_ref[...], approximate=True)
    mu = jnp.mean(v, axis=-1, keepdims=True)
    vc = v - mu
    v = vc * lax.rsqrt(jnp.mean(vc * vc, axis=-1, keepdims=True) + EPS) * lg_ref[...] + lb_ref[...]
    is_sample = i == N_PROMPT // ROW_TILE

    @pl.when(jnp.logical_not(is_sample))
    def _():
        n_idx = lax.broadcasted_iota(jnp.int32, (SGU_CHUNK, SGU_CHUNK), 0)
        m_idx = lax.broadcasted_iota(jnp.int32, (SGU_CHUNK, SGU_CHUNK), 1)
        causal = m_idx <= n_idx
        for g in range(SGU_GROUPS):
            sl = slice(g * LANES, (g + 1) * LANES)
            wm = jnp.where(causal, ws_ref[g], 0.0).astype(BF16)
            s = jnp.dot(wm, v[:, sl].astype(BF16), preferred_element_type=F32) + bst_ref[:, g:g + 1]
            y_ref[:, sl] = (u[:, sl] * s).astype(BF16)

    @pl.when(is_sample)
    def _():
        vs_ref[...] = v
        for g in range(SGU_GROUPS):
            sl = slice(g * LANES, (g + 1) * LANES)
            s = v[:, sl] * ws_ref[g][0:1, 0:1] + bst_ref[0:1, g:g + 1]
            y_ref[:, sl] = (u[:, sl] * s).astype(BF16)


def _sgu(z, ln_g, ln_b, w_s, b_s_t, l):
    vec = pl.BlockSpec((1, BRANCH_WIDTH), lambda i: (0, 0))
    return pl.pallas_call(
        _sgu_kernel,
        grid=(N_ALL // ROW_TILE,),
        in_specs=[
            pl.BlockSpec((ROW_TILE, BRANCH_WIDTH), lambda i: (i, 0)),
            pl.BlockSpec((ROW_TILE, BRANCH_WIDTH), lambda i: (i, 1)),
            vec, vec,
            pl.BlockSpec((None, SGU_GROUPS, SGU_CHUNK, SGU_CHUNK), lambda i: (l, 0, 0, 0)),
            pl.BlockSpec((None, SGU_CHUNK, SGU_GROUPS), lambda i: (l, 0, 0)),
        ],
        out_specs=[
            pl.BlockSpec((ROW_TILE, BRANCH_WIDTH), lambda i: (i, 0)),
            pl.BlockSpec((DEC_BATCH, BRANCH_WIDTH), lambda i: (0, 0)),
        ],
        out_shape=[
            jax.ShapeDtypeStruct((N_ALL, BRANCH_WIDTH), BF16),
            jax.ShapeDtypeStruct((DEC_BATCH, BRANCH_WIDTH), F32),
        ],
        compiler_params=_cparams(("arbitrary",)),
        name="sgu",
    )(z, z, ln_g[l].reshape(1, BRANCH_WIDTH), ln_b[l].reshape(1, BRANCH_WIDTH), w_s, b_s_t)


def _hgrn_lower_bound(lb_ref, l):
    raw = lb_ref[...]
    e = jnp.exp(raw - jnp.max(raw, axis=0, keepdims=True))
    p = e / jnp.sum(e, axis=0, keepdims=True)
    acc = p[0:1, :]
    for j in range(1, l + 1):
        acc = acc + p[j:j + 1, :]
    return jnp.maximum(acc - p[0:1, :], 0.0)


def _hgrn_prep(zq, zf, lb_ref, l):
    e = jnp.exp(-jnp.abs(zf))
    ope = 1.0 + e
    sig_neg = jnp.where(zf >= 0.0, e, 1.0) / ope
    log_sig = jnp.minimum(zf, 0.0) - jnp.log(ope)
    if l == 0:
        return _silu(zq), sig_neg, log_sig
    lb = _hgrn_lower_bound(lb_ref, l)
    a = jnp.log(lb)
    c = jnp.log1p(-lb) + log_sig
    log_f = jnp.maximum(a, c) + jnp.log(1.0 + jnp.exp(-jnp.abs(a - c)))
    return _silu(zq), (1.0 - lb) * sig_neg, log_f


def _gla_prep(zq, zk, glr, wup_ref, bup_ref):
    x = jnp.dot(glr[:, :GLA_RANK].astype(BF16), wup_ref[...].astype(BF16),
                preferred_element_type=F32) + bup_ref[...]
    log_a = _log_sigmoid(x) / GLA_TAU
    return zq, zk * (GLA_DK ** -0.5), log_a


def _lane_bcast_col(row):
    return jnp.broadcast_to(row, (LANES, LANES)).T


def _head_out(o, g_row, gate):
    o = o * lax.rsqrt(jnp.mean(o * o, axis=-1, keepdims=True) + EPS) * g_row
    return o * _silu(gate)


class _RecRefs(NamedTuple):
    gnorm: object
    y: object
    s_out: object
    s: object
    s_prev: object
    q: object
    b: object
    o_intra: object


def _rec_scratch(heads, dk, dv):
    return [
        pltpu.VMEM((heads, dk, dv), F32),
        pltpu.VMEM((heads, dk, dv), F32),
        pltpu.VMEM((REC_CHUNK, heads * dk), F32),
        pltpu.VMEM((REC_CHUNK, heads * dk), F32),
        pltpu.VMEM((heads, REC_CHUNK, dv), F32),
    ]


def _finish_head(h, o_intra, q_dec, k_dec, b_last, vh, gate, s_old, refs, dv):
    vs = slice(h * dv, (h + 1) * dv)
    o = o_intra + jnp.dot(q_dec.astype(BF16), s_old.astype(BF16), preferred_element_type=F32)
    refs.y[:, vs] = _head_out(o, refs.gnorm[...], gate[:, vs]).astype(BF16)
    upd = jnp.dot(k_dec.T.astype(BF16), vh, preferred_element_type=F32)
    decay = _lane_bcast_col(jnp.exp(b_last))
    for j in range(dv // LANES):
        ls = slice(j * LANES, (j + 1) * LANES)
        refs.s[h, :, ls] = decay * s_old[:, ls] + upd[:, ls]


def _chunk_start(c, refs):
    @pl.when(c == 0)
    def _():
        refs.s[...] = jnp.zeros_like(refs.s)


def _chunk_main(q, k, v, log_f, gate, refs, *, heads, dk, dv):
    t_idx = lax.broadcasted_iota(jnp.int32, (REC_CHUNK, REC_CHUNK), 0)
    s_idx = lax.broadcasted_iota(jnp.int32, (REC_CHUNK, REC_CHUNK), 1)
    tril = jnp.where(s_idx <= t_idx, 1.0, 0.0).astype(F32)
    b_all = jnp.dot(tril, log_f, precision=lax.Precision.HIGHEST, preferred_element_type=F32)
    v16 = v.astype(BF16)

    mid = HALF // 2 - 1

    def drop(lo, hi):
        return b_all[lo:lo + 1] - b_all[hi:hi + 1]

    worst = jnp.maximum(jnp.maximum(drop(0, mid), drop(mid, HALF - 1)),
                        jnp.maximum(drop(HALF, HALF + mid), drop(HALF + mid, REC_CHUNK - 1)))
    unsafe = jnp.max(worst) > MAX_SPLIT_EXPONENT

    causal_half = (lax.broadcasted_iota(jnp.int32, (HALF, HALF), 1)
                   <= lax.broadcasted_iota(jnp.int32, (HALF, HALF), 0))
    for h in range(heads):
        ks = slice(h * dk, (h + 1) * dk)
        qh, kh, bh = q[:, ks], k[:, ks], b_all[:, ks]
        vh = v16[:, h * dv:(h + 1) * dv]
        b_a, b_b = bh[:HALF], bh[HALF:]
        v_a, v_b = vh[:HALF], vh[HALF:]
        r_a = b_a[mid:mid + 1]
        r_b = b_b[mid:mid + 1]
        r_m = b_a[HALF - 1:HALF]
        b_last = b_b[HALF - 1:HALF]
        q_a = qh[:HALF] * jnp.exp(b_a - r_a)
        q_b = qh[HALF:] * jnp.exp(b_b - r_b)
        k_a = kh[:HALF] * jnp.exp(r_a - b_a)
        k_b = kh[HALF:] * jnp.exp(r_b - b_b)
        att_aa = jnp.where(causal_half, _nt(q_a.astype(BF16), k_a.astype(BF16)), 0.0).astype(BF16)
        att_bb = jnp.where(causal_half, _nt(q_b.astype(BF16), k_b.astype(BF16)), 0.0).astype(BF16)
        att_ba = _nt((q_b * jnp.exp(r_b - r_m)).astype(BF16),
                     (k_a * jnp.exp(r_m - r_a)).astype(BF16)).astype(BF16)
        o_a = jnp.dot(att_aa, v_a, preferred_element_type=F32)
        o_b = (jnp.dot(att_ba, v_a, preferred_element_type=F32)
               + jnp.dot(att_bb, v_b, preferred_element_type=F32))
        q_dec = jnp.concatenate([q_a * jnp.exp(r_a), q_b * jnp.exp(r_b)], axis=0)
        k_dec = jnp.concatenate([k_a * jnp.exp(b_last - r_a), k_b * jnp.exp(b_last - r_b)], axis=0)
        s_old = refs.s[h]
        refs.s_prev[h] = s_old
        _finish_head(h, jnp.concatenate([o_a, o_b], axis=0), q_dec, k_dec, b_last, vh, gate, s_old,
                     refs, dv)
    return unsafe, b_all, v16


def _chunk_redo(unsafe, q, k, v, b_all, v16, gate, refs, *, heads, dk, dv):
    @pl.when(unsafe)
    def _():
        refs.q[...] = q
        refs.b[...] = b_all
        s_pos = lax.broadcasted_iota(jnp.int32, (REC_CHUNK, 1), 0)
        for h in range(heads):
            ks = slice(h * dk, (h + 1) * dk)
            vs = slice(h * dv, (h + 1) * dv)
            qh, kh, bh = q[:, ks], k[:, ks], b_all[:, ks]
            vh = v[:, vs]

            def row(t, carry):
                b_t = refs.b[pl.ds(t, 1), :][:, ks]
                q_t = refs.q[pl.ds(t, 1), :][:, ks]
                p = jnp.exp(jnp.minimum(b_t - bh, 0.0)) * kh * q_t
                w = jnp.where(s_pos <= t, jnp.sum(p, axis=1, keepdims=True), 0.0)
                refs.o_intra[h, pl.ds(t, 1), :] = jnp.sum(w * vh, axis=0, keepdims=True)
                return carry

            lax.fori_loop(0, REC_CHUNK, row, 0)
            b_last = bh[REC_CHUNK - 1:REC_CHUNK]
            _finish_head(h, refs.o_intra[h], qh * jnp.exp(bh), kh * jnp.exp(b_last - bh), b_last,
                         v16[:, vs], gate, refs.s_prev[h], refs, dv)


def _chunk_end(last, refs):
    @pl.when(last)
    def _():
        refs.s_out[...] = refs.s[...]


def _prompt_rec_kernel(hq_ref, hf_ref, hi_ref, hg_ref, lb_ref, hgn_ref,
                       gq_ref, gk_ref, gv_ref, gr_ref, glr_ref, wup_ref, bup_ref, glan_ref,
                       yb_ref, shg_ref, yc_ref, sgla_ref, *scratch, l):
    hg = _RecRefs(hgn_ref, yb_ref, shg_ref, *scratch[:5])
    gla = _RecRefs(glan_ref, yc_ref, sgla_ref, *scratch[5:])
    hg_dims = dict(heads=HG_HEADS, dk=HG_DK, dv=HG_DV)
    gla_dims = dict(heads=GLA_HEADS, dk=GLA_DK, dv=GLA_DV)
    c = pl.program_id(1)
    last = c == pl.num_programs(1) - 1
    _chunk_start(c, hg)
    _chunk_start(c, gla)

    hq, hk, hlf = _hgrn_prep(hq_ref[...], hf_ref[...], lb_ref, l)
    hv, hgate = hi_ref[...], hg_ref[...]
    h_unsafe, h_b, h_v16 = _chunk_main(hq, hk, hv, hlf, hgate, hg, **hg_dims)
    gq, gk, gla_la = _gla_prep(gq_ref[...], gk_ref[...], glr_ref[...], wup_ref, bup_ref)
    gv, ggate = gv_ref[...], gr_ref[...]
    g_unsafe, g_b, g_v16 = _chunk_main(gq, gk, gv, gla_la, ggate, gla, **gla_dims)

    _chunk_redo(h_unsafe, hq, hk, hv, h_b, h_v16, hgate, hg, **hg_dims)
    _chunk_redo(g_unsafe, gq, gk, gv, g_b, g_v16, ggate, gla, **gla_dims)
    _chunk_end(last, hg)
    _chunk_end(last, gla)


def _prompt_recurrences(z, z_glr, hg_lb, hg_norm_g, w_up, b_up, gla_norm_g, l):
    n_c = SEQ // REC_CHUNK
    w = BRANCH_WIDTH
    kw = GLA_DK_TOTAL

    def zcol(width, block):
        return pl.BlockSpec((REC_CHUNK, width), lambda b, c: (b * n_c + c, block))

    def const(shape):
        return pl.BlockSpec(shape, lambda b, c: (0,) * len(shape))

    def state_spec(heads, dk, dv):
        return pl.BlockSpec((None, heads, dk, dv), lambda b, c: (b, 0, 0, 0))

    y_spec = pl.BlockSpec((REC_CHUNK, w), lambda b, c: (b * n_c + c, 0))
    in_specs = [
        zcol(w, 2), zcol(w, 3), zcol(w, 4), zcol(w, 5), const((DEPTH, w)), const((1, HG_DV)),
        zcol(kw, 6 * w // kw), zcol(kw, 6 * w // kw + 1), zcol(w, 7), zcol(w, 8), zcol(LANES, 0),
        pl.BlockSpec((None, GLA_RANK, kw), lambda b, c: (l, 0, 0)), const((1, kw)), const((1, GLA_DV)),
    ]
    args = [z, z, z, z, hg_lb, hg_norm_g[l].reshape(1, HG_DV),
            z, z, z, z, z_glr, w_up, b_up[l].reshape(1, kw), gla_norm_g[l].reshape(1, GLA_DV)]
    return pl.pallas_call(
        functools.partial(_prompt_rec_kernel, l=l),
        grid=(BATCH, n_c),
        in_specs=in_specs,
        out_specs=[y_spec, state_spec(HG_HEADS, HG_DK, HG_DV),
                   y_spec, state_spec(GLA_HEADS, GLA_DK, GLA_DV)],
        out_shape=[
            jax.ShapeDtypeStruct((N_ALL, w), BF16),
            jax.ShapeDtypeStruct((BATCH, HG_HEADS, HG_DK, HG_DV), F32),
            jax.ShapeDtypeStruct((N_ALL, w), BF16),
            jax.ShapeDtypeStruct((BATCH, GLA_HEADS, GLA_DK, GLA_DV), F32),
        ],
        scratch_shapes=_rec_scratch(HG_HEADS, HG_DK, HG_DV) + _rec_scratch(GLA_HEADS, GLA_DK, GLA_DV),
        compiler_params=_cparams(("parallel", "arbitrary")),
        name="prompt_recurrences",
    )(*args)


def _sample_step(q, k, v, log_f, gate, gnorm_ref, s_ref, so_ref, i, *, heads, dk, dv):
    outs = []
    for h in range(heads):
        ks = slice(h * dk, (h + 1) * dk)
        f_col = _lane_bcast_col(jnp.exp(log_f[:, ks]))
        k_col = _lane_bcast_col(k[:, ks])
        q16 = jnp.broadcast_to(q[:, ks], (16, dk)).astype(BF16)
        parts = []
        for j in range(dv // LANES):
            ls = slice(j * LANES, (j + 1) * LANES)
            v_row = v[:, h * dv + j * LANES:h * dv + (j + 1) * LANES]
            s_new = f_col * s_ref[i, h, :, ls] + k_col * v_row
            so_ref[i, h, :, ls] = s_new
            parts.append(jnp.dot(q16, s_new.astype(BF16), preferred_element_type=F32)[0:1])
        o = parts[0] if len(parts) == 1 else jnp.concatenate(parts, axis=1)
        outs.append(_head_out(o, gnorm_ref[...], gate[:, h * dv:(h + 1) * dv]))
    return jnp.concatenate(outs, axis=1)


def _flush_sample_rows(y_ref, yacc_ref):
    @pl.when(pl.program_id(0) == pl.num_programs(0) - 1)
    def _():
        y_ref[...] = yacc_ref[...].astype(BF16)


def _gate_tile(h_ref, w_ref, g_ref):
    w = w_ref[...].astype(BF16)
    for r in range(0, MM_TM, MM_ROW_SUB):
        rows = slice(r, r + MM_ROW_SUB)
        g_ref[rows, :] = _sigmoid(_nt(h_ref[rows, :], w)).astype(BF16)


def _unit_index(step):
    return jnp.minimum(step, DEC_BATCH // SAMPLE_NB - 1)


def _hgrn_sample_kernel(h_ref, w_ref, zq_ref, zf_ref, zi_ref, zg_ref, lb_ref, gnorm_ref, s_ref, *rest, l):
    g_ref, y_ref, so_ref, yacc_ref = rest[-4:]
    _gate_tile(h_ref, w_ref, g_ref)
    base = _unit_index(pl.program_id(0)) * SAMPLE_NB
    q, k, log_f = _hgrn_prep(zq_ref[...], zf_ref[...], lb_ref, l)
    v, gate = zi_ref[...], zg_ref[...]
    for i in range(SAMPLE_NB):
        row = slice(i, i + 1)
        yacc_ref[pl.ds(base + i, 1), :] = _sample_step(
            q[row], k[row], v[row], log_f[row], gate[row], gnorm_ref, s_ref, so_ref, i,
            heads=HG_HEADS, dk=HG_DK, dv=HG_DV)
    _flush_sample_rows(y_ref, yacc_ref)


def _gla_sample_kernel(h_ref, w_ref, zq_ref, zk_ref, zv_ref, zr_ref, glr_ref, wup_ref, bup_ref,
                       gnorm_ref, s_ref, *rest):
    g_ref, y_ref, so_ref, yacc_ref = rest[-4:]
    _gate_tile(h_ref, w_ref, g_ref)
    base = _unit_index(pl.program_id(0)) * SAMPLE_NB
    glr = jnp.concatenate([glr_ref[...], jnp.zeros((16 - SAMPLE_NB, LANES), F32)], axis=0)
    q, k, log_a = _gla_prep(zq_ref[...], zk_ref[...], glr, wup_ref, bup_ref)
    v, gate = zv_ref[...], zr_ref[...]
    for i in range(SAMPLE_NB):
        row = slice(i, i + 1)
        yacc_ref[pl.ds(base + i, 1), :] = _sample_step(
            q[row], k[row], v[row], log_a[row], gate[row], gnorm_ref, s_ref, so_ref, i,
            heads=GLA_HEADS, dk=GLA_DK, dv=GLA_DV)
    _flush_sample_rows(y_ref, yacc_ref)


def _zrow(width, block):
    return pl.BlockSpec((SAMPLE_NB, width), lambda j: (N_PROMPT // SAMPLE_NB + _unit_index(j), block))


def _gate_sample_call(kernel, in_specs, args, h, w_in_t, state, y_all, new_state, gates, row_tile, l,
                      heads, dk, dv, name):
    tn = GATE_TN
    n_steps = N_BRANCH * D_MODEL // tn
    assert n_steps >= DEC_BATCH // SAMPLE_NB
    h_spec = pl.BlockSpec((MM_TM, D_MODEL), lambda j: (row_tile, 0), pipeline_mode=pl.Buffered(1))
    w_spec = pl.BlockSpec((None, pl.Element(tn), pl.Element(D_MODEL)),
                          lambda j: (l, pl.multiple_of(OFF_GATE + j * tn, 8), 0))
    st = pl.BlockSpec((None, SAMPLE_NB, heads, dk, dv), lambda j: (l, _unit_index(j), 0, 0, 0))
    any_spec = pl.BlockSpec(memory_space=pl.ANY)
    in_specs = [h_spec, w_spec] + in_specs + [st, any_spec]
    args = [h, w_in_t] + list(args) + [state, y_all]
    aliases = {len(args) - 1: 1}
    for buf, out_idx in ((new_state, 2), (gates, 0)):
        if buf is not None:
            aliases[len(args)] = out_idx
            in_specs.append(any_spec)
            args.append(buf)
    return pl.pallas_call(
        kernel,
        grid=(n_steps,),
        in_specs=in_specs,
        out_specs=[
            pl.BlockSpec((MM_TM, tn), lambda j: (row_tile, j)),
            pl.BlockSpec((DEC_BATCH, BRANCH_WIDTH), lambda j: (N_PROMPT // DEC_BATCH, 0)),
            st,
        ],
        out_shape=[
            jax.ShapeDtypeStruct((N_ALL, N_BRANCH * D_MODEL), BF16),
            jax.ShapeDtypeStruct((N_ALL, BRANCH_WIDTH), BF16),
            jax.ShapeDtypeStruct((DEPTH, DEC_BATCH, heads, dk, dv), F32),
        ],
        scratch_shapes=[pltpu.VMEM((DEC_BATCH, BRANCH_WIDTH), F32)],
        input_output_aliases=aliases,
        compiler_params=_cparams(("arbitrary",)),
        name=name,
    )(*args)


def _hgrn_sample(h, w_in_t, z, hg_lb, g_norm, state, y_all, new_state, gates, row_tile, l):
    w = BRANCH_WIDTH
    in_specs = [_zrow(w, 2), _zrow(w, 3), _zrow(w, 4), _zrow(w, 5),
                pl.BlockSpec((DEPTH, w), lambda j: (0, 0)),
                pl.BlockSpec((1, HG_DV), lambda j: (0, 0))]
    args = [z, z, z, z, hg_lb, g_norm[l].reshape(1, HG_DV)]
    return _gate_sample_call(functools.partial(_hgrn_sample_kernel, l=l), in_specs, args, h, w_in_t,
                             state, y_all, new_state, gates, row_tile, l,
                             HG_HEADS, HG_DK, HG_DV, "gate_hgrn_sample")


def _gla_sample(h, w_in_t, z, z_glr, w_up, b_up, g_norm, state, y_all, new_state, gates, row_tile, l):
    w = BRANCH_WIDTH
    kw = GLA_DK_TOTAL
    in_specs = [_zrow(kw, 6 * w // kw), _zrow(kw, 6 * w // kw + 1), _zrow(w, 7), _zrow(w, 8),
                _zrow(LANES, 0),
                pl.BlockSpec((None, GLA_RANK, kw), lambda j: (l, 0, 0)),
                pl.BlockSpec((1, kw), lambda j: (0, 0)),
                pl.BlockSpec((1, GLA_DV), lambda j: (0, 0))]
    args = [z, z, z, z, z_glr, w_up, b_up[l].reshape(1, kw), g_norm[l].reshape(1, GLA_DV)]
    return _gate_sample_call(_gla_sample_kernel, in_specs, args, h, w_in_t, state, y_all, new_state,
                             gates, row_tile, l, GLA_HEADS, GLA_DK, GLA_DV, "gate_gla_sample")


def kernel(x_prompt, x_sample, state_hgrn, state_gla, c_prompt, c_sample, w_ada, b_ada, g_pre_mix, g_post_mix, g_pre_mlp, g_post_mlp, w_in, sgu_ln_g, sgu_ln_b, sgu_w_s, sgu_b_s, hg_lb, hg_norm_g, gla_w_up, gla_b_up, gla_norm_g, w_branch, w_out, w_mlp_up, w_mlp_down):
    x = jnp.concatenate([x_prompt.reshape(N_PROMPT, D_MODEL), x_sample.reshape(DEC_BATCH, D_MODEL)], axis=0)
    c_all = jnp.concatenate([c_sample, c_prompt, jnp.zeros((MOD_ROWS - DEC_BATCH - BATCH, D_MODEL), F32)], axis=0)
    mod = _modulation(c_all, w_ada, b_ada)
    w_in_t = jnp.swapaxes(w_in, 1, 2)
    b_s_t = jnp.swapaxes(sgu_b_s, 1, 2)

    hg_p, gla_p, v_rows = [], [], []
    hg_s = gla_s = None
    h = _prenorm(x, g_pre_mix[0], mod, 0, 0, 1)
    for l in range(DEPTH):
        z = _matmul(h, w_in_t, l, n_out=Z_MAIN, tn=WIDE_TN, w_transposed=True, single_buffer_a=True,
                    name="in_proj")
        z_glr = _matmul(h, w_in_t, l, n_out=LANES, col_off=OFF_GLR, tn=LANES, w_transposed=True,
                        name="in_proj_lowrank")

        y_a, v_s = _sgu(z, sgu_ln_g, sgu_ln_b, sgu_w_s, b_s_t, l)
        y_b, s_hg_p, y_c, s_gla_p = _prompt_recurrences(z, z_glr, hg_lb, hg_norm_g, gla_w_up, gla_b_up,
                                                        gla_norm_g, l)
        gates, y_b, hg_s = _hgrn_sample(h, w_in_t, z, hg_lb, hg_norm_g, state_hgrn, y_b, hg_s,
                                        None, 0, l)
        gates, y_c, gla_s = _gla_sample(h, w_in_t, z, z_glr, gla_w_up, gla_b_up, gla_norm_g,
                                        state_gla, y_c, gla_s, gates, 1, l)

        merged = _merge(y_a, y_b, y_c, w_branch, gates, l)
        out = _matmul(merged, w_out, l, n_out=D_MODEL, tn=WIDE_TN, single_buffer_a=True,
                      name="out_proj")
        x, h2 = _resid(x, out, g_post_mix[l], mod, l, 2, nxt=(g_pre_mlp[l], l, 3, 4))

        up = _matmul(h2, w_mlp_up, l, n_out=D_FF, tn=WIDE_TN, act=_act_relu2, out_dtype=BF16,
                     single_buffer_a=True, name="mlp_up")
        y2 = _matmul(up, w_mlp_down, l, n_out=D_MODEL, tm=MM_TM_HALF, tn=1024, tk=1024,
                     row_sub=MM_TM_HALF // 2, name="mlp_down")
        if l + 1 < DEPTH:
            x, h = _resid(x, y2, g_post_mlp[l], mod, l, 5, nxt=(g_pre_mix[l + 1], l + 1, 0, 1))
        else:
            (x,) = _resid(x, y2, g_post_mlp[l], mod, l, 5)

        hg_p.append(s_hg_p)
        gla_p.append(s_gla_p)
        v_rows.append(v_s.reshape(DEC_BATCH, 1, BRANCH_WIDTH))

    y_prompt = x[:N_PROMPT].reshape(BATCH, SEQ, D_MODEL)
    y_sample = x[N_PROMPT:].reshape(DEC_BATCH, 1, D_MODEL)
    return (y_prompt, y_sample, jnp.stack(hg_p), jnp.stack(gla_p), hg_s, gla_s, jnp.stack(v_rows))
```

```python
import functools
from typing import NamedTuple

import jax
import jax.numpy as jnp
from jax import lax
from jax.experimental import pallas as pl
from jax.experimental.pallas import tpu as pltpu

F32 = jnp.float32
BF16 = jnp.bfloat16

D_MODEL = 2048
BATCH = 4
SEQ = 2048
DEPTH = 4
DEC_BATCH = 128
BRANCH_WIDTH = D_MODEL // 2
N_BRANCH = 3
SGU_GROUPS = 8
SGU_CHUNK = 128
HG_HEADS = 8
HG_DK = BRANCH_WIDTH // HG_HEADS
HG_DV = BRANCH_WIDTH // HG_HEADS
GLA_HEADS = 4
GLA_DK_TOTAL = BRANCH_WIDTH // 2
GLA_DK = GLA_DK_TOTAL // GLA_HEADS
GLA_DV = BRANCH_WIDTH // GLA_HEADS
GLA_RANK = 16
GLA_TAU = 16.0
D_FF = 4 * D_MODEL
EPS = 1e-6

N_PROMPT = BATCH * SEQ
N_ALL = N_PROMPT + DEC_BATCH
MOD_ROWS = DEC_BATCH + 8

OFF_GLR = 2 * BRANCH_WIDTH + 4 * BRANCH_WIDTH + 2 * GLA_DK_TOTAL + 2 * BRANCH_WIDTH
OFF_GATE = OFF_GLR + GLA_RANK
Z_MAIN = OFF_GLR

LANES = 128
ROW_TILE = 128
NORM_TILE = 640
MAX_SPLIT_EXPONENT = 60.0
MM_TM = 4160
MM_ROW_SUB = 832
MM_TM_HALF = 2080
REC_CHUNK = 128
HALF = REC_CHUNK // 2
SAMPLE_NB = 8
GATE_TN = 256
WIDE_TN = 512
VMEM_LIMIT = 56 * 1024 * 1024


def _cparams(sem):
    return pltpu.CompilerParams(dimension_semantics=sem, vmem_limit_bytes=VMEM_LIMIT)


def _sigmoid(x):
    return 1.0 / (1.0 + jnp.exp(-x))


def _silu(x):
    return x * _sigmoid(x)


def _log_sigmoid(x):
    return jnp.minimum(x, 0.0) - jnp.log(1.0 + jnp.exp(-jnp.abs(x)))


def _rms(x, g):
    return x * lax.rsqrt(jnp.mean(x * x, axis=-1, keepdims=True) + EPS) * g


def _mod_kernel(c_ref, w_ref, b_ref, o_ref):
    a = _silu(c_ref[...]).astype(BF16)
    o_ref[...] = jnp.dot(a, w_ref[...].astype(BF16), preferred_element_type=F32) + b_ref[...]


def _modulation(c_all, w_ada, b_ada):
    tn = 1024
    n = 6 * D_MODEL
    return pl.pallas_call(
        _mod_kernel,
        grid=(DEPTH, n // tn),
        in_specs=[
            pl.BlockSpec((MOD_ROWS, D_MODEL), lambda l, j: (0, 0)),
            pl.BlockSpec((None, D_MODEL, tn), lambda l, j: (l, 0, j)),
            pl.BlockSpec((None, 1, tn), lambda l, j: (l, 0, j)),
        ],
        out_specs=pl.BlockSpec((None, MOD_ROWS, tn), lambda l, j: (l, 0, j)),
        out_shape=jax.ShapeDtypeStruct((DEPTH, MOD_ROWS, n), F32),
        compiler_params=_cparams(("parallel", "parallel")),
        name="adaln_mod",
    )(c_all, w_ada, b_ada.reshape(DEPTH, 1, n))


def _mod_specs(l, j):
    return [
        pl.BlockSpec((None, DEC_BATCH, D_MODEL), lambda i: (l, 0, j)),
        pl.BlockSpec((None, 8, D_MODEL), lambda i: (l, DEC_BATCH // 8, j)),
    ]


def _pick_mod(s_ref, p_ref, sub):
    g = pl.program_id(0) * (NORM_TILE // ROW_TILE) + sub
    b = jnp.minimum(g // (SEQ // ROW_TILE), BATCH - 1)
    is_sample = g >= N_PROMPT // ROW_TILE
    return jnp.where(is_sample, s_ref[...], p_ref[pl.ds(b, 1), :])


def _prenorm_kernel(x_ref, g_ref, sc_s, sc_p, sh_s, sh_p, h_ref):
    for sub in range(NORM_TILE // ROW_TILE):
        rows = slice(sub * ROW_TILE, (sub + 1) * ROW_TILE)
        h = (_rms(x_ref[rows, :], g_ref[...]) * (1.0 + _pick_mod(sc_s, sc_p, sub))
             + _pick_mod(sh_s, sh_p, sub))
        h_ref[rows, :] = h.astype(BF16)


def _prenorm(x, g, mod, l, j_shift, j_scale):
    row = pl.BlockSpec((NORM_TILE, D_MODEL), lambda i: (i, 0))
    vec = pl.BlockSpec((1, D_MODEL), lambda i: (0, 0))
    return pl.pallas_call(
        _prenorm_kernel,
        grid=(N_ALL // NORM_TILE,),
        in_specs=[row, vec] + _mod_specs(l, j_scale) + _mod_specs(l, j_shift),
        out_specs=row,
        out_shape=jax.ShapeDtypeStruct((N_ALL, D_MODEL), BF16),
        compiler_params=_cparams(("parallel",)),
        name="prenorm",
    )(x, g.reshape(1, D_MODEL), mod, mod, mod, mod)


def _resid_kernel(x_ref, y_ref, gpost_ref, gt_s, gt_p, *rest, with_next):
    for sub in range(NORM_TILE // ROW_TILE):
        rows = slice(sub * ROW_TILE, (sub + 1) * ROW_TILE)
        x_new = x_ref[rows, :] + _pick_mod(gt_s, gt_p, sub) * _rms(y_ref[rows, :], gpost_ref[...])
        if with_next:
            gpre_ref, sc_s, sc_p, sh_s, sh_p, xo_ref, h_ref = rest
            xo_ref[rows, :] = x_new
            h = (_rms(x_new, gpre_ref[...]) * (1.0 + _pick_mod(sc_s, sc_p, sub))
                 + _pick_mod(sh_s, sh_p, sub))
            h_ref[rows, :] = h.astype(BF16)
        else:
            (xo_ref,) = rest
            xo_ref[rows, :] = x_new


def _resid(x, y, g_post, mod, l_gate, j_gate, nxt=None):
    row = pl.BlockSpec((NORM_TILE, D_MODEL), lambda i: (i, 0))
    vec = pl.BlockSpec((1, D_MODEL), lambda i: (0, 0))
    in_specs = [row, row, vec] + _mod_specs(l_gate, j_gate)
    args = [x, y, g_post.reshape(1, D_MODEL), mod, mod]
    out_specs = [row]
    out_shape = [jax.ShapeDtypeStruct((N_ALL, D_MODEL), F32)]
    if nxt is not None:
        g_pre, l_n, j_shift, j_scale = nxt
        in_specs += [vec] + _mod_specs(l_n, j_scale) + _mod_specs(l_n, j_shift)
        args += [g_pre.reshape(1, D_MODEL), mod, mod, mod, mod]
        out_specs.append(row)
        out_shape.append(jax.ShapeDtypeStruct((N_ALL, D_MODEL), BF16))
    return pl.pallas_call(
        functools.partial(_resid_kernel, with_next=nxt is not None),
        grid=(N_ALL // NORM_TILE,),
        in_specs=in_specs,
        out_specs=out_specs,
        out_shape=out_shape,
        compiler_params=_cparams(("parallel",)),
        name="resid_norm",
    )(*args)


def _act_none(x):
    return x


def _act_relu2(x):
    return jnp.square(jnp.maximum(x, 0.0))


def _nt(a, b):
    return lax.dot_general(a, b, (((1,), (1,)), ((), ())), preferred_element_type=F32)


def _mm_kernel(a_ref, w_ref, o_ref, *, nk, act, w_transposed, row_sub):
    w = w_ref[...].astype(BF16)
    if nk > 1:
        @pl.when(pl.program_id(2) == 0)
        def _():
            o_ref[...] = jnp.zeros_like(o_ref)

    for r in range(0, a_ref.shape[0], row_sub):
        rows = slice(r, r + row_sub)
        a = a_ref[rows, :]
        part = _nt(a, w) if w_transposed else jnp.dot(a, w, preferred_element_type=F32)
        if nk == 1:
            o_ref[rows, :] = act(part).astype(o_ref.dtype)
        else:
            o_ref[rows, :] += part


def _matmul(a, w, l, *, n_out, col_off=0, tm=MM_TM, tn=256, tk=None, row_sub=MM_ROW_SUB,
            act=_act_none, out_dtype=F32, w_transposed=False, single_buffer_a=False, name):
    k_dim = a.shape[1]
    tk = k_dim if tk is None else tk
    nk = k_dim // tk
    assert nk == 1 or (act is _act_none and out_dtype == F32)
    if single_buffer_a:
        a_spec = pl.BlockSpec((tm, tk), lambda i, j, k: (i, k), pipeline_mode=pl.Buffered(1))
    else:
        a_spec = pl.BlockSpec((tm, tk), lambda i, j, k: (i, k))
    if w_transposed:
        assert col_off % 8 == 0 and tn % 8 == 0 and tk % LANES == 0
        w_spec = pl.BlockSpec((None, pl.Element(tn), pl.Element(tk)),
                              lambda i, j, k: (l, pl.multiple_of(col_off + j * tn, 8),
                                               pl.multiple_of(k * tk, LANES)))
    else:
        w_spec = pl.BlockSpec((None, tk, tn), lambda i, j, k: (l, k, col_off // tn + j))
    return pl.pallas_call(
        functools.partial(_mm_kernel, nk=nk, act=act, w_transposed=w_transposed, row_sub=row_sub),
        grid=(N_ALL // tm, n_out // tn, nk),
        in_specs=[a_spec, w_spec],
        out_specs=pl.BlockSpec((tm, tn), lambda i, j, k: (i, j)),
        out_shape=jax.ShapeDtypeStruct((N_ALL, n_out), out_dtype),
        compiler_params=_cparams(("parallel", "parallel", "arbitrary")),
        name=name,
    )(a, w)


def _merge_kernel(ya_ref, yb_ref, yc_ref, wa_ref, wb_ref, wc_ref, ga_ref, gb_ref, gc_ref, o_ref):
    wa = wa_ref[...].astype(BF16)
    wb = wb_ref[...].astype(BF16)
    wc = wc_ref[...].astype(BF16)
    sub = MM_TM_HALF // 2
    for r in range(0, MM_TM_HALF, sub):
        rows = slice(r, r + sub)
        merged = ga_ref[rows, :] * jnp.dot(ya_ref[rows, :], wa, preferred_element_type=F32)
        merged = merged + gb_ref[rows, :] * jnp.dot(yb_ref[rows, :], wb, preferred_element_type=F32)
        merged = merged + gc_ref[rows, :] * jnp.dot(yc_ref[rows, :], wc, preferred_element_type=F32)
        o_ref[rows, :] = merged.astype(BF16)


def _merge(ya, yb, yc, w_branch, gates, l):
    tn = WIDE_TN
    tm = MM_TM_HALF
    nb = D_MODEL // tn
    y_spec = pl.BlockSpec((tm, BRANCH_WIDTH), lambda i, j: (i, 0), pipeline_mode=pl.Buffered(1))

    def w_spec(b):
        return pl.BlockSpec((None, None, BRANCH_WIDTH, tn), lambda i, j: (l, b, 0, j))

    def g_spec(b):
        return pl.BlockSpec((tm, tn), lambda i, j: (i, b * nb + j))

    return pl.pallas_call(
        _merge_kernel,
        grid=(N_ALL // tm, nb),
        in_specs=[y_spec, y_spec, y_spec, w_spec(0), w_spec(1), w_spec(2),
                  g_spec(0), g_spec(1), g_spec(2)],
        out_specs=pl.BlockSpec((tm, tn), lambda i, j: (i, j)),
        out_shape=jax.ShapeDtypeStruct((N_ALL, D_MODEL), BF16),
        compiler_params=_cparams(("parallel", "parallel")),
        name="branch_merge",
    )(ya, yb, yc, w_branch, w_branch, w_branch, gates, gates, gates)


def _sgu_uv(zu_ref, zv_ref, lg_ref, lb_ref):
    u = jax.nn.gelu(zu_ref[...], approximate=True)
    v = jax.nn.gelu(zv_ref[...], approximate=True)
    mu = jnp.mean(v, axis=-1, keepdims=True)
    vc = v - mu
    v = vc * lax.rsqrt(jnp.mean(vc * vc, axis=-1, keepdims=True) + EPS) * lg_ref[...] + lb_ref[...]
    return u, v


def _sgu_prompt_chunk(zu_ref, zv_ref, lg_ref, lb_ref, ws_ref, bst_ref, y_ref):
    u, v = _sgu_uv(zu_ref, zv_ref, lg_ref, lb_ref)
    n_idx = lax.broadcasted_iota(jnp.int32, (SGU_CHUNK, SGU_CHUNK), 0)
    m_idx = lax.broadcasted_iota(jnp.int32, (SGU_CHUNK, SGU_CHUNK), 1)
    causal = m_idx <= n_idx
    for g in range(SGU_GROUPS):
        sl = slice(g * LANES, (g + 1) * LANES)
        wm = jnp.where(causal, ws_ref[g], 0.0).astype(BF16)
        s = jnp.dot(wm, v[:, sl].astype(BF16), preferred_element_type=F32) + bst_ref[:, g:g + 1]
        y_ref[:, sl] = (u[:, sl] * s).astype(BF16)


def _sgu_sample_kernel(zu_ref, zv_ref, lg_ref, lb_ref, ws_ref, bst_ref, ya_any, y_ref, vs_ref):
    del ya_any
    u, v = _sgu_uv(zu_ref, zv_ref, lg_ref, lb_ref)
    vs_ref[...] = v
    for g in range(SGU_GROUPS):
        sl = slice(g * LANES, (g + 1) * LANES)
        s = v[:, sl] * ws_ref[g][0:1, 0:1] + bst_ref[0:1, g:g + 1]
        y_ref[:, sl] = (u[:, sl] * s).astype(BF16)


def _sgu_sample(z, y_a, ln_g, ln_b, w_s, b_s_t, l):
    blk = N_PROMPT // ROW_TILE
    vec = pl.BlockSpec((1, BRANCH_WIDTH), lambda i: (0, 0))
    tile = pl.BlockSpec((ROW_TILE, BRANCH_WIDTH), lambda i: (blk, 0))
    return pl.pallas_call(
        _sgu_sample_kernel,
        grid=(1,),
        in_specs=[
            tile,
            pl.BlockSpec((ROW_TILE, BRANCH_WIDTH), lambda i: (blk, 1)),
            vec, vec,
            pl.BlockSpec((None, SGU_GROUPS, SGU_CHUNK, SGU_CHUNK), lambda i: (l, 0, 0, 0)),
            pl.BlockSpec((None, SGU_CHUNK, SGU_GROUPS), lambda i: (l, 0, 0)),
            pl.BlockSpec(memory_space=pl.ANY),
        ],
        out_specs=[tile, pl.BlockSpec((DEC_BATCH, BRANCH_WIDTH), lambda i: (0, 0))],
        out_shape=[
            jax.ShapeDtypeStruct((N_ALL, BRANCH_WIDTH), BF16),
            jax.ShapeDtypeStruct((DEC_BATCH, BRANCH_WIDTH), F32),
        ],
        input_output_aliases={6: 0},
        compiler_params=_cparams(("arbitrary",)),
        name="sgu_sample",
    )(z, z, ln_g[l].reshape(1, BRANCH_WIDTH), ln_b[l].reshape(1, BRANCH_WIDTH), w_s, b_s_t, y_a)


def _hgrn_lower_bound(lb_ref, l):
    raw = lb_ref[...]
    e = jnp.exp(raw - jnp.max(raw, axis=0, keepdims=True))
    p = e / jnp.sum(e, axis=0, keepdims=True)
    acc = p[0:1, :]
    for j in range(1, l + 1):
        acc = acc + p[j:j + 1, :]
    return jnp.maximum(acc - p[0:1, :], 0.0)


def _hgrn_prep(zq, zf, lb_ref, l):
    e = jnp.exp(-jnp.abs(zf))
    ope = 1.0 + e
    sig_neg = jnp.where(zf >= 0.0, e, 1.0) / ope
    log_sig = jnp.minimum(zf, 0.0) - jnp.log(ope)
    if l == 0:
        return _silu(zq), sig_neg, log_sig
    lb = _hgrn_lower_bound(lb_ref, l)
    a = jnp.log(lb)
    c = jnp.log1p(-lb) + log_sig
    log_f = jnp.maximum(a, c) + jnp.log(1.0 + jnp.exp(-jnp.abs(a - c)))
    return _silu(zq), (1.0 - lb) * sig_neg, log_f


def _gla_prep(zq, zk, glr, wup_ref, bup_ref):
    x = jnp.dot(glr[:, :GLA_RANK].astype(BF16), wup_ref[...].astype(BF16),
                preferred_element_type=F32) + bup_ref[...]
    log_a = _log_sigmoid(x) / GLA_TAU
    return zq, zk * (GLA_DK ** -0.5), log_a


def _lane_bcast_col(row):
    return jnp.broadcast_to(row, (LANES, LANES)).T


def _head_out(o, g_row, gate):
    o = o * lax.rsqrt(jnp.mean(o * o, axis=-1, keepdims=True) + EPS) * g_row
    return o * _silu(gate)


class _RecRefs(NamedTuple):
    gnorm: object
    y: object
    s_out: object
    s: object
    s_prev: object
    q: object
    b: object
    o_intra: object


def _rec_scratch(heads, dk, dv):
    return [
        pltpu.VMEM((heads, dk, dv), F32),
        pltpu.VMEM((heads, dk, dv), F32),
        pltpu.VMEM((REC_CHUNK, heads * dk), F32),
        pltpu.VMEM((REC_CHUNK, heads * dk), F32),
        pltpu.VMEM((heads, REC_CHUNK, dv), F32),
    ]


def _finish_head(h, o_intra, q_dec, k_dec, b_last, vh, gate, s_old, refs, dv):
    vs = slice(h * dv, (h + 1) * dv)
    o = o_intra + jnp.dot(q_dec.astype(BF16), s_old.astype(BF16), preferred_element_type=F32)
    refs.y[:, vs] = _head_out(o, refs.gnorm[...], gate[:, vs]).astype(BF16)
    upd = jnp.dot(k_dec.T.astype(BF16), vh, preferred_element_type=F32)
    decay = _lane_bcast_col(jnp.exp(b_last))
    for j in range(dv // LANES):
        ls = slice(j * LANES, (j + 1) * LANES)
        refs.s[h, :, ls] = decay * s_old[:, ls] + upd[:, ls]


def _chunk_start(c, refs):
    @pl.when(c == 0)
    def _():
        refs.s[...] = jnp.zeros_like(refs.s)


def _chunk_main(q, k, v, log_f, gate, refs, *, heads, dk, dv):
    t_idx = lax.broadcasted_iota(jnp.int32, (REC_CHUNK, REC_CHUNK), 0)
    s_idx = lax.broadcasted_iota(jnp.int32, (REC_CHUNK, REC_CHUNK), 1)
    tril = jnp.where(s_idx <= t_idx, 1.0, 0.0).astype(F32)
    b_all = jnp.dot(tril, log_f, precision=lax.Precision.HIGHEST, preferred_element_type=F32)
    v16 = v.astype(BF16)

    mid = HALF // 2 - 1

    def drop(lo, hi):
        return b_all[lo:lo + 1] - b_all[hi:hi + 1]

    worst = jnp.maximum(jnp.maximum(drop(0, mid), drop(mid, HALF - 1)),
                        jnp.maximum(drop(HALF, HALF + mid), drop(HALF + mid, REC_CHUNK - 1)))
    unsafe = jnp.max(worst) > MAX_SPLIT_EXPONENT

    causal_half = (lax.broadcasted_iota(jnp.int32, (HALF, HALF), 1)
                   <= lax.broadcasted_iota(jnp.int32, (HALF, HALF), 0))
    for h in range(heads):
        ks = slice(h * dk, (h + 1) * dk)
        qh, kh, bh = q[:, ks], k[:, ks], b_all[:, ks]
        vh = v16[:, h * dv:(h + 1) * dv]
        b_a, b_b = bh[:HALF], bh[HALF:]
        v_a, v_b = vh[:HALF], vh[HALF:]
        r_a = b_a[mid:mid + 1]
        r_b = b_b[mid:mid + 1]
        r_m = b_a[HALF - 1:HALF]
        b_last = b_b[HALF - 1:HALF]
        q_a = qh[:HALF] * jnp.exp(b_a - r_a)
        q_b = qh[HALF:] * jnp.exp(b_b - r_b)
        k_a = kh[:HALF] * jnp.exp(r_a - b_a)
        k_b = kh[HALF:] * jnp.exp(r_b - b_b)
        att_aa = jnp.where(causal_half, _nt(q_a.astype(BF16), k_a.astype(BF16)), 0.0).astype(BF16)
        att_bb = jnp.where(causal_half, _nt(q_b.astype(BF16), k_b.astype(BF16)), 0.0).astype(BF16)
        att_ba = _nt((q_b * jnp.exp(r_b - r_m)).astype(BF16),
                     (k_a * jnp.exp(r_m - r_a)).astype(BF16)).astype(BF16)
        o_a = jnp.dot(att_aa, v_a, preferred_element_type=F32)
        o_b = (jnp.dot(att_ba, v_a, preferred_element_type=F32)
               + jnp.dot(att_bb, v_b, preferred_element_type=F32))
        q_dec = jnp.concatenate([q_a * jnp.exp(r_a), q_b * jnp.exp(r_b)], axis=0)
        k_dec = jnp.concatenate([k_a * jnp.exp(b_last - r_a), k_b * jnp.exp(b_last - r_b)], axis=0)
        s_old = refs.s[h]
        refs.s_prev[h] = s_old
        _finish_head(h, jnp.concatenate([o_a, o_b], axis=0), q_dec, k_dec, b_last, vh, gate, s_old,
                     refs, dv)
    return unsafe, b_all, v16


def _chunk_redo(unsafe, q, k, v, b_all, v16, gate, refs, *, heads, dk, dv):
    @pl.when(unsafe)
    def _():
        refs.q[...] = q
        refs.b[...] = b_all
        s_pos = lax.broadcasted_iota(jnp.int32, (REC_CHUNK, 1), 0)
        for h in range(heads):
            ks = slice(h * dk, (h + 1) * dk)
            vs = slice(h * dv, (h + 1) * dv)
            qh, kh, bh = q[:, ks], k[:, ks], b_all[:, ks]
            vh = v[:, vs]

            def row(t, carry):
                b_t = refs.b[pl.ds(t, 1), :][:, ks]
                q_t = refs.q[pl.ds(t, 1), :][:, ks]
                p = jnp.exp(jnp.minimum(b_t - bh, 0.0)) * kh * q_t
                w = jnp.where(s_pos <= t, jnp.sum(p, axis=1, keepdims=True), 0.0)
                refs.o_intra[h, pl.ds(t, 1), :] = jnp.sum(w * vh, axis=0, keepdims=True)
                return carry

            lax.fori_loop(0, REC_CHUNK, row, 0)
            b_last = bh[REC_CHUNK - 1:REC_CHUNK]
            _finish_head(h, refs.o_intra[h], qh * jnp.exp(bh), kh * jnp.exp(b_last - bh), b_last,
                         v16[:, vs], gate, refs.s_prev[h], refs, dv)


def _chunk_end(last, refs):
    @pl.when(last)
    def _():
        refs.s_out[...] = refs.s[...]


def _prompt_mixers_kernel(zu_ref, zv_ref, lng_ref, lnb_ref, ws_ref, bst_ref,
                          hq_ref, hf_ref, hi_ref, hg_ref, lb_ref, hgn_ref,
                          gq_ref, gk_ref, gv_ref, gr_ref, glr_ref, wup_ref, bup_ref, glan_ref,
                          ya_ref, yb_ref, shg_ref, yc_ref, sgla_ref, *scratch, l):
    hg = _RecRefs(hgn_ref, yb_ref, shg_ref, *scratch[:5])
    gla = _RecRefs(glan_ref, yc_ref, sgla_ref, *scratch[5:])
    hg_dims = dict(heads=HG_HEADS, dk=HG_DK, dv=HG_DV)
    gla_dims = dict(heads=GLA_HEADS, dk=GLA_DK, dv=GLA_DV)
    c = pl.program_id(1)
    last = c == pl.num_programs(1) - 1
    _chunk_start(c, hg)
    _chunk_start(c, gla)

    _sgu_prompt_chunk(zu_ref, zv_ref, lng_ref, lnb_ref, ws_ref, bst_ref, ya_ref)

    hq, hk, hlf = _hgrn_prep(hq_ref[...], hf_ref[...], lb_ref, l)
    hv, hgate = hi_ref[...], hg_ref[...]
    h_unsafe, h_b, h_v16 = _chunk_main(hq, hk, hv, hlf, hgate, hg, **hg_dims)
    gq, gk, gla_la = _gla_prep(gq_ref[...], gk_ref[...], glr_ref[...], wup_ref, bup_ref)
    gv, ggate = gv_ref[...], gr_ref[...]
    g_unsafe, g_b, g_v16 = _chunk_main(gq, gk, gv, gla_la, ggate, gla, **gla_dims)

    _chunk_redo(h_unsafe, hq, hk, hv, h_b, h_v16, hgate, hg, **hg_dims)
    _chunk_redo(g_unsafe, gq, gk, gv, g_b, g_v16, ggate, gla, **gla_dims)
    _chunk_end(last, hg)
    _chunk_end(last, gla)


def _prompt_mixers(z, z_glr, ln_g, ln_b, w_s, b_s_t, hg_lb, hg_norm_g, w_up, b_up, gla_norm_g, l):
    n_c = SEQ // REC_CHUNK
    w = BRANCH_WIDTH
    kw = GLA_DK_TOTAL
    assert REC_CHUNK == SGU_CHUNK

    def zcol(width, block):
        return pl.BlockSpec((REC_CHUNK, width), lambda b, c: (b * n_c + c, block))

    def const(shape):
        return pl.BlockSpec(shape, lambda b, c: (0,) * len(shape))

    def state_spec(heads, dk, dv):
        return pl.BlockSpec((None, heads, dk, dv), lambda b, c: (b, 0, 0, 0))

    y_spec = pl.BlockSpec((REC_CHUNK, w), lambda b, c: (b * n_c + c, 0))
    in_specs = [
        zcol(w, 0), zcol(w, 1), const((1, w)), const((1, w)),
        pl.BlockSpec((None, SGU_GROUPS, SGU_CHUNK, SGU_CHUNK), lambda b, c: (l, 0, 0, 0)),
        pl.BlockSpec((None, SGU_CHUNK, SGU_GROUPS), lambda b, c: (l, 0, 0)),
        zcol(w, 2), zcol(w, 3), zcol(w, 4), zcol(w, 5), const((DEPTH, w)), const((1, HG_DV)),
        zcol(kw, 6 * w // kw), zcol(kw, 6 * w // kw + 1), zcol(w, 7), zcol(w, 8), zcol(LANES, 0),
        pl.BlockSpec((None, GLA_RANK, kw), lambda b, c: (l, 0, 0)), const((1, kw)), const((1, GLA_DV)),
    ]
    args = [z, z, ln_g[l].reshape(1, w), ln_b[l].reshape(1, w), w_s, b_s_t,
            z, z, z, z, hg_lb, hg_norm_g[l].reshape(1, HG_DV),
            z, z, z, z, z_glr, w_up, b_up[l].reshape(1, kw), gla_norm_g[l].reshape(1, GLA_DV)]
    return pl.pallas_call(
        functools.partial(_prompt_mixers_kernel, l=l),
        grid=(BATCH, n_c),
        in_specs=in_specs,
        out_specs=[y_spec, y_spec, state_spec(HG_HEADS, HG_DK, HG_DV),
                   y_spec, state_spec(GLA_HEADS, GLA_DK, GLA_DV)],
        out_shape=[
            jax.ShapeDtypeStruct((N_ALL, w), BF16),
            jax.ShapeDtypeStruct((N_ALL, w), BF16),
            jax.ShapeDtypeStruct((BATCH, HG_HEADS, HG_DK, HG_DV), F32),
            jax.ShapeDtypeStruct((N_ALL, w), BF16),
            jax.ShapeDtypeStruct((BATCH, GLA_HEADS, GLA_DK, GLA_DV), F32),
        ],
        scratch_shapes=_rec_scratch(HG_HEADS, HG_DK, HG_DV) + _rec_scratch(GLA_HEADS, GLA_DK, GLA_DV),
        compiler_params=_cparams(("parallel", "arbitrary")),
        name="prompt_mixers",
    )(*args)


def _sample_step(q, k, v, log_f, gate, gnorm_ref, s_ref, so_ref, i, *, heads, dk, dv):
    outs = []
    for h in range(heads):
        ks = slice(h * dk, (h + 1) * dk)
        f_col = _lane_bcast_col(jnp.exp(log_f[:, ks]))
        k_col = _lane_bcast_col(k[:, ks])
        q16 = jnp.broadcast_to(q[:, ks], (16, dk)).astype(BF16)
        parts = []
        for j in range(dv // LANES):
            ls = slice(j * LANES, (j + 1) * LANES)
            v_row = v[:, h * dv + j * LANES:h * dv + (j + 1) * LANES]
            s_new = f_col * s_ref[i, h, :, ls] + k_col * v_row
            so_ref[i, h, :, ls] = s_new
            parts.append(jnp.dot(q16, s_new.astype(BF16), preferred_element_type=F32)[0:1])
        o = parts[0] if len(parts) == 1 else jnp.concatenate(parts, axis=1)
        outs.append(_head_out(o, gnorm_ref[...], gate[:, h * dv:(h + 1) * dv]))
    return jnp.concatenate(outs, axis=1)


def _flush_sample_rows(y_ref, yacc_ref):
    @pl.when(pl.program_id(0) == pl.num_programs(0) - 1)
    def _():
        y_ref[...] = yacc_ref[...].astype(BF16)


def _gate_tile(h_ref, w_ref, g_ref):
    w = w_ref[...].astype(BF16)
    for r in range(0, MM_TM, MM_ROW_SUB):
        rows = slice(r, r + MM_ROW_SUB)
        g_ref[rows, :] = _sigmoid(_nt(h_ref[rows, :], w)).astype(BF16)


def _unit_index(step):
    return jnp.minimum(step, DEC_BATCH // SAMPLE_NB - 1)


def _hgrn_sample_kernel(h_ref, w_ref, zq_ref, zf_ref, zi_ref, zg_ref, lb_ref, gnorm_ref, s_ref, *rest, l):
    g_ref, y_ref, so_ref, yacc_ref = rest[-4:]
    _gate_tile(h_ref, w_ref, g_ref)
    base = _unit_index(pl.program_id(0)) * SAMPLE_NB
    q, k, log_f = _hgrn_prep(zq_ref[...], zf_ref[...], lb_ref, l)
    v, gate = zi_ref[...], zg_ref[...]
    for i in range(SAMPLE_NB):
        row = slice(i, i + 1)
        yacc_ref[pl.ds(base + i, 1), :] = _sample_step(
            q[row], k[row], v[row], log_f[row], gate[row], gnorm_ref, s_ref, so_ref, i,
            heads=HG_HEADS, dk=HG_DK, dv=HG_DV)
    _flush_sample_rows(y_ref, yacc_ref)


def _gla_sample_kernel(h_ref, w_ref, zq_ref, zk_ref, zv_ref, zr_ref, glr_ref, wup_ref, bup_ref,
                       gnorm_ref, s_ref, *rest):
    g_ref, y_ref, so_ref, yacc_ref = rest[-4:]
    _gate_tile(h_ref, w_ref, g_ref)
    base = _unit_index(pl.program_id(0)) * SAMPLE_NB
    glr = jnp.concatenate([glr_ref[...], jnp.zeros((16 - SAMPLE_NB, LANES), F32)], axis=0)
    q, k, log_a = _gla_prep(zq_ref[...], zk_ref[...], glr, wup_ref, bup_ref)
    v, gate = zv_ref[...], zr_ref[...]
    for i in range(SAMPLE_NB):
        row = slice(i, i + 1)
        yacc_ref[pl.ds(base + i, 1), :] = _sample_step(
            q[row], k[row], v[row], log_a[row], gate[row], gnorm_ref, s_ref, so_ref, i,
            heads=GLA_HEADS, dk=GLA_DK, dv=GLA_DV)
    _flush_sample_rows(y_ref, yacc_ref)


def _zrow(width, block):
    return pl.BlockSpec((SAMPLE_NB, width), lambda j: (N_PROMPT // SAMPLE_NB + _unit_index(j), block))


def _gate_sample_call(kernel, in_specs, args, h, w_in_t, state, y_all, new_state, gates, row_tile, l,
                      heads, dk, dv, name):
    tn = GATE_TN
    n_steps = N_BRANCH * D_MODEL // tn
    assert n_steps >= DEC_BATCH // SAMPLE_NB
    h_spec = pl.BlockSpec((MM_TM, D_MODEL), lambda j: (row_tile, 0), pipeline_mode=pl.Buffered(1))
    w_spec = pl.BlockSpec((None, pl.Element(tn), pl.Element(D_MODEL)),
                          lambda j: (l, pl.multiple_of(OFF_GATE + j * tn, 8), 0))
    st = pl.BlockSpec((None, SAMPLE_NB, heads, dk, dv), lambda j: (l, _unit_index(j), 0, 0, 0))
    any_spec = pl.BlockSpec(memory_space=pl.ANY)
    in_specs = [h_spec, w_spec] + in_specs + [st, any_spec]
    args = [h, w_in_t] + list(args) + [state, y_all]
    aliases = {len(args) - 1: 1}
    for buf, out_idx in ((new_state, 2), (gates, 0)):
        if buf is not None:
            aliases[len(args)] = out_idx
            in_specs.append(any_spec)
            args.append(buf)
    return pl.pallas_call(
        kernel,
        grid=(n_steps,),
        in_specs=in_specs,
        out_specs=[
            pl.BlockSpec((MM_TM, tn), lambda j: (row_tile, j)),
            pl.BlockSpec((DEC_BATCH, BRANCH_WIDTH), lambda j: (N_PROMPT // DEC_BATCH, 0)),
            st,
        ],
        out_shape=[
            jax.ShapeDtypeStruct((N_ALL, N_BRANCH * D_MODEL), BF16),
            jax.ShapeDtypeStruct((N_ALL, BRANCH_WIDTH), BF16),
            jax.ShapeDtypeStruct((DEPTH, DEC_BATCH, heads, dk, dv), F32),
        ],
        scratch_shapes=[pltpu.VMEM((DEC_BATCH, BRANCH_WIDTH), F32)],
        input_output_aliases=aliases,
        compiler_params=_cparams(("arbitrary",)),
        name=name,
    )(*args)


def _hgrn_sample(h, w_in_t, z, hg_lb, g_norm, state, y_all, new_state, gates, row_tile, l):
    w = BRANCH_WIDTH
    in_specs = [_zrow(w, 2), _zrow(w, 3), _zrow(w, 4), _zrow(w, 5),
                pl.BlockSpec((DEPTH, w), lambda j: (0, 0)),
                pl.BlockSpec((1, HG_DV), lambda j: (0, 0))]
    args = [z, z, z, z, hg_lb, g_norm[l].reshape(1, HG_DV)]
    return _gate_sample_call(functools.partial(_hgrn_sample_kernel, l=l), in_specs, args, h, w_in_t,
                             state, y_all, new_state, gates, row_tile, l,
                             HG_HEADS, HG_DK, HG_DV, "gate_hgrn_sample")


def _gla_sample(h, w_in_t, z, z_glr, w_up, b_up, g_norm, state, y_all, new_state, gates, row_tile, l):
    w = BRANCH_WIDTH
    kw = GLA_DK_TOTAL
    in_specs = [_zrow(kw, 6 * w // kw), _zrow(kw, 6 * w // kw + 1), _zrow(w, 7), _zrow(w, 8),
                _zrow(LANES, 0),
                pl.BlockSpec((None, GLA_RANK, kw), lambda j: (l, 0, 0)),
                pl.BlockSpec((1, kw), lambda j: (0, 0)),
                pl.BlockSpec((1, GLA_DV), lambda j: (0, 0))]
    args = [z, z, z, z, z_glr, w_up, b_up[l].reshape(1, kw), g_norm[l].reshape(1, GLA_DV)]
    return _gate_sample_call(_gla_sample_kernel, in_specs, args, h, w_in_t, state, y_all, new_state,
                             gates, row_tile, l, GLA_HEADS, GLA_DK, GLA_DV, "gate_gla_sample")


def kernel(x_prompt, x_sample, state_hgrn, state_gla, c_prompt, c_sample, w_ada, b_ada, g_pre_mix, g_post_mix, g_pre_mlp, g_post_mlp, w_in, sgu_ln_g, sgu_ln_b, sgu_w_s, sgu_b_s, hg_lb, hg_norm_g, gla_w_up, gla_b_up, gla_norm_g, w_branch, w_out, w_mlp_up, w_mlp_down):
    x = jnp.concatenate([x_prompt.reshape(N_PROMPT, D_MODEL), x_sample.reshape(DEC_BATCH, D_MODEL)], axis=0)
    c_all = jnp.concatenate([c_sample, c_prompt, jnp.zeros((MOD_ROWS - DEC_BATCH - BATCH, D_MODEL), F32)], axis=0)
    mod = _modulation(c_all, w_ada, b_ada)
    w_in_t = jnp.swapaxes(w_in, 1, 2)
    b_s_t = jnp.swapaxes(sgu_b_s, 1, 2)

    hg_p, gla_p, v_rows = [], [], []
    hg_s = gla_s = None
    h = _prenorm(x, g_pre_mix[0], mod, 0, 0, 1)
    for l in range(DEPTH):
        z = _matmul(h, w_in_t, l, n_out=Z_MAIN, tn=WIDE_TN, w_transposed=True, single_buffer_a=True,
                    name="in_proj")
        z_glr = _matmul(h, w_in_t, l, n_out=LANES, col_off=OFF_GLR, tn=LANES, w_transposed=True,
                        name="in_proj_lowrank")

        y_a, y_b, s_hg_p, y_c, s_gla_p = _prompt_mixers(
            z, z_glr, sgu_ln_g, sgu_ln_b, sgu_w_s, b_s_t, hg_lb, hg_norm_g, gla_w_up, gla_b_up,
            gla_norm_g, l)
        y_a, v_s = _sgu_sample(z, y_a, sgu_ln_g, sgu_ln_b, sgu_w_s, b_s_t, l)
        gates, y_b, hg_s = _hgrn_sample(h, w_in_t, z, hg_lb, hg_norm_g, state_hgrn, y_b, hg_s,
                                        None, 0, l)
        gates, y_c, gla_s = _gla_sample(h, w_in_t, z, z_glr, gla_w_up, gla_b_up, gla_norm_g,
                                        state_gla, y_c, gla_s, gates, 1, l)

        merged = _merge(y_a, y_b, y_c, w_branch, gates, l)
        out = _matmul(merged, w_out, l, n_out=D_MODEL, name="out_proj")
        x, h2 = _resid(x, out, g_post_mix[l], mod, l, 2, nxt=(g_pre_mlp[l], l, 3, 4))

        up = _matmul(h2, w_mlp_up, l, n_out=D_FF, tn=WIDE_TN, act=_act_relu2, out_dtype=BF16,
                     single_buffer_a=True, name="mlp_up")
        y2 = _matmul(up, w_mlp_down, l, n_out=D_MODEL, tm=MM_TM_HALF, tn=1024, tk=1024,
                     row_sub=MM_TM_HALF // 2, name="mlp_down")
        if l + 1 < DEPTH:
            x, h = _resid(x, y2, g_post_mlp[l], mod, l, 5, nxt=(g_pre_mix[l + 1], l + 1, 0, 1))
        else:
            (x,) = _resid(x, y2, g_post_mlp[l], mod, l, 5)

        hg_p.append(s_hg_p)
        gla_p.append(s_gla_p)
        v_rows.append(v_s.reshape(DEC_BATCH, 1, BRANCH_WIDTH))

    y_prompt = x[:N_PROMPT].reshape(BATCH, SEQ, D_MODEL)
    y_sample = x[N_PROMPT:].reshape(DEC_BATCH, 1, D_MODEL)
    return (y_prompt, y_sample, jnp.stack(hg_p), jnp.stack(gla_p), hg_s, gla_s, jnp.stack(v_rows))
```

```python
import functools
from typing import NamedTuple

import jax
import jax.numpy as jnp
from jax import lax
from jax.experimental import pallas as pl
from jax.experimental.pallas import tpu as pltpu

F32 = jnp.float32
BF16 = jnp.bfloat16

D_MODEL = 2048
BATCH = 4
SEQ = 2048
DEPTH = 4
DEC_BATCH = 128
BRANCH_WIDTH = D_MODEL // 2
N_BRANCH = 3
SGU_GROUPS = 8
SGU_CHUNK = 128
HG_HEADS = 8
HG_DK = BRANCH_WIDTH // HG_HEADS
HG_DV = BRANCH_WIDTH // HG_HEADS
GLA_HEADS = 4
GLA_DK_TOTAL = BRANCH_WIDTH // 2
GLA_DK = GLA_DK_TOTAL // GLA_HEADS
GLA_DV = BRANCH_WIDTH // GLA_HEADS
GLA_RANK = 16
GLA_TAU = 16.0
D_FF = 4 * D_MODEL
EPS = 1e-6

N_PROMPT = BATCH * SEQ
N_ALL = N_PROMPT + DEC_BATCH
MOD_ROWS = DEC_BATCH + 8

OFF_GLR = 2 * BRANCH_WIDTH + 4 * BRANCH_WIDTH + 2 * GLA_DK_TOTAL + 2 * BRANCH_WIDTH
OFF_GATE = OFF_GLR + GLA_RANK
Z_MAIN = OFF_GLR

LANES = 128
ROW_TILE = 128
NORM_TILE = 640
MAX_SPLIT_EXPONENT = 60.0
MM_TM = 4160
MM_ROW_SUB = 832
MM_TM_HALF = 2080
REC_CHUNK = 128
HALF = REC_CHUNK // 2
SAMPLE_NB = 8
GATE_TN = 256
WIDE_TN = 512
VMEM_LIMIT = 56 * 1024 * 1024


def _cparams(sem):
    return pltpu.CompilerParams(dimension_semantics=sem, vmem_limit_bytes=VMEM_LIMIT)


def _sigmoid(x):
    return 1.0 / (1.0 + jnp.exp(-x))


def _silu(x):
    return x * _sigmoid(x)


def _log_sigmoid(x):
    return jnp.minimum(x, 0.0) - jnp.log(1.0 + jnp.exp(-jnp.abs(x)))


def _rms(x, g):
    return x * lax.rsqrt(jnp.mean(x * x, axis=-1, keepdims=True) + EPS) * g


def _mod_kernel(c_ref, w_ref, b_ref, o_ref):
    a = _silu(c_ref[...]).astype(BF16)
    o_ref[...] = jnp.dot(a, w_ref[...].astype(BF16), preferred_element_type=F32) + b_ref[...]


def _modulation(c_all, w_ada, b_ada):
    tn = 1024
    n = 6 * D_MODEL
    return pl.pallas_call(
        _mod_kernel,
        grid=(DEPTH, n // tn),
        in_specs=[
            pl.BlockSpec((MOD_ROWS, D_MODEL), lambda l, j: (0, 0)),
            pl.BlockSpec((None, D_MODEL, tn), lambda l, j: (l, 0, j)),
            pl.BlockSpec((None, 1, tn), lambda l, j: (l, 0, j)),
        ],
        out_specs=pl.BlockSpec((None, MOD_ROWS, tn), lambda l, j: (l, 0, j)),
        out_shape=jax.ShapeDtypeStruct((DEPTH, MOD_ROWS, n), F32),
        compiler_params=_cparams(("parallel", "parallel")),
        name="adaln_mod",
    )(c_all, w_ada, b_ada.reshape(DEPTH, 1, n))


def _mod_specs(l, j):
    return [
        pl.BlockSpec((None, DEC_BATCH, D_MODEL), lambda i: (l, 0, j)),
        pl.BlockSpec((None, 8, D_MODEL), lambda i: (l, DEC_BATCH // 8, j)),
    ]


def _pick_mod(s_ref, p_ref, sub):
    g = pl.program_id(0) * (NORM_TILE // ROW_TILE) + sub
    b = jnp.minimum(g // (SEQ // ROW_TILE), BATCH - 1)
    is_sample = g >= N_PROMPT // ROW_TILE
    return jnp.where(is_sample, s_ref[...], p_ref[pl.ds(b, 1), :])


def _prenorm_kernel(x_ref, g_ref, sc_s, sc_p, sh_s, sh_p, h_ref):
    for sub in range(NORM_TILE // ROW_TILE):
        rows = slice(sub * ROW_TILE, (sub + 1) * ROW_TILE)
        h = (_rms(x_ref[rows, :], g_ref[...]) * (1.0 + _pick_mod(sc_s, sc_p, sub))
             + _pick_mod(sh_s, sh_p, sub))
        h_ref[rows, :] = h.astype(BF16)


def _prenorm(x, g, mod, l, j_shift, j_scale):
    row = pl.BlockSpec((NORM_TILE, D_MODEL), lambda i: (i, 0))
    vec = pl.BlockSpec((1, D_MODEL), lambda i: (0, 0))
    return pl.pallas_call(
        _prenorm_kernel,
        grid=(N_ALL // NORM_TILE,),
        in_specs=[row, vec] + _mod_specs(l, j_scale) + _mod_specs(l, j_shift),
        out_specs=row,
        out_shape=jax.ShapeDtypeStruct((N_ALL, D_MODEL), BF16),
        compiler_params=_cparams(("parallel",)),
        name="prenorm",
    )(x, g.reshape(1, D_MODEL), mod, mod, mod, mod)


def _resid_kernel(x_ref, y_ref, gpost_ref, gt_s, gt_p, *rest, with_next):
    for sub in range(NORM_TILE // ROW_TILE):
        rows = slice(sub * ROW_TILE, (sub + 1) * ROW_TILE)
        x_new = x_ref[rows, :] + _pick_mod(gt_s, gt_p, sub) * _rms(y_ref[rows, :], gpost_ref[...])
        if with_next:
            gpre_ref, sc_s, sc_p, sh_s, sh_p, xo_ref, h_ref = rest
            xo_ref[rows, :] = x_new
            h = (_rms(x_new, gpre_ref[...]) * (1.0 + _pick_mod(sc_s, sc_p, sub))
                 + _pick_mod(sh_s, sh_p, sub))
            h_ref[rows, :] = h.astype(BF16)
        else:
            (xo_ref,) = rest
            xo_ref[rows, :] = x_new


def _resid(x, y, g_post, mod, l_gate, j_gate, nxt=None):
    row = pl.BlockSpec((NORM_TILE, D_MODEL), lambda i: (i, 0))
    vec = pl.BlockSpec((1, D_MODEL), lambda i: (0, 0))
    in_specs = [row, row, vec] + _mod_specs(l_gate, j_gate)
    args = [x, y, g_post.reshape(1, D_MODEL), mod, mod]
    out_specs = [row]
    out_shape = [jax.ShapeDtypeStruct((N_ALL, D_MODEL), F32)]
    if nxt is not None:
        g_pre, l_n, j_shift, j_scale = nxt
        in_specs += [vec] + _mod_specs(l_n, j_scale) + _mod_specs(l_n, j_shift)
        args += [g_pre.reshape(1, D_MODEL), mod, mod, mod, mod]
        out_specs.append(row)
        out_shape.append(jax.ShapeDtypeStruct((N_ALL, D_MODEL), BF16))
    return pl.pallas_call(
        functools.partial(_resid_kernel, with_next=nxt is not None),
        grid=(N_ALL // NORM_TILE,),
        in_specs=in_specs,
        out_specs=out_specs,
        out_shape=out_shape,
        compiler_params=_cparams(("parallel",)),
        name="resid_norm",
    )(*args)


def _act_none(x):
    return x


def _act_relu2(x):
    return jnp.square(jnp.maximum(x, 0.0))


def _nt(a, b):
    return lax.dot_general(a, b, (((1,), (1,)), ((), ())), preferred_element_type=F32)


def _mm_kernel(a_ref, w_ref, o_ref, *, nk, act, w_transposed, row_sub):
    w = w_ref[...].astype(BF16)
    if nk > 1:
        @pl.when(pl.program_id(2) == 0)
        def _():
            o_ref[...] = jnp.zeros_like(o_ref)

    for r in range(0, a_ref.shape[0], row_sub):
        rows = slice(r, r + row_sub)
        a = a_ref[rows, :]
        part = _nt(a, w) if w_transposed else jnp.dot(a, w, preferred_element_type=F32)
        if nk == 1:
            o_ref[rows, :] = act(part).astype(o_ref.dtype)
        else:
            o_ref[rows, :] += part


def _matmul(a, w, l, *, n_out, col_off=0, tm=MM_TM, tn=256, tk=None, row_sub=MM_ROW_SUB,
            act=_act_none, out_dtype=F32, w_transposed=False, single_buffer_a=False, name):
    k_dim = a.shape[1]
    tk = k_dim if tk is None else tk
    nk = k_dim // tk
    assert nk == 1 or (act is _act_none and out_dtype == F32)
    if single_buffer_a:
        a_spec = pl.BlockSpec((tm, tk), lambda i, j, k: (i, k), pipeline_mode=pl.Buffered(1))
    else:
        a_spec = pl.BlockSpec((tm, tk), lambda i, j, k: (i, k))
    if w_transposed:
        assert col_off % 8 == 0 and tn % 8 == 0 and tk % LANES == 0
        w_spec = pl.BlockSpec((None, pl.Element(tn), pl.Element(tk)),
                              lambda i, j, k: (l, pl.multiple_of(col_off + j * tn, 8),
                                               pl.multiple_of(k * tk, LANES)))
    else:
        w_spec = pl.BlockSpec((None, tk, tn), lambda i, j, k: (l, k, col_off // tn + j))
    return pl.pallas_call(
        functools.partial(_mm_kernel, nk=nk, act=act, w_transposed=w_transposed, row_sub=row_sub),
        grid=(N_ALL // tm, n_out // tn, nk),
        in_specs=[a_spec, w_spec],
        out_specs=pl.BlockSpec((tm, tn), lambda i, j, k: (i, j)),
        out_shape=jax.ShapeDtypeStruct((N_ALL, n_out), out_dtype),
        compiler_params=_cparams(("parallel", "parallel", "arbitrary")),
        name=name,
    )(a, w)


def _merge_kernel(ya_ref, yb_ref, yc_ref, wa_ref, wb_ref, wc_ref, ga_ref, gb_ref, gc_ref, o_ref):
    wa = wa_ref[...].astype(BF16)
    wb = wb_ref[...].astype(BF16)
    wc = wc_ref[...].astype(BF16)
    sub = MM_TM_HALF // 2
    for r in range(0, MM_TM_HALF, sub):
        rows = slice(r, r + sub)
        merged = ga_ref[rows, :] * jnp.dot(ya_ref[rows, :], wa, preferred_element_type=F32)
        merged = merged + gb_ref[rows, :] * jnp.dot(yb_ref[rows, :], wb, preferred_element_type=F32)
        merged = merged + gc_ref[rows, :] * jnp.dot(yc_ref[rows, :], wc, preferred_element_type=F32)
        o_ref[rows, :] = merged.astype(BF16)


def _merge(ya, yb, yc, w_branch, gates, l):
    tn = 256
    tm = MM_TM_HALF
    nb = D_MODEL // tn
    y_spec = pl.BlockSpec((tm, BRANCH_WIDTH), lambda i, j: (i, 0))

    def w_spec(b):
        return pl.BlockSpec((None, None, BRANCH_WIDTH, tn), lambda i, j: (l, b, 0, j))

    def g_spec(b):
        return pl.BlockSpec((tm, tn), lambda i, j: (i, b * nb + j))

    return pl.pallas_call(
        _merge_kernel,
        grid=(N_ALL // tm, nb),
        in_specs=[y_spec, y_spec, y_spec, w_spec(0), w_spec(1), w_spec(2),
                  g_spec(0), g_spec(1), g_spec(2)],
        out_specs=pl.BlockSpec((tm, tn), lambda i, j: (i, j)),
        out_shape=jax.ShapeDtypeStruct((N_ALL, D_MODEL), BF16),
        compiler_params=_cparams(("parallel", "parallel")),
        name="branch_merge",
    )(ya, yb, yc, w_branch, w_branch, w_branch, gates, gates, gates)


def _sgu_uv(zu_ref, zv_ref, lg_ref, lb_ref):
    u = jax.nn.gelu(zu_ref[...], approximate=True)
    v = jax.nn.gelu(zv_ref[...], approximate=True)
    mu = jnp.mean(v, axis=-1, keepdims=True)
    vc = v - mu
    v = vc * lax.rsqrt(jnp.mean(vc * vc, axis=-1, keepdims=True) + EPS) * lg_ref[...] + lb_ref[...]
    return u, v


def _sgu_prompt_chunk(zu_ref, zv_ref, lg_ref, lb_ref, ws_ref, bst_ref, y_ref):
    u, v = _sgu_uv(zu_ref, zv_ref, lg_ref, lb_ref)
    n_idx = lax.broadcasted_iota(jnp.int32, (SGU_CHUNK, SGU_CHUNK), 0)
    m_idx = lax.broadcasted_iota(jnp.int32, (SGU_CHUNK, SGU_CHUNK), 1)
    causal = m_idx <= n_idx
    for g in range(SGU_GROUPS):
        sl = slice(g * LANES, (g + 1) * LANES)
        wm = jnp.where(causal, ws_ref[g], 0.0).astype(BF16)
        s = jnp.dot(wm, v[:, sl].astype(BF16), preferred_element_type=F32) + bst_ref[:, g:g + 1]
        y_ref[:, sl] = (u[:, sl] * s).astype(BF16)


def _sgu_sample_kernel(zu_ref, zv_ref, lg_ref, lb_ref, ws_ref, bst_ref, ya_any, y_ref, vs_ref):
    del ya_any
    u, v = _sgu_uv(zu_ref, zv_ref, lg_ref, lb_ref)
    vs_ref[...] = v
    for g in range(SGU_GROUPS):
        sl = slice(g * LANES, (g + 1) * LANES)
        s = v[:, sl] * ws_ref[g][0:1, 0:1] + bst_ref[0:1, g:g + 1]
        y_ref[:, sl] = (u[:, sl] * s).astype(BF16)


def _sgu_sample(z, y_a, ln_g, ln_b, w_s, b_s_t, l):
    blk = N_PROMPT // ROW_TILE
    vec = pl.BlockSpec((1, BRANCH_WIDTH), lambda i: (0, 0))
    tile = pl.BlockSpec((ROW_TILE, BRANCH_WIDTH), lambda i: (blk, 0))
    return pl.pallas_call(
        _sgu_sample_kernel,
        grid=(1,),
        in_specs=[
            tile,
            pl.BlockSpec((ROW_TILE, BRANCH_WIDTH), lambda i: (blk, 1)),
            vec, vec,
            pl.BlockSpec((None, SGU_GROUPS, SGU_CHUNK, SGU_CHUNK), lambda i: (l, 0, 0, 0)),
            pl.BlockSpec((None, SGU_CHUNK, SGU_GROUPS), lambda i: (l, 0, 0)),
            pl.BlockSpec(memory_space=pl.ANY),
        ],
        out_specs=[tile, pl.BlockSpec((DEC_BATCH, BRANCH_WIDTH), lambda i: (0, 0))],
        out_shape=[
            jax.ShapeDtypeStruct((N_ALL, BRANCH_WIDTH), BF16),
            jax.ShapeDtypeStruct((DEC_BATCH, BRANCH_WIDTH), F32),
        ],
        input_output_aliases={6: 0},
        compiler_params=_cparams(("arbitrary",)),
        name="sgu_sample",
    )(z, z, ln_g[l].reshape(1, BRANCH_WIDTH), ln_b[l].reshape(1, BRANCH_WIDTH), w_s, b_s_t, y_a)


def _hgrn_lower_bound(lb_ref, l):
    raw = lb_ref[...]
    e = jnp.exp(raw - jnp.max(raw, axis=0, keepdims=True))
    p = e / jnp.sum(e, axis=0, keepdims=True)
    acc = p[0:1, :]
    for j in range(1, l + 1):
        acc = acc + p[j:j + 1, :]
    return jnp.maximum(acc - p[0:1, :], 0.0)


def _hgrn_prep(zq, zf, lb_ref, l):
    e = jnp.exp(-jnp.abs(zf))
    ope = 1.0 + e
    sig_neg = jnp.where(zf >= 0.0, e, 1.0) / ope
    log_sig = jnp.minimum(zf, 0.0) - jnp.log(ope)
    if l == 0:
        return _silu(zq), sig_neg, log_sig
    lb = _hgrn_lower_bound(lb_ref, l)
    a = jnp.log(lb)
    c = jnp.log1p(-lb) + log_sig
    log_f = jnp.maximum(a, c) + jnp.log(1.0 + jnp.exp(-jnp.abs(a - c)))
    return _silu(zq), (1.0 - lb) * sig_neg, log_f


def _gla_prep(zq, zk, glr, wup_ref, bup_ref):
    x = jnp.dot(glr[:, :GLA_RANK].astype(BF16), wup_ref[...].astype(BF16),
                preferred_element_type=F32) + bup_ref[...]
    log_a = _log_sigmoid(x) / GLA_TAU
    return zq, zk * (GLA_DK ** -0.5), log_a


def _lane_bcast_col(row):
    return jnp.broadcast_to(row, (LANES, LANES)).T


def _head_out(o, g_row, gate):
    o = o * lax.rsqrt(jnp.mean(o * o, axis=-1, keepdims=True) + EPS) * g_row
    return o * _silu(gate)


class _RecRefs(NamedTuple):
    gnorm: object
    y: object
    s_out: object
    s: object
    s_prev: object
    q: object
    b: object
    o_intra: object


def _rec_scratch(heads, dk, dv):
    return [
        pltpu.VMEM((heads, dk, dv), F32),
        pltpu.VMEM((heads, dk, dv), F32),
        pltpu.VMEM((REC_CHUNK, heads * dk), F32),
        pltpu.VMEM((REC_CHUNK, heads * dk), F32),
        pltpu.VMEM((heads, REC_CHUNK, dv), F32),
    ]


def _finish_head(h, o_intra, q_dec, k_dec, b_last, vh, gate, s_old, refs, dv):
    vs = slice(h * dv, (h + 1) * dv)
    o = o_intra + jnp.dot(q_dec.astype(BF16), s_old.astype(BF16), preferred_element_type=F32)
    refs.y[:, vs] = _head_out(o, refs.gnorm[...], gate[:, vs]).astype(BF16)
    upd = jnp.dot(k_dec.T.astype(BF16), vh, preferred_element_type=F32)
    decay = _lane_bcast_col(jnp.exp(b_last))
    for j in range(dv // LANES):
        ls = slice(j * LANES, (j + 1) * LANES)
        refs.s[h, :, ls] = decay * s_old[:, ls] + upd[:, ls]


def _chunk_start(c, refs):
    @pl.when(c == 0)
    def _():
        refs.s[...] = jnp.zeros_like(refs.s)


def _chunk_main(q, k, v, log_f, gate, refs, *, heads, dk, dv):
    t_idx = lax.broadcasted_iota(jnp.int32, (REC_CHUNK, REC_CHUNK), 0)
    s_idx = lax.broadcasted_iota(jnp.int32, (REC_CHUNK, REC_CHUNK), 1)
    tril = jnp.where(s_idx <= t_idx, 1.0, 0.0).astype(F32)
    b_all = jnp.dot(tril, log_f, precision=lax.Precision.HIGHEST, preferred_element_type=F32)
    v16 = v.astype(BF16)

    mid = HALF // 2 - 1

    def drop(lo, hi):
        return b_all[lo:lo + 1] - b_all[hi:hi + 1]

    worst = jnp.maximum(jnp.maximum(drop(0, mid), drop(mid, HALF - 1)),
                        jnp.maximum(drop(HALF, HALF + mid), drop(HALF + mid, REC_CHUNK - 1)))
    unsafe = jnp.max(worst) > MAX_SPLIT_EXPONENT

    causal_half = (lax.broadcasted_iota(jnp.int32, (HALF, HALF), 1)
                   <= lax.broadcasted_iota(jnp.int32, (HALF, HALF), 0))
    for h in range(heads):
        ks = slice(h * dk, (h + 1) * dk)
        qh, kh, bh = q[:, ks], k[:, ks], b_all[:, ks]
        vh = v16[:, h * dv:(h + 1) * dv]
        b_a, b_b = bh[:HALF], bh[HALF:]
        v_a, v_b = vh[:HALF], vh[HALF:]
        r_a = b_a[mid:mid + 1]
        r_b = b_b[mid:mid + 1]
        r_m = b_a[HALF - 1:HALF]
        b_last = b_b[HALF - 1:HALF]
        q_a = qh[:HALF] * jnp.exp(b_a - r_a)
        q_b = qh[HALF:] * jnp.exp(b_b - r_b)
        k_a = kh[:HALF] * jnp.exp(r_a - b_a)
        k_b = kh[HALF:] * jnp.exp(r_b - b_b)
        att_aa = jnp.where(causal_half, _nt(q_a.astype(BF16), k_a.astype(BF16)), 0.0).astype(BF16)
        att_bb = jnp.where(causal_half, _nt(q_b.astype(BF16), k_b.astype(BF16)), 0.0).astype(BF16)
        att_ba = _nt((q_b * jnp.exp(r_b - r_m)).astype(BF16),
                     (k_a * jnp.exp(r_m - r_a)).astype(BF16)).astype(BF16)
        o_a = jnp.dot(att_aa, v_a, preferred_element_type=F32)
        o_b = (jnp.dot(att_ba, v_a, preferred_element_type=F32)
               + jnp.dot(att_bb, v_b, preferred_element_type=F32))
        q_dec = jnp.concatenate([q_a * jnp.exp(r_a), q_b * jnp.exp(r_b)], axis=0)
        k_dec = jnp.concatenate([k_a * jnp.exp(b_last - r_a), k_b * jnp.exp(b_last - r_b)], axis=0)
        s_old = refs.s[h]
        refs.s_prev[h] = s_old
        _finish_head(h, jnp.concatenate([o_a, o_b], axis=0), q_dec, k_dec, b_last, vh, gate, s_old,
                     refs, dv)
    return unsafe, b_all, v16


def _chunk_redo(unsafe, q, k, v, b_all, v16, gate, refs, *, heads, dk, dv):
    @pl.when(unsafe)
    def _():
        refs.q[...] = q
        refs.b[...] = b_all
        s_pos = lax.broadcasted_iota(jnp.int32, (REC_CHUNK, 1), 0)
        for h in range(heads):
            ks = slice(h * dk, (h + 1) * dk)
            vs = slice(h * dv, (h + 1) * dv)
            qh, kh, bh = q[:, ks], k[:, ks], b_all[:, ks]
            vh = v[:, vs]

            def row(t, carry):
                b_t = refs.b[pl.ds(t, 1), :][:, ks]
                q_t = refs.q[pl.ds(t, 1), :][:, ks]
                p = jnp.exp(jnp.minimum(b_t - bh, 0.0)) * kh * q_t
                w = jnp.where(s_pos <= t, jnp.sum(p, axis=1, keepdims=True), 0.0)
                refs.o_intra[h, pl.ds(t, 1), :] = jnp.sum(w * vh, axis=0, keepdims=True)
                return carry

            lax.fori_loop(0, REC_CHUNK, row, 0)
            b_last = bh[REC_CHUNK - 1:REC_CHUNK]
            _finish_head(h, refs.o_intra[h], qh * jnp.exp(bh), kh * jnp.exp(b_last - bh), b_last,
                         v16[:, vs], gate, refs.s_prev[h], refs, dv)


def _chunk_end(last, refs):
    @pl.when(last)
    def _():
        refs.s_out[...] = refs.s[...]


def _prompt_mixers_kernel(zu_ref, zv_ref, lng_ref, lnb_ref, ws_ref, bst_ref,
                          hq_ref, hf_ref, hi_ref, hg_ref, lb_ref, hgn_ref,
                          gq_ref, gk_ref, gv_ref, gr_ref, glr_ref, wup_ref, bup_ref, glan_ref,
                          ya_ref, yb_ref, shg_ref, yc_ref, sgla_ref, *scratch, l):
    hg = _RecRefs(hgn_ref, yb_ref, shg_ref, *scratch[:5])
    gla = _RecRefs(glan_ref, yc_ref, sgla_ref, *scratch[5:])
    hg_dims = dict(heads=HG_HEADS, dk=HG_DK, dv=HG_DV)
    gla_dims = dict(heads=GLA_HEADS, dk=GLA_DK, dv=GLA_DV)
    c = pl.program_id(1)
    last = c == pl.num_programs(1) - 1
    _chunk_start(c, hg)
    _chunk_start(c, gla)

    _sgu_prompt_chunk(zu_ref, zv_ref, lng_ref, lnb_ref, ws_ref, bst_ref, ya_ref)

    hq, hk, hlf = _hgrn_prep(hq_ref[...], hf_ref[...], lb_ref, l)
    hv, hgate = hi_ref[...], hg_ref[...]
    h_unsafe, h_b, h_v16 = _chunk_main(hq, hk, hv, hlf, hgate, hg, **hg_dims)
    gq, gk, gla_la = _gla_prep(gq_ref[...], gk_ref[...], glr_ref[...], wup_ref, bup_ref)
    gv, ggate = gv_ref[...], gr_ref[...]
    g_unsafe, g_b, g_v16 = _chunk_main(gq, gk, gv, gla_la, ggate, gla, **gla_dims)

    _chunk_redo(h_unsafe, hq, hk, hv, h_b, h_v16, hgate, hg, **hg_dims)
    _chunk_redo(g_unsafe, gq, gk, gv, g_b, g_v16, ggate, gla, **gla_dims)
    _chunk_end(last, hg)
    _chunk_end(last, gla)


def _prompt_mixers(z, z_glr, ln_g, ln_b, w_s, b_s_t, hg_lb, hg_norm_g, w_up, b_up, gla_norm_g, l):
    n_c = SEQ // REC_CHUNK
    w = BRANCH_WIDTH
    kw = GLA_DK_TOTAL
    assert REC_CHUNK == SGU_CHUNK

    def zcol(width, block):
        return pl.BlockSpec((REC_CHUNK, width), lambda b, c: (b * n_c + c, block))

    def const(shape):
        return pl.BlockSpec(shape, lambda b, c: (0,) * len(shape))

    def state_spec(heads, dk, dv):
        return pl.BlockSpec((None, heads, dk, dv), lambda b, c: (b, 0, 0, 0))

    y_spec = pl.BlockSpec((REC_CHUNK, w), lambda b, c: (b * n_c + c, 0))
    in_specs = [
        zcol(w, 0), zcol(w, 1), const((1, w)), const((1, w)),
        pl.BlockSpec((None, SGU_GROUPS, SGU_CHUNK, SGU_CHUNK), lambda b, c: (l, 0, 0, 0)),
        pl.BlockSpec((None, SGU_CHUNK, SGU_GROUPS), lambda b, c: (l, 0, 0)),
        zcol(w, 2), zcol(w, 3), zcol(w, 4), zcol(w, 5), const((DEPTH, w)), const((1, HG_DV)),
        zcol(kw, 6 * w // kw), zcol(kw, 6 * w // kw + 1), zcol(w, 7), zcol(w, 8), zcol(LANES, 0),
        pl.BlockSpec((None, GLA_RANK, kw), lambda b, c: (l, 0, 0)), const((1, kw)), const((1, GLA_DV)),
    ]
    args = [z, z, ln_g[l].reshape(1, w), ln_b[l].reshape(1, w), w_s, b_s_t,
            z, z, z, z, hg_lb, hg_norm_g[l].reshape(1, HG_DV),
            z, z, z, z, z_glr, w_up, b_up[l].reshape(1, kw), gla_norm_g[l].reshape(1, GLA_DV)]
    return pl.pallas_call(
        functools.partial(_prompt_mixers_kernel, l=l),
        grid=(BATCH, n_c),
        in_specs=in_specs,
        out_specs=[y_spec, y_spec, state_spec(HG_HEADS, HG_DK, HG_DV),
                   y_spec, state_spec(GLA_HEADS, GLA_DK, GLA_DV)],
        out_shape=[
            jax.ShapeDtypeStruct((N_ALL, w), BF16),
            jax.ShapeDtypeStruct((N_ALL, w), BF16),
            jax.ShapeDtypeStruct((BATCH, HG_HEADS, HG_DK, HG_DV), F32),
            jax.ShapeDtypeStruct((N_ALL, w), BF16),
            jax.ShapeDtypeStruct((BATCH, GLA_HEADS, GLA_DK, GLA_DV), F32),
        ],
        scratch_shapes=_rec_scratch(HG_HEADS, HG_DK, HG_DV) + _rec_scratch(GLA_HEADS, GLA_DK, GLA_DV),
        compiler_params=_cparams(("parallel", "arbitrary")),
        name="prompt_mixers",
    )(*args)


def _sample_step(q, k, v, log_f, gate, gnorm_ref, s_ref, so_ref, i, *, heads, dk, dv):
    outs = []
    for h in range(heads):
        ks = slice(h * dk, (h + 1) * dk)
        f_col = _lane_bcast_col(jnp.exp(log_f[:, ks]))
        k_col = _lane_bcast_col(k[:, ks])
        q16 = jnp.broadcast_to(q[:, ks], (16, dk)).astype(BF16)
        parts = []
        for j in range(dv // LANES):
            ls = slice(j * LANES, (j + 1) * LANES)
            v_row = v[:, h * dv + j * LANES:h * dv + (j + 1) * LANES]
            s_new = f_col * s_ref[i, h, :, ls] + k_col * v_row
            so_ref[i, h, :, ls] = s_new
            parts.append(jnp.dot(q16, s_new.astype(BF16), preferred_element_type=F32)[0:1])
        o = parts[0] if len(parts) == 1 else jnp.concatenate(parts, axis=1)
        outs.append(_head_out(o, gnorm_ref[...], gate[:, h * dv:(h + 1) * dv]))
    return jnp.concatenate(outs, axis=1)


def _flush_sample_rows(y_ref, yacc_ref):
    @pl.when(pl.program_id(0) == pl.num_programs(0) - 1)
    def _():
        y_ref[...] = yacc_ref[...].astype(BF16)


def _gate_tile(h_ref, w_ref, g_ref):
    w = w_ref[...].astype(BF16)
    for r in range(0, MM_TM, MM_ROW_SUB):
        rows = slice(r, r + MM_ROW_SUB)
        g_ref[rows, :] = (0.5 + 0.5 * jnp.tanh(0.5 * _nt(h_ref[rows, :], w))).astype(BF16)


def _unit_index(step):
    return jnp.minimum(step, DEC_BATCH // SAMPLE_NB - 1)


def _hgrn_sample_kernel(h_ref, w_ref, zq_ref, zf_ref, zi_ref, zg_ref, lb_ref, gnorm_ref, s_ref, *rest, l):
    g_ref, y_ref, so_ref, yacc_ref = rest[-4:]
    _gate_tile(h_ref, w_ref, g_ref)
    base = _unit_index(pl.program_id(0)) * SAMPLE_NB
    q, k, log_f = _hgrn_prep(zq_ref[...], zf_ref[...], lb_ref, l)
    v, gate = zi_ref[...], zg_ref[...]
    for i in range(SAMPLE_NB):
        row = slice(i, i + 1)
        yacc_ref[pl.ds(base + i, 1), :] = _sample_step(
            q[row], k[row], v[row], log_f[row], gate[row], gnorm_ref, s_ref, so_ref, i,
            heads=HG_HEADS, dk=HG_DK, dv=HG_DV)
    _flush_sample_rows(y_ref, yacc_ref)


def _gla_sample_kernel(h_ref, w_ref, zq_ref, zk_ref, zv_ref, zr_ref, glr_ref, wup_ref, bup_ref,
                       gnorm_ref, s_ref, *rest):
    g_ref, y_ref, so_ref, yacc_ref = rest[-4:]
    _gate_tile(h_ref, w_ref, g_ref)
    base = _unit_index(pl.program_id(0)) * SAMPLE_NB
    glr = jnp.concatenate([glr_ref[...], jnp.zeros((16 - SAMPLE_NB, LANES), F32)], axis=0)
    q, k, log_a = _gla_prep(zq_ref[...], zk_ref[...], glr, wup_ref, bup_ref)
    v, gate = zv_ref[...], zr_ref[...]
    for i in range(SAMPLE_NB):
        row = slice(i, i + 1)
        yacc_ref[pl.ds(base + i, 1), :] = _sample_step(
            q[row], k[row], v[row], log_a[row], gate[row], gnorm_ref, s_ref, so_ref, i,
            heads=GLA_HEADS, dk=GLA_DK, dv=GLA_DV)
    _flush_sample_rows(y_ref, yacc_ref)


def _zrow(width, block):
    return pl.BlockSpec((SAMPLE_NB, width), lambda j: (N_PROMPT // SAMPLE_NB + _unit_index(j), block))


def _gate_sample_call(kernel, in_specs, args, h, w_in_t, state, y_all, new_state, gates, row_tile, l,
                      heads, dk, dv, name):
    tn = GATE_TN
    n_steps = N_BRANCH * D_MODEL // tn
    assert n_steps >= DEC_BATCH // SAMPLE_NB
    h_spec = pl.BlockSpec((MM_TM, D_MODEL), lambda j: (row_tile, 0), pipeline_mode=pl.Buffered(1))
    w_spec = pl.BlockSpec((None, pl.Element(tn), pl.Element(D_MODEL)),
                          lambda j: (l, pl.multiple_of(OFF_GATE + j * tn, 8), 0))
    st = pl.BlockSpec((None, SAMPLE_NB, heads, dk, dv), lambda j: (l, _unit_index(j), 0, 0, 0))
    any_spec = pl.BlockSpec(memory_space=pl.ANY)
    in_specs = [h_spec, w_spec] + in_specs + [st, any_spec]
    args = [h, w_in_t] + list(args) + [state, y_all]
    aliases = {len(args) - 1: 1}
    for buf, out_idx in ((new_state, 2), (gates, 0)):
        if buf is not None:
            aliases[len(args)] = out_idx
            in_specs.append(any_spec)
            args.append(buf)
    return pl.pallas_call(
        kernel,
        grid=(n_steps,),
        in_specs=in_specs,
        out_specs=[
            pl.BlockSpec((MM_TM, tn), lambda j: (row_tile, j)),
            pl.BlockSpec((DEC_BATCH, BRANCH_WIDTH), lambda j: (N_PROMPT // DEC_BATCH, 0)),
            st,
        ],
        out_shape=[
            jax.ShapeDtypeStruct((N_ALL, N_BRANCH * D_MODEL), BF16),
            jax.ShapeDtypeStruct((N_ALL, BRANCH_WIDTH), BF16),
            jax.ShapeDtypeStruct((DEPTH, DEC_BATCH, heads, dk, dv), F32),
        ],
        scratch_shapes=[pltpu.VMEM((DEC_BATCH, BRANCH_WIDTH), F32)],
        input_output_aliases=aliases,
        compiler_params=_cparams(("arbitrary",)),
        name=name,
    )(*args)


def _hgrn_sample(h, w_in_t, z, hg_lb, g_norm, state, y_all, new_state, gates, row_tile, l):
    w = BRANCH_WIDTH
    in_specs = [_zrow(w, 2), _zrow(w, 3), _zrow(w, 4), _zrow(w, 5),
                pl.BlockSpec((DEPTH, w), lambda j: (0, 0)),
                pl.BlockSpec((1, HG_DV), lambda j: (0, 0))]
    args = [z, z, z, z, hg_lb, g_norm[l].reshape(1, HG_DV)]
    return _gate_sample_call(functools.partial(_hgrn_sample_kernel, l=l), in_specs, args, h, w_in_t,
                             state, y_all, new_state, gates, row_tile, l,
                             HG_HEADS, HG_DK, HG_DV, "gate_hgrn_sample")


def _gla_sample(h, w_in_t, z, z_glr, w_up, b_up, g_norm, state, y_all, new_state, gates, row_tile, l):
    w = BRANCH_WIDTH
    kw = GLA_DK_TOTAL
    in_specs = [_zrow(kw, 6 * w // kw), _zrow(kw, 6 * w // kw + 1), _zrow(w, 7), _zrow(w, 8),
                _zrow(LANES, 0),
                pl.BlockSpec((None, GLA_RANK, kw), lambda j: (l, 0, 0)),
                pl.BlockSpec((1, kw), lambda j: (0, 0)),
                pl.BlockSpec((1, GLA_DV), lambda j: (0, 0))]
    args = [z, z, z, z, z_glr, w_up, b_up[l].reshape(1, kw), g_norm[l].reshape(1, GLA_DV)]
    return _gate_sample_call(_gla_sample_kernel, in_specs, args, h, w_in_t, state, y_all, new_state,
                             gates, row_tile, l, GLA_HEADS, GLA_DK, GLA_DV, "gate_gla_sample")


def kernel(x_prompt, x_sample, state_hgrn, state_gla, c_prompt, c_sample, w_ada, b_ada, g_pre_mix, g_post_mix, g_pre_mlp, g_post_mlp, w_in, sgu_ln_g, sgu_ln_b, sgu_w_s, sgu_b_s, hg_lb, hg_norm_g, gla_w_up, gla_b_up, gla_norm_g, w_branch, w_out, w_mlp_up, w_mlp_down):
    x = jnp.concatenate([x_prompt.reshape(N_PROMPT, D_MODEL), x_sample.reshape(DEC_BATCH, D_MODEL)], axis=0)
    c_all = jnp.concatenate([c_sample, c_prompt, jnp.zeros((MOD_ROWS - DEC_BATCH - BATCH, D_MODEL), F32)], axis=0)
    mod = _modulation(c_all, w_ada, b_ada)
    w_in_t = jnp.swapaxes(w_in, 1, 2)
    b_s_t = jnp.swapaxes(sgu_b_s, 1, 2)

    hg_p, gla_p, v_rows = [], [], []
    hg_s = gla_s = None
    h = _prenorm(x, g_pre_mix[0], mod, 0, 0, 1)
    for l in range(DEPTH):
        z = _matmul(h, w_in_t, l, n_out=Z_MAIN, tn=WIDE_TN, w_transposed=True, single_buffer_a=True,
                    name="in_proj")
        z_glr = _matmul(h, w_in_t, l, n_out=LANES, col_off=OFF_GLR, tn=LANES, w_transposed=True,
                        name="in_proj_lowrank")

        y_a, y_b, s_hg_p, y_c, s_gla_p = _prompt_mixers(
            z, z_glr, sgu_ln_g, sgu_ln_b, sgu_w_s, b_s_t, hg_lb, hg_norm_g, gla_w_up, gla_b_up,
            gla_norm_g, l)
        y_a, v_s = _sgu_sample(z, y_a, sgu_ln_g, sgu_ln_b, sgu_w_s, b_s_t, l)
        gates, y_b, hg_s = _hgrn_sample(h, w_in_t, z, hg_lb, hg_norm_g, state_hgrn, y_b, hg_s,
                                        None, 0, l)
        gates, y_c, gla_s = _gla_sample(h, w_in_t, z, z_glr, gla_w_up, gla_b_up, gla_norm_g,
                                        state_gla, y_c, gla_s, gates, 1, l)

        merged = _merge(y_a, y_b, y_c, w_branch, gates, l)
        out = _matmul(merged, w_out, l, n_out=D_MODEL, name="out_proj")
        x, h2 = _resid(x, out, g_post_mix[l], mod, l, 2, nxt=(g_pre_mlp[l], l, 3, 4))

        up = _matmul(h2, w_mlp_up, l, n_out=D_FF, tn=WIDE_TN, act=_act_relu2, out_dtype=BF16,
                     single_buffer_a=True, name="mlp_up")
        y2 = _matmul(up, w_mlp_down, l, n_out=D_MODEL, tm=MM_TM_HALF, tn=1024, tk=1024,
                     row_sub=MM_TM_HALF // 2, name="mlp_down")
        if l + 1 < DEPTH:
            x, h = _resid(x, y2, g_post_mlp[l], mod, l, 5, nxt=(g_pre_mix[l + 1], l + 1, 0, 1))
        else:
            (x,) = _resid(x, y2, g_post_mlp[l], mod, l, 5)

        hg_p.append(s_hg_p)
        gla_p.append(s_gla_p)
        v_rows.append(v_s.reshape(DEC_BATCH, 1, BRANCH_WIDTH))

    y_prompt = x[:N_PROMPT].reshape(BATCH, SEQ, D_MODEL)
    y_sample = x[N_PROMPT:].reshape(DEC_BATCH, 1, D_MODEL)
    return (y_prompt, y_sample, jnp.stack(hg_p), jnp.stack(gla_p), hg_s, gla_s, jnp.stack(v_rows))
```

```python
import functools
from typing import NamedTuple

import jax
import jax.numpy as jnp
from jax import lax
from jax.experimental import pallas as pl
from jax.experimental.pallas import tpu as pltpu

F32 = jnp.float32
BF16 = jnp.bfloat16

D_MODEL = 2048
BATCH = 4
SEQ = 2048
DEPTH = 4
DEC_BATCH = 128
BRANCH_WIDTH = D_MODEL // 2
N_BRANCH = 3
SGU_GROUPS = 8
SGU_CHUNK = 128
HG_HEADS = 8
HG_DK = BRANCH_WIDTH // HG_HEADS
HG_DV = BRANCH_WIDTH // HG_HEADS
GLA_HEADS = 4
GLA_DK_TOTAL = BRANCH_WIDTH // 2
GLA_DK = GLA_DK_TOTAL // GLA_HEADS
GLA_DV = BRANCH_WIDTH // GLA_HEADS
GLA_RANK = 16
GLA_TAU = 16.0
D_FF = 4 * D_MODEL
EPS = 1e-6

N_PROMPT = BATCH * SEQ
N_ALL = N_PROMPT + DEC_BATCH
MOD_ROWS = DEC_BATCH + 8

OFF_GLR = 2 * BRANCH_WIDTH + 4 * BRANCH_WIDTH + 2 * GLA_DK_TOTAL + 2 * BRANCH_WIDTH
OFF_GATE = OFF_GLR + GLA_RANK
Z_MAIN = OFF_GLR

LANES = 128
ROW_TILE = 128
NORM_TILE = 640
LAST_NORM_TILE = N_ALL // NORM_TILE - 1
SAMPLE_SUB = (N_PROMPT - LAST_NORM_TILE * NORM_TILE) // ROW_TILE
MAX_SPLIT_EXPONENT = 60.0
MM_TM = 4160
MM_ROW_SUB = 832
MM_TM_HALF = 2080
REC_CHUNK = 128
HALF = REC_CHUNK // 2
SAMPLE_NB = 8
GATE_TN = 256
WIDE_TN = 512
VMEM_LIMIT = 56 * 1024 * 1024


def _cparams(sem):
    return pltpu.CompilerParams(dimension_semantics=sem, vmem_limit_bytes=VMEM_LIMIT)


def _sigmoid(x):
    return 0.5 + 0.5 * jnp.tanh(0.5 * x)


def _silu(x):
    return x * _sigmoid(x)


def _log_sigmoid(x):
    return jnp.minimum(x, 0.0) - jnp.log(1.0 + jnp.exp(-jnp.abs(x)))


def _rms(x, g):
    return x * lax.rsqrt(jnp.mean(x * x, axis=-1, keepdims=True) + EPS) * g


def _mod_kernel(c_ref, w_ref, b_ref, o_ref):
    a = _silu(c_ref[...]).astype(BF16)
    o_ref[...] = jnp.dot(a, w_ref[...].astype(BF16), preferred_element_type=F32) + b_ref[...]


def _modulation(c_all, w_ada, b_ada):
    tn = 1024
    n = 6 * D_MODEL
    return pl.pallas_call(
        _mod_kernel,
        grid=(DEPTH, n // tn),
        in_specs=[
            pl.BlockSpec((MOD_ROWS, D_MODEL), lambda l, j: (0, 0)),
            pl.BlockSpec((None, D_MODEL, tn), lambda l, j: (l, 0, j)),
            pl.BlockSpec((None, 1, tn), lambda l, j: (l, 0, j)),
        ],
        out_specs=pl.BlockSpec((None, MOD_ROWS, tn), lambda l, j: (l, 0, j)),
        out_shape=jax.ShapeDtypeStruct((DEPTH, MOD_ROWS, n), F32),
        compiler_params=_cparams(("parallel", "parallel")),
        name="adaln_mod",
    )(c_all, w_ada, b_ada.reshape(DEPTH, 1, n))


def _mod_specs(l, j):
    return [
        pl.BlockSpec((None, DEC_BATCH, D_MODEL), lambda i: (l, 0, j)),
        pl.BlockSpec((None, 8, D_MODEL), lambda i: (l, DEC_BATCH // 8, j)),
    ]


def _pick_mod(s_ref, p_ref, sub):
    g = pl.program_id(0) * (NORM_TILE // ROW_TILE) + sub
    b = jnp.minimum(g // (SEQ // ROW_TILE), BATCH - 1)
    is_sample = g >= N_PROMPT // ROW_TILE
    return jnp.where(is_sample, s_ref[...], p_ref[pl.ds(b, 1), :])


def _x_rows(x_ref, xs_ref, sub):
    x = x_ref[sub * ROW_TILE:(sub + 1) * ROW_TILE, :]
    if xs_ref is None or sub != SAMPLE_SUB:
        return x
    return jnp.where(pl.program_id(0) == LAST_NORM_TILE, xs_ref[...], x)


def _prenorm_kernel(x_ref, xs_ref, g_ref, sc_s, sc_p, sh_s, sh_p, h_ref):
    for sub in range(NORM_TILE // ROW_TILE):
        rows = slice(sub * ROW_TILE, (sub + 1) * ROW_TILE)
        h = (_rms(_x_rows(x_ref, xs_ref, sub), g_ref[...]) * (1.0 + _pick_mod(sc_s, sc_p, sub))
             + _pick_mod(sh_s, sh_p, sub))
        h_ref[rows, :] = h.astype(BF16)


def _prenorm(x_p, x_s, g, mod, l, j_shift, j_scale):
    row = pl.BlockSpec((NORM_TILE, D_MODEL), lambda i: (i, 0))
    vec = pl.BlockSpec((1, D_MODEL), lambda i: (0, 0))
    smp = pl.BlockSpec((DEC_BATCH, D_MODEL), lambda i: (0, 0))
    return pl.pallas_call(
        _prenorm_kernel,
        grid=(N_ALL // NORM_TILE,),
        in_specs=[row, smp, vec] + _mod_specs(l, j_scale) + _mod_specs(l, j_shift),
        out_specs=row,
        out_shape=jax.ShapeDtypeStruct((N_ALL, D_MODEL), BF16),
        compiler_params=_cparams(("parallel",)),
        name="prenorm",
    )(x_p, x_s, g.reshape(1, D_MODEL), mod, mod, mod, mod)


def _resid_kernel(x_ref, *rest, with_next, split_in, split_out):
    xs_ref = None
    if split_in:
        xs_ref, *rest = rest
    y_ref, gpost_ref, gt_s, gt_p, *rest = rest
    for sub in range(NORM_TILE // ROW_TILE):
        rows = slice(sub * ROW_TILE, (sub + 1) * ROW_TILE)
        x_new = (_x_rows(x_ref, xs_ref, sub)
                 + _pick_mod(gt_s, gt_p, sub) * _rms(y_ref[rows, :], gpost_ref[...]))
        if with_next:
            gpre_ref, sc_s, sc_p, sh_s, sh_p, xo_ref, h_ref = rest
            xo_ref[rows, :] = x_new
            h = (_rms(x_new, gpre_ref[...]) * (1.0 + _pick_mod(sc_s, sc_p, sub))
                 + _pick_mod(sh_s, sh_p, sub))
            h_ref[rows, :] = h.astype(BF16)
        elif split_out:
            xo_ref, xso_ref = rest
            xo_ref[rows, :] = x_new
            if sub == SAMPLE_SUB:
                @pl.when(pl.program_id(0) == LAST_NORM_TILE)
                def _():
                    xso_ref[...] = x_new
        else:
            (xo_ref,) = rest
            xo_ref[rows, :] = x_new


def _resid(x, y, g_post, mod, l_gate, j_gate, nxt=None, x_sample=None, split_out=False):
    row = pl.BlockSpec((NORM_TILE, D_MODEL), lambda i: (i, 0))
    vec = pl.BlockSpec((1, D_MODEL), lambda i: (0, 0))
    smp = pl.BlockSpec((DEC_BATCH, D_MODEL), lambda i: (0, 0))
    in_specs = [row] + ([smp] if x_sample is not None else []) + [row, vec] + _mod_specs(l_gate, j_gate)
    args = [x] + ([x_sample] if x_sample is not None else []) + [y, g_post.reshape(1, D_MODEL), mod, mod]
    out_specs = [row]
    out_shape = [jax.ShapeDtypeStruct((N_ALL, D_MODEL), F32)]
    if nxt is not None:
        assert not split_out
        g_pre, l_n, j_shift, j_scale = nxt
        in_specs += [vec] + _mod_specs(l_n, j_scale) + _mod_specs(l_n, j_shift)
        args += [g_pre.reshape(1, D_MODEL), mod, mod, mod, mod]
        out_specs.append(row)
        out_shape.append(jax.ShapeDtypeStruct((N_ALL, D_MODEL), BF16))
    elif split_out:
        out_specs.append(smp)
        out_shape = [jax.ShapeDtypeStruct((N_PROMPT, D_MODEL), F32),
                     jax.ShapeDtypeStruct((DEC_BATCH, D_MODEL), F32)]
    return pl.pallas_call(
        functools.partial(_resid_kernel, with_next=nxt is not None, split_in=x_sample is not None,
                          split_out=split_out),
        grid=(N_ALL // NORM_TILE,),
        in_specs=in_specs,
        out_specs=out_specs,
        out_shape=out_shape,
        compiler_params=_cparams(("arbitrary",) if split_out else ("parallel",)),
        name="resid_norm",
    )(*args)


def _act_none(x):
    return x


def _act_relu2(x):
    return jnp.square(jnp.maximum(x, 0.0))


def _nt(a, b):
    return lax.dot_general(a, b, (((1,), (1,)), ((), ())), preferred_element_type=F32)


def _mm_kernel(a_ref, w_ref, o_ref, *, nk, act, w_transposed, row_sub):
    w = w_ref[...].astype(BF16)
    if nk > 1:
        @pl.when(pl.program_id(2) == 0)
        def _():
            o_ref[...] = jnp.zeros_like(o_ref)

    for r in range(0, a_ref.shape[0], row_sub):
        rows = slice(r, r + row_sub)
        a = a_ref[rows, :]
        part = _nt(a, w) if w_transposed else jnp.dot(a, w, preferred_element_type=F32)
        if nk == 1:
            o_ref[rows, :] = act(part).astype(o_ref.dtype)
        else:
            o_ref[rows, :] += part


def _matmul(a, w, l, *, n_out, col_off=0, tm=MM_TM, tn=256, tk=None, row_sub=MM_ROW_SUB,
            act=_act_none, out_dtype=F32, w_transposed=False, single_buffer_a=False, name):
    k_dim = a.shape[1]
    tk = k_dim if tk is None else tk
    nk = k_dim // tk
    assert nk == 1 or (act is _act_none and out_dtype == F32)
    if single_buffer_a:
        a_spec = pl.BlockSpec((tm, tk), lambda i, j, k: (i, k), pipeline_mode=pl.Buffered(1))
    else:
        a_spec = pl.BlockSpec((tm, tk), lambda i, j, k: (i, k))
    if w_transposed:
        assert col_off % 8 == 0 and tn % 8 == 0 and tk % LANES == 0
        w_spec = pl.BlockSpec((None, pl.Element(tn), pl.Element(tk)),
                              lambda i, j, k: (l, pl.multiple_of(col_off + j * tn, 8),
                                               pl.multiple_of(k * tk, LANES)))
    else:
        w_spec = pl.BlockSpec((None, tk, tn), lambda i, j, k: (l, k, col_off // tn + j))
    return pl.pallas_call(
        functools.partial(_mm_kernel, nk=nk, act=act, w_transposed=w_transposed, row_sub=row_sub),
        grid=(N_ALL // tm, n_out // tn, nk),
        in_specs=[a_spec, w_spec],
        out_specs=pl.BlockSpec((tm, tn), lambda i, j, k: (i, j)),
        out_shape=jax.ShapeDtypeStruct((N_ALL, n_out), out_dtype),
        compiler_params=_cparams(("parallel", "parallel", "arbitrary")),
        name=name,
    )(a, w)


def _merge_kernel(ya_ref, yb_ref, yc_ref, wa_ref, wb_ref, wc_ref, ga_ref, gb_ref, gc_ref, o_ref):
    wa = wa_ref[...].astype(BF16)
    wb = wb_ref[...].astype(BF16)
    wc = wc_ref[...].astype(BF16)
    sub = MM_TM_HALF // 2
    for r in range(0, MM_TM_HALF, sub):
        rows = slice(r, r + sub)
        merged = ga_ref[rows, :] * jnp.dot(ya_ref[rows, :], wa, preferred_element_type=F32)
        merged = merged + gb_ref[rows, :] * jnp.dot(yb_ref[rows, :], wb, preferred_element_type=F32)
        merged = merged + gc_ref[rows, :] * jnp.dot(yc_ref[rows, :], wc, preferred_element_type=F32)
        o_ref[rows, :] = merged.astype(BF16)


def _merge(ya, yb, yc, w_branch, gates, l):
    tn = 256
    tm = MM_TM_HALF
    nb = D_MODEL // tn
    y_spec = pl.BlockSpec((tm, BRANCH_WIDTH), lambda i, j: (i, 0))

    def w_spec(b):
        return pl.BlockSpec((None, None, BRANCH_WIDTH, tn), lambda i, j: (l, b, 0, j))

    def g_spec(b):
        return pl.BlockSpec((tm, tn), lambda i, j: (i, b * nb + j))

    return pl.pallas_call(
        _merge_kernel,
        grid=(N_ALL // tm, nb),
        in_specs=[y_spec, y_spec, y_spec, w_spec(0), w_spec(1), w_spec(2),
                  g_spec(0), g_spec(1), g_spec(2)],
        out_specs=pl.BlockSpec((tm, tn), lambda i, j: (i, j)),
        out_shape=jax.ShapeDtypeStruct((N_ALL, D_MODEL), BF16),
        compiler_params=_cparams(("parallel", "parallel")),
        name="branch_merge",
    )(ya, yb, yc, w_branch, w_branch, w_branch, gates, gates, gates)


def _sgu_uv(zu_ref, zv_ref, lg_ref, lb_ref):
    u = jax.nn.gelu(zu_ref[...], approximate=True)
    v = jax.nn.gelu(zv_ref[...], approximate=True)
    mu = jnp.mean(v, axis=-1, keepdims=True)
    vc = v - mu
    v = vc * lax.rsqrt(jnp.mean(vc * vc, axis=-1, keepdims=True) + EPS) * lg_ref[...] + lb_ref[...]
    return u, v


def _sgu_prompt_chunk(zu_ref, zv_ref, lg_ref, lb_ref, ws_ref, bst_ref, y_ref):
    u, v = _sgu_uv(zu_ref, zv_ref, lg_ref, lb_ref)
    n_idx = lax.broadcasted_iota(jnp.int32, (SGU_CHUNK, SGU_CHUNK), 0)
    m_idx = lax.broadcasted_iota(jnp.int32, (SGU_CHUNK, SGU_CHUNK), 1)
    causal = m_idx <= n_idx
    for g in range(SGU_GROUPS):
        sl = slice(g * LANES, (g + 1) * LANES)
        wm = jnp.where(causal, ws_ref[g], 0.0).astype(BF16)
        s = jnp.dot(wm, v[:, sl].astype(BF16), preferred_element_type=F32) + bst_ref[:, g:g + 1]
        y_ref[:, sl] = (u[:, sl] * s).astype(BF16)


def _sgu_sample_kernel(zu_ref, zv_ref, lg_ref, lb_ref, ws_ref, bst_ref, ya_any, y_ref, vs_ref):
    del ya_any
    u, v = _sgu_uv(zu_ref, zv_ref, lg_ref, lb_ref)
    vs_ref[...] = v
    for g in range(SGU_GROUPS):
        sl = slice(g * LANES, (g + 1) * LANES)
        s = v[:, sl] * ws_ref[g][0:1, 0:1] + bst_ref[0:1, g:g + 1]
        y_ref[:, sl] = (u[:, sl] * s).astype(BF16)


def _sgu_sample(z, y_a, ln_g, ln_b, w_s, b_s_t, l):
    blk = N_PROMPT // ROW_TILE
    vec = pl.BlockSpec((1, BRANCH_WIDTH), lambda i: (0, 0))
    tile = pl.BlockSpec((ROW_TILE, BRANCH_WIDTH), lambda i: (blk, 0))
    return pl.pallas_call(
        _sgu_sample_kernel,
        grid=(1,),
        in_specs=[
            tile,
            pl.BlockSpec((ROW_TILE, BRANCH_WIDTH), lambda i: (blk, 1)),
            vec, vec,
            pl.BlockSpec((None, SGU_GROUPS, SGU_CHUNK, SGU_CHUNK), lambda i: (l, 0, 0, 0)),
            pl.BlockSpec((None, SGU_CHUNK, SGU_GROUPS), lambda i: (l, 0, 0)),
            pl.BlockSpec(memory_space=pl.ANY),
        ],
        out_specs=[tile, pl.BlockSpec((DEC_BATCH, BRANCH_WIDTH), lambda i: (0, 0))],
        out_shape=[
            jax.ShapeDtypeStruct((N_ALL, BRANCH_WIDTH), BF16),
            jax.ShapeDtypeStruct((DEC_BATCH, BRANCH_WIDTH), F32),
        ],
        input_output_aliases={6: 0},
        compiler_params=_cparams(("arbitrary",)),
        name="sgu_sample",
    )(z, z, ln_g[l].reshape(1, BRANCH_WIDTH), ln_b[l].reshape(1, BRANCH_WIDTH), w_s, b_s_t, y_a)


def _hgrn_lower_bound(lb_ref, l):
    raw = lb_ref[...]
    e = jnp.exp(raw - jnp.max(raw, axis=0, keepdims=True))
    p = e / jnp.sum(e, axis=0, keepdims=True)
    acc = p[0:1, :]
    for j in range(1, l + 1):
        acc = acc + p[j:j + 1, :]
    return jnp.maximum(acc - p[0:1, :], 0.0)


def _hgrn_prep(zq, zf, lb_ref, l):
    e = jnp.exp(-jnp.abs(zf))
    ope = 1.0 + e
    sig_neg = jnp.where(zf >= 0.0, e, 1.0) / ope
    log_sig = jnp.minimum(zf, 0.0) - jnp.log(ope)
    if l == 0:
        return _silu(zq), sig_neg, log_sig
    lb = _hgrn_lower_bound(lb_ref, l)
    a = jnp.log(lb)
    c = jnp.log1p(-lb) + log_sig
    log_f = jnp.maximum(a, c) + jnp.log(1.0 + jnp.exp(-jnp.abs(a - c)))
    return _silu(zq), (1.0 - lb) * sig_neg, log_f


def _gla_prep(zq, zk, glr, wup_ref, bup_ref):
    x = jnp.dot(glr[:, :GLA_RANK].astype(BF16), wup_ref[...].astype(BF16),
                preferred_element_type=F32) + bup_ref[...]
    log_a = _log_sigmoid(x) / GLA_TAU
    return zq, zk * (GLA_DK ** -0.5), log_a


def _lane_bcast_col(row):
    return jnp.broadcast_to(row, (LANES, LANES)).T


def _head_out(o, g_row, gate):
    o = o * lax.rsqrt(jnp.mean(o * o, axis=-1, keepdims=True) + EPS) * g_row
    return o * _silu(gate)


class _RecRefs(NamedTuple):
    gnorm: object
    y: object
    s_out: object
    s: object
    s_prev: object
    q: object
    b: object
    o_intra: object


def _rec_scratch(heads, dk, dv):
    return [
        pltpu.VMEM((heads, dk, dv), F32),
        pltpu.VMEM((heads, dk, dv), F32),
        pltpu.VMEM((REC_CHUNK, heads * dk), F32),
        pltpu.VMEM((REC_CHUNK, heads * dk), F32),
        pltpu.VMEM((heads, REC_CHUNK, dv), F32),
    ]


def _finish_head(h, o_intra, q_dec, k_dec, b_last, vh, gate, s_old, refs, dv):
    vs = slice(h * dv, (h + 1) * dv)
    o = o_intra + jnp.dot(q_dec.astype(BF16), s_old.astype(BF16), preferred_element_type=F32)
    refs.y[:, vs] = _head_out(o, refs.gnorm[...], gate[:, vs]).astype(BF16)
    upd = jnp.dot(k_dec.T.astype(BF16), vh, preferred_element_type=F32)
    decay = _lane_bcast_col(jnp.exp(b_last))
    for j in range(dv // LANES):
        ls = slice(j * LANES, (j + 1) * LANES)
        refs.s[h, :, ls] = decay * s_old[:, ls] + upd[:, ls]


def _chunk_start(c, refs):
    @pl.when(c == 0)
    def _():
        refs.s[...] = jnp.zeros_like(refs.s)


def _chunk_main(q, k, v, log_f, gate, refs, *, heads, dk, dv):
    t_idx = lax.broadcasted_iota(jnp.int32, (REC_CHUNK, REC_CHUNK), 0)
    s_idx = lax.broadcasted_iota(jnp.int32, (REC_CHUNK, REC_CHUNK), 1)
    tril = jnp.where(s_idx <= t_idx, 1.0, 0.0).astype(F32)
    b_all = jnp.dot(tril, log_f, precision=lax.Precision.HIGHEST, preferred_element_type=F32)
    v16 = v.astype(BF16)

    mid = HALF // 2 - 1

    def drop(lo, hi):
        return b_all[lo:lo + 1] - b_all[hi:hi + 1]

    worst = jnp.maximum(jnp.maximum(drop(0, mid), drop(mid, HALF - 1)),
                        jnp.maximum(drop(HALF, HALF + mid), drop(HALF + mid, REC_CHUNK - 1)))
    unsafe = jnp.max(worst) > MAX_SPLIT_EXPONENT

    causal_half = (lax.broadcasted_iota(jnp.int32, (HALF, HALF), 1)
                   <= lax.broadcasted_iota(jnp.int32, (HALF, HALF), 0))
    for h in range(heads):
        ks = slice(h * dk, (h + 1) * dk)
        qh, kh, bh = q[:, ks], k[:, ks], b_all[:, ks]
        vh = v16[:, h * dv:(h + 1) * dv]
        b_a, b_b = bh[:HALF], bh[HALF:]
        v_a, v_b = vh[:HALF], vh[HALF:]
        r_a = b_a[mid:mid + 1]
        r_b = b_b[mid:mid + 1]
        r_m = b_a[HALF - 1:HALF]
        b_last = b_b[HALF - 1:HALF]
        q_a = qh[:HALF] * jnp.exp(b_a - r_a)
        q_b = qh[HALF:] * jnp.exp(b_b - r_b)
        k_a = kh[:HALF] * jnp.exp(r_a - b_a)
        k_b = kh[HALF:] * jnp.exp(r_b - b_b)
        att_aa = jnp.where(causal_half, _nt(q_a.astype(BF16), k_a.astype(BF16)), 0.0).astype(BF16)
        att_bb = jnp.where(causal_half, _nt(q_b.astype(BF16), k_b.astype(BF16)), 0.0).astype(BF16)
        att_ba = _nt((q_b * jnp.exp(r_b - r_m)).astype(BF16),
                     (k_a * jnp.exp(r_m - r_a)).astype(BF16)).astype(BF16)
        o_a = jnp.dot(att_aa, v_a, preferred_element_type=F32)
        o_b = (jnp.dot(att_ba, v_a, preferred_element_type=F32)
               + jnp.dot(att_bb, v_b, preferred_element_type=F32))
        q_dec = jnp.concatenate([q_a * jnp.exp(r_a), q_b * jnp.exp(r_b)], axis=0)
        k_dec = jnp.concatenate([k_a * jnp.exp(b_last - r_a), k_b * jnp.exp(b_last - r_b)], axis=0)
        s_old = refs.s[h]
        refs.s_prev[h] = s_old
        _finish_head(h, jnp.concatenate([o_a, o_b], axis=0), q_dec, k_dec, b_last, vh, gate, s_old,
                     refs, dv)
    return unsafe, b_all, v16


def _chunk_redo(unsafe, q, k, v, b_all, v16, gate, refs, *, heads, dk, dv):
    @pl.when(unsafe)
    def _():
        refs.q[...] = q
        refs.b[...] = b_all
        s_pos = lax.broadcasted_iota(jnp.int32, (REC_CHUNK, 1), 0)
        for h in range(heads):
            ks = slice(h * dk, (h + 1) * dk)
            vs = slice(h * dv, (h + 1) * dv)
            qh, kh, bh = q[:, ks], k[:, ks], b_all[:, ks]
            vh = v[:, vs]

            def row(t, carry):
                b_t = refs.b[pl.ds(t, 1), :][:, ks]
                q_t = refs.q[pl.ds(t, 1), :][:, ks]
                p = jnp.exp(jnp.minimum(b_t - bh, 0.0)) * kh * q_t
                w = jnp.where(s_pos <= t, jnp.sum(p, axis=1, keepdims=True), 0.0)
                refs.o_intra[h, pl.ds(t, 1), :] = jnp.sum(w * vh, axis=0, keepdims=True)
                return carry

            lax.fori_loop(0, REC_CHUNK, row, 0)
            b_last = bh[REC_CHUNK - 1:REC_CHUNK]
            _finish_head(h, refs.o_intra[h], qh * jnp.exp(bh), kh * jnp.exp(b_last - bh), b_last,
                         v16[:, vs], gate, refs.s_prev[h], refs, dv)


def _chunk_end(last, refs):
    @pl.when(last)
    def _():
        refs.s_out[...] = refs.s[...]


def _prompt_mixers_kernel(zu_ref, zv_ref, lng_ref, lnb_ref, ws_ref, bst_ref,
                          hq_ref, hf_ref, hi_ref, hg_ref, lb_ref, hgn_ref,
                          gq_ref, gk_ref, gv_ref, gr_ref, glr_ref, wup_ref, bup_ref, glan_ref,
                          ya_ref, yb_ref, shg_ref, yc_ref, sgla_ref, *scratch, l):
    hg = _RecRefs(hgn_ref, yb_ref, shg_ref, *scratch[:5])
    gla = _RecRefs(glan_ref, yc_ref, sgla_ref, *scratch[5:])
    hg_dims = dict(heads=HG_HEADS, dk=HG_DK, dv=HG_DV)
    gla_dims = dict(heads=GLA_HEADS, dk=GLA_DK, dv=GLA_DV)
    c = pl.program_id(1)
    last = c == pl.num_programs(1) - 1
    _chunk_start(c, hg)
    _chunk_start(c, gla)

    _sgu_prompt_chunk(zu_ref, zv_ref, lng_ref, lnb_ref, ws_ref, bst_ref, ya_ref)

    hq, hk, hlf = _hgrn_prep(hq_ref[...], hf_ref[...], lb_ref, l)
    hv, hgate = hi_ref[...], hg_ref[...]
    h_unsafe, h_b, h_v16 = _chunk_main(hq, hk, hv, hlf, hgate, hg, **hg_dims)
    gq, gk, gla_la = _gla_prep(gq_ref[...], gk_ref[...], glr_ref[...], wup_ref, bup_ref)
    gv, ggate = gv_ref[...], gr_ref[...]
    g_unsafe, g_b, g_v16 = _chunk_main(gq, gk, gv, gla_la, ggate, gla, **gla_dims)

    _chunk_redo(h_unsafe, hq, hk, hv, h_b, h_v16, hgate, hg, **hg_dims)
    _chunk_redo(g_unsafe, gq, gk, gv, g_b, g_v16, ggate, gla, **gla_dims)
    _chunk_end(last, hg)
    _chunk_end(last, gla)


def _prompt_mixers(z, z_glr, ln_g, ln_b, w_s, b_s_t, hg_lb, hg_norm_g, w_up, b_up, gla_norm_g, l):
    n_c = SEQ // REC_CHUNK
    w = BRANCH_WIDTH
    kw = GLA_DK_TOTAL
    assert REC_CHUNK == SGU_CHUNK

    def zcol(width, block):
        return pl.BlockSpec((REC_CHUNK, width), lambda b, c: (b * n_c + c, block))

    def const(shape):
        return pl.BlockSpec(shape, lambda b, c: (0,) * len(shape))

    def state_spec(heads, dk, dv):
        return pl.BlockSpec((None, heads, dk, dv), lambda b, c: (b, 0, 0, 0))

    y_spec = pl.BlockSpec((REC_CHUNK, w), lambda b, c: (b * n_c + c, 0))
    in_specs = [
        zcol(w, 0), zcol(w, 1), const((1, w)), const((1, w)),
        pl.BlockSpec((None, SGU_GROUPS, SGU_CHUNK, SGU_CHUNK), lambda b, c: (l, 0, 0, 0)),
        pl.BlockSpec((None, SGU_CHUNK, SGU_GROUPS), lambda b, c: (l, 0, 0)),
        zcol(w, 2), zcol(w, 3), zcol(w, 4), zcol(w, 5), const((DEPTH, w)), const((1, HG_DV)),
        zcol(kw, 6 * w // kw), zcol(kw, 6 * w // kw + 1), zcol(w, 7), zcol(w, 8), zcol(LANES, 0),
        pl.BlockSpec((None, GLA_RANK, kw), lambda b, c: (l, 0, 0)), const((1, kw)), const((1, GLA_DV)),
    ]
    args = [z, z, ln_g[l].reshape(1, w), ln_b[l].reshape(1, w), w_s, b_s_t,
            z, z, z, z, hg_lb, hg_norm_g[l].reshape(1, HG_DV),
            z, z, z, z, z_glr, w_up, b_up[l].reshape(1, kw), gla_norm_g[l].reshape(1, GLA_DV)]
    return pl.pallas_call(
        functools.partial(_prompt_mixers_kernel, l=l),
        grid=(BATCH, n_c),
        in_specs=in_specs,
        out_specs=[y_spec, y_spec, state_spec(HG_HEADS, HG_DK, HG_DV),
                   y_spec, state_spec(GLA_HEADS, GLA_DK, GLA_DV)],
        out_shape=[
            jax.ShapeDtypeStruct((N_ALL, w), BF16),
            jax.ShapeDtypeStruct((N_ALL, w), BF16),
            jax.ShapeDtypeStruct((BATCH, HG_HEADS, HG_DK, HG_DV), F32),
            jax.ShapeDtypeStruct((N_ALL, w), BF16),
            jax.ShapeDtypeStruct((BATCH, GLA_HEADS, GLA_DK, GLA_DV), F32),
        ],
        scratch_shapes=_rec_scratch(HG_HEADS, HG_DK, HG_DV) + _rec_scratch(GLA_HEADS, GLA_DK, GLA_DV),
        compiler_params=_cparams(("parallel", "arbitrary")),
        name="prompt_mixers",
    )(*args)


def _sample_step(q, k, v, log_f, gate, gnorm_ref, s_ref, so_ref, i, *, heads, dk, dv):
    outs = []
    for h in range(heads):
        ks = slice(h * dk, (h + 1) * dk)
        f_col = _lane_bcast_col(jnp.exp(log_f[:, ks]))
        k_col = _lane_bcast_col(k[:, ks])
        q16 = jnp.broadcast_to(q[:, ks], (16, dk)).astype(BF16)
        parts = []
        for j in range(dv // LANES):
            ls = slice(j * LANES, (j + 1) * LANES)
            v_row = v[:, h * dv + j * LANES:h * dv + (j + 1) * LANES]
            s_new = f_col * s_ref[i, h, :, ls] + k_col * v_row
            so_ref[i, h, :, ls] = s_new
            parts.append(jnp.dot(q16, s_new.astype(BF16), preferred_element_type=F32)[0:1])
        o = parts[0] if len(parts) == 1 else jnp.concatenate(parts, axis=1)
        outs.append(_head_out(o, gnorm_ref[...], gate[:, h * dv:(h + 1) * dv]))
    return jnp.concatenate(outs, axis=1)


def _flush_sample_rows(y_ref, yacc_ref):
    @pl.when(pl.program_id(0) == pl.num_programs(0) - 1)
    def _():
        y_ref[...] = yacc_ref[...].astype(BF16)


def _gate_tile(h_ref, w_ref, g_ref):
    w = w_ref[...].astype(BF16)
    for r in range(0, MM_TM, MM_ROW_SUB):
        rows = slice(r, r + MM_ROW_SUB)
        g_ref[rows, :] = _sigmoid(_nt(h_ref[rows, :], w)).astype(BF16)


def _unit_index(step):
    return jnp.minimum(step, DEC_BATCH // SAMPLE_NB - 1)


def _hgrn_sample_kernel(h_ref, w_ref, zq_ref, zf_ref, zi_ref, zg_ref, lb_ref, gnorm_ref, s_ref, *rest, l):
    g_ref, y_ref, so_ref, yacc_ref = rest[-4:]
    _gate_tile(h_ref, w_ref, g_ref)
    base = _unit_index(pl.program_id(0)) * SAMPLE_NB
    q, k, log_f = _hgrn_prep(zq_ref[...], zf_ref[...], lb_ref, l)
    v, gate = zi_ref[...], zg_ref[...]
    for i in range(SAMPLE_NB):
        row = slice(i, i + 1)
        yacc_ref[pl.ds(base + i, 1), :] = _sample_step(
            q[row], k[row], v[row], log_f[row], gate[row], gnorm_ref, s_ref, so_ref, i,
            heads=HG_HEADS, dk=HG_DK, dv=HG_DV)
    _flush_sample_rows(y_ref, yacc_ref)


def _gla_sample_kernel(h_ref, w_ref, zq_ref, zk_ref, zv_ref, zr_ref, glr_ref, wup_ref, bup_ref,
                       gnorm_ref, s_ref, *rest):
    g_ref, y_ref, so_ref, yacc_ref = rest[-4:]
    _gate_tile(h_ref, w_ref, g_ref)
    base = _unit_index(pl.program_id(0)) * SAMPLE_NB
    glr = jnp.concatenate([glr_ref[...], jnp.zeros((16 - SAMPLE_NB, LANES), F32)], axis=0)
    q, k, log_a = _gla_prep(zq_ref[...], zk_ref[...], glr, wup_ref, bup_ref)
    v, gate = zv_ref[...], zr_ref[...]
    for i in range(SAMPLE_NB):
        row = slice(i, i + 1)
        yacc_ref[pl.ds(base + i, 1), :] = _sample_step(
            q[row], k[row], v[row], log_a[row], gate[row], gnorm_ref, s_ref, so_ref, i,
            heads=GLA_HEADS, dk=GLA_DK, dv=GLA_DV)
    _flush_sample_rows(y_ref, yacc_ref)


def _zrow(width, block):
    return pl.BlockSpec((SAMPLE_NB, width), lambda j: (N_PROMPT // SAMPLE_NB + _unit_index(j), block))


def _gate_sample_call(kernel, in_specs, args, h, w_in_t, state, y_all, new_state, gates, row_tile, l,
                      heads, dk, dv, name):
    tn = GATE_TN
    n_steps = N_BRANCH * D_MODEL // tn
    assert n_steps >= DEC_BATCH // SAMPLE_NB
    h_spec = pl.BlockSpec((MM_TM, D_MODEL), lambda j: (row_tile, 0), pipeline_mode=pl.Buffered(1))
    w_spec = pl.BlockSpec((None, pl.Element(tn), pl.Element(D_MODEL)),
                          lambda j: (l, pl.multiple_of(OFF_GATE + j * tn, 8), 0))
    st = pl.BlockSpec((None, SAMPLE_NB, heads, dk, dv), lambda j: (l, _unit_index(j), 0, 0, 0))
    any_spec = pl.BlockSpec(memory_space=pl.ANY)
    in_specs = [h_spec, w_spec] + in_specs + [st, any_spec]
    args = [h, w_in_t] + list(args) + [state, y_all]
    aliases = {len(args) - 1: 1}
    for buf, out_idx in ((new_state, 2), (gates, 0)):
        if buf is not None:
            aliases[len(args)] = out_idx
            in_specs.append(any_spec)
            args.append(buf)
    return pl.pallas_call(
        kernel,
        grid=(n_steps,),
        in_specs=in_specs,
        out_specs=[
            pl.BlockSpec((MM_TM, tn), lambda j: (row_tile, j)),
            pl.BlockSpec((DEC_BATCH, BRANCH_WIDTH), lambda j: (N_PROMPT // DEC_BATCH, 0)),
            st,
        ],
        out_shape=[
            jax.ShapeDtypeStruct((N_ALL, N_BRANCH * D_MODEL), BF16),
            jax.ShapeDtypeStruct((N_ALL, BRANCH_WIDTH), BF16),
            jax.ShapeDtypeStruct((DEPTH, DEC_BATCH, heads, dk, dv), F32),
        ],
        scratch_shapes=[pltpu.VMEM((DEC_BATCH, BRANCH_WIDTH), F32)],
        input_output_aliases=aliases,
        compiler_params=_cparams(("arbitrary",)),
        name=name,
    )(*args)


def _hgrn_sample(h, w_in_t, z, hg_lb, g_norm, state, y_all, new_state, gates, row_tile, l):
    w = BRANCH_WIDTH
    in_specs = [_zrow(w, 2), _zrow(w, 3), _zrow(w, 4), _zrow(w, 5),
                pl.BlockSpec((DEPTH, w), lambda j: (0, 0)),
                pl.BlockSpec((1, HG_DV), lambda j: (0, 0))]
    args = [z, z, z, z, hg_lb, g_norm[l].reshape(1, HG_DV)]
    return _gate_sample_call(functools.partial(_hgrn_sample_kernel, l=l), in_specs, args, h, w_in_t,
                             state, y_all, new_state, gates, row_tile, l,
                             HG_HEADS, HG_DK, HG_DV, "gate_hgrn_sample")


def _gla_sample(h, w_in_t, z, z_glr, w_up, b_up, g_norm, state, y_all, new_state, gates, row_tile, l):
    w = BRANCH_WIDTH
    kw = GLA_DK_TOTAL
    in_specs = [_zrow(kw, 6 * w // kw), _zrow(kw, 6 * w // kw + 1), _zrow(w, 7), _zrow(w, 8),
                _zrow(LANES, 0),
                pl.BlockSpec((None, GLA_RANK, kw), lambda j: (l, 0, 0)),
                pl.BlockSpec((1, kw), lambda j: (0, 0)),
                pl.BlockSpec((1, GLA_DV), lambda j: (0, 0))]
    args = [z, z, z, z, z_glr, w_up, b_up[l].reshape(1, kw), g_norm[l].reshape(1, GLA_DV)]
    return _gate_sample_call(_gla_sample_kernel, in_specs, args, h, w_in_t, state, y_all, new_state,
                             gates, row_tile, l, GLA_HEADS, GLA_DK, GLA_DV, "gate_gla_sample")


def kernel(x_prompt, x_sample, state_hgrn, state_gla, c_prompt, c_sample, w_ada, b_ada, g_pre_mix, g_post_mix, g_pre_mlp, g_post_mlp, w_in, sgu_ln_g, sgu_ln_b, sgu_w_s, sgu_b_s, hg_lb, hg_norm_g, gla_w_up, gla_b_up, gla_norm_g, w_branch, w_out, w_mlp_up, w_mlp_down):
    x = x_prompt.reshape(N_PROMPT, D_MODEL)
    x_s = x_sample.reshape(DEC_BATCH, D_MODEL)
    c_all = jnp.concatenate([c_sample, c_prompt, jnp.zeros((MOD_ROWS - DEC_BATCH - BATCH, D_MODEL), F32)], axis=0)
    mod = _modulation(c_all, w_ada, b_ada)
    w_in_t = jnp.swapaxes(w_in, 1, 2)
    b_s_t = jnp.swapaxes(sgu_b_s, 1, 2)

    hg_p, gla_p, v_rows = [], [], []
    hg_s = gla_s = None
    h = _prenorm(x, x_s, g_pre_mix[0], mod, 0, 0, 1)
    for l in range(DEPTH):
        z = _matmul(h, w_in_t, l, n_out=Z_MAIN, tn=WIDE_TN, w_transposed=True, single_buffer_a=True,
                    name="in_proj")
        z_glr = _matmul(h, w_in_t, l, n_out=LANES, col_off=OFF_GLR, tn=LANES, w_transposed=True,
                        name="in_proj_lowrank")

        y_a, y_b, s_hg_p, y_c, s_gla_p = _prompt_mixers(
            z, z_glr, sgu_ln_g, sgu_ln_b, sgu_w_s, b_s_t, hg_lb, hg_norm_g, gla_w_up, gla_b_up,
            gla_norm_g, l)
        y_a, v_s = _sgu_sample(z, y_a, sgu_ln_g, sgu_ln_b, sgu_w_s, b_s_t, l)
        gates, y_b, hg_s = _hgrn_sample(h, w_in_t, z, hg_lb, hg_norm_g, state_hgrn, y_b, hg_s,
                                        None, 0, l)
        gates, y_c, gla_s = _gla_sample(h, w_in_t, z, z_glr, gla_w_up, gla_b_up, gla_norm_g,
                                        state_gla, y_c, gla_s, gates, 1, l)

        merged = _merge(y_a, y_b, y_c, w_branch, gates, l)
        out = _matmul(merged, w_out, l, n_out=D_MODEL, name="out_proj")
        x, h2 = _resid(x, out, g_post_mix[l], mod, l, 2, nxt=(g_pre_mlp[l], l, 3, 4),
                       x_sample=x_s if l == 0 else None)

        up = _matmul(h2, w_mlp_up, l, n_out=D_FF, tn=WIDE_TN, act=_act_relu2, out_dtype=BF16,
                     single_buffer_a=True, name="mlp_up")
        y2 = _matmul(up, w_mlp_down, l, n_out=D_MODEL, tm=MM_TM_HALF, tn=1024, tk=1024,
                     row_sub=MM_TM_HALF // 2, name="mlp_down")
        if l + 1 < DEPTH:
            x, h = _resid(x, y2, g_post_mlp[l], mod, l, 5, nxt=(g_pre_mix[l + 1], l + 1, 0, 1))
        else:
            y_prompt, y_sample = _resid(x, y2, g_post_mlp[l], mod, l, 5, split_out=True)

        hg_p.append(s_hg_p)
        gla_p.append(s_gla_p)
        v_rows.append(v_s.reshape(DEC_BATCH, 1, BRANCH_WIDTH))

    y_prompt = y_prompt.reshape(BATCH, SEQ, D_MODEL)
    y_sample = y_sample.reshape(DEC_BATCH, 1, D_MODEL)
    return (y_prompt, y_sample, jnp.stack(hg_p), jnp.stack(gla_p), hg_s, gla_s, jnp.stack(v_rows))
```

```python
import functools
from typing import NamedTuple

import jax
import jax.numpy as jnp
from jax import lax
from jax.experimental import pallas as pl
from jax.experimental.pallas import tpu as pltpu

F32 = jnp.float32
BF16 = jnp.bfloat16

D_MODEL = 2048
BATCH = 4
SEQ = 2048
DEPTH = 4
DEC_BATCH = 128
BRANCH_WIDTH = D_MODEL // 2
N_BRANCH = 3
SGU_GROUPS = 8
SGU_CHUNK = 128
HG_HEADS = 8
HG_DK = BRANCH_WIDTH // HG_HEADS
HG_DV = BRANCH_WIDTH // HG_HEADS
GLA_HEADS = 4
GLA_DK_TOTAL = BRANCH_WIDTH // 2
GLA_DK = GLA_DK_TOTAL // GLA_HEADS
GLA_DV = BRANCH_WIDTH // GLA_HEADS
GLA_RANK = 16
GLA_TAU = 16.0
D_FF = 4 * D_MODEL
EPS = 1e-6

N_PROMPT = BATCH * SEQ
N_ALL = N_PROMPT + DEC_BATCH
MOD_ROWS = DEC_BATCH + 8

OFF_GLR = 2 * BRANCH_WIDTH + 4 * BRANCH_WIDTH + 2 * GLA_DK_TOTAL + 2 * BRANCH_WIDTH
OFF_GATE = OFF_GLR + GLA_RANK
Z_MAIN = OFF_GLR

LANES = 128
ROW_TILE = 128
NORM_TILE = 640
LAST_NORM_TILE = N_ALL // NORM_TILE - 1
SAMPLE_SUB = (N_PROMPT - LAST_NORM_TILE * NORM_TILE) // ROW_TILE
MAX_SPLIT_EXPONENT = 60.0
MM_TM = 4160
MM_ROW_SUB = 832
MM_TM_HALF = 2080
REC_CHUNK = 128
HALF = REC_CHUNK // 2
SAMPLE_NB = 8
GATE_TN = 256
WIDE_TN = 512
VMEM_LIMIT = 56 * 1024 * 1024


def _cparams(sem):
    return pltpu.CompilerParams(dimension_semantics=sem, vmem_limit_bytes=VMEM_LIMIT)


def _sigmoid(x):
    return 0.5 + 0.5 * jnp.tanh(0.5 * x)


def _silu(x):
    return x * _sigmoid(x)


def _log_sigmoid(x):
    return jnp.minimum(x, 0.0) - jnp.log(1.0 + jnp.exp(-jnp.abs(x)))


def _rms(x, g):
    return x * lax.rsqrt(jnp.mean(x * x, axis=-1, keepdims=True) + EPS) * g


def _mod_kernel(c_ref, w_ref, b_ref, o_ref):
    a = _silu(c_ref[...]).astype(BF16)
    o_ref[...] = jnp.dot(a, w_ref[...].astype(BF16), preferred_element_type=F32) + b_ref[...]


def _modulation(c_all, w_ada, b_ada):
    tn = 1024
    n = 6 * D_MODEL
    return pl.pallas_call(
        _mod_kernel,
        grid=(DEPTH, n // tn),
        in_specs=[
            pl.BlockSpec((MOD_ROWS, D_MODEL), lambda l, j: (0, 0)),
            pl.BlockSpec((None, D_MODEL, tn), lambda l, j: (l, 0, j)),
            pl.BlockSpec((None, 1, tn), lambda l, j: (l, 0, j)),
        ],
        out_specs=pl.BlockSpec((None, MOD_ROWS, tn), lambda l, j: (l, 0, j)),
        out_shape=jax.ShapeDtypeStruct((DEPTH, MOD_ROWS, n), F32),
        compiler_params=_cparams(("parallel", "parallel")),
        name="adaln_mod",
    )(c_all, w_ada, b_ada.reshape(DEPTH, 1, n))


def _mod_specs(l, j):
    return [
        pl.BlockSpec((None, DEC_BATCH, D_MODEL), lambda i: (l, 0, j)),
        pl.BlockSpec((None, 8, D_MODEL), lambda i: (l, DEC_BATCH // 8, j)),
    ]


def _pick_mod(s_ref, p_ref, sub):
    g = pl.program_id(0) * (NORM_TILE // ROW_TILE) + sub
    b = jnp.minimum(g // (SEQ // ROW_TILE), BATCH - 1)
    is_sample = g >= N_PROMPT // ROW_TILE
    return jnp.where(is_sample, s_ref[...], p_ref[pl.ds(b, 1), :])


def _x_rows(x_ref, xs_ref, sub):
    x = x_ref[sub * ROW_TILE:(sub + 1) * ROW_TILE, :]
    if xs_ref is None or sub != SAMPLE_SUB:
        return x
    return jnp.where(pl.program_id(0) == LAST_NORM_TILE, xs_ref[...], x)


def _prenorm_kernel(x_ref, xs_ref, g_ref, sc_s, sc_p, sh_s, sh_p, h_ref):
    for sub in range(NORM_TILE // ROW_TILE):
        rows = slice(sub * ROW_TILE, (sub + 1) * ROW_TILE)
        h = (_rms(_x_rows(x_ref, xs_ref, sub), g_ref[...]) * (1.0 + _pick_mod(sc_s, sc_p, sub))
             + _pick_mod(sh_s, sh_p, sub))
        h_ref[rows, :] = h.astype(BF16)


def _prenorm(x_p, x_s, g, mod, l, j_shift, j_scale):
    row = pl.BlockSpec((NORM_TILE, D_MODEL), lambda i: (i, 0))
    vec = pl.BlockSpec((1, D_MODEL), lambda i: (0, 0))
    smp = pl.BlockSpec((DEC_BATCH, D_MODEL), lambda i: (0, 0))
    return pl.pallas_call(
        _prenorm_kernel,
        grid=(N_ALL // NORM_TILE,),
        in_specs=[row, smp, vec] + _mod_specs(l, j_scale) + _mod_specs(l, j_shift),
        out_specs=row,
        out_shape=jax.ShapeDtypeStruct((N_ALL, D_MODEL), BF16),
        compiler_params=_cparams(("parallel",)),
        name="prenorm",
    )(x_p, x_s, g.reshape(1, D_MODEL), mod, mod, mod, mod)


def _resid_kernel(x_ref, *rest, with_next, split_in, split_out):
    xs_ref = None
    if split_in:
        xs_ref, *rest = rest
    y_ref, gpost_ref, gt_s, gt_p, *rest = rest
    for sub in range(NORM_TILE // ROW_TILE):
        rows = slice(sub * ROW_TILE, (sub + 1) * ROW_TILE)
        x_new = (_x_rows(x_ref, xs_ref, sub)
                 + _pick_mod(gt_s, gt_p, sub) * _rms(y_ref[rows, :], gpost_ref[...]))
        if with_next:
            gpre_ref, sc_s, sc_p, sh_s, sh_p, xo_ref, h_ref = rest
            xo_ref[rows, :] = x_new
            h = (_rms(x_new, gpre_ref[...]) * (1.0 + _pick_mod(sc_s, sc_p, sub))
                 + _pick_mod(sh_s, sh_p, sub))
            h_ref[rows, :] = h.astype(BF16)
        elif split_out:
            xo_ref, xso_ref = rest
            xo_ref[rows, :] = x_new
            if sub == SAMPLE_SUB:
                @pl.when(pl.program_id(0) == LAST_NORM_TILE)
                def _():
                    xso_ref[...] = x_new
        else:
            (xo_ref,) = rest
            xo_ref[rows, :] = x_new


def _resid(x, y, g_post, mod, l_gate, j_gate, nxt=None, x_sample=None, split_out=False):
    row = pl.BlockSpec((NORM_TILE, D_MODEL), lambda i: (i, 0))
    vec = pl.BlockSpec((1, D_MODEL), lambda i: (0, 0))
    smp = pl.BlockSpec((DEC_BATCH, D_MODEL), lambda i: (0, 0))
    in_specs = [row] + ([smp] if x_sample is not None else []) + [row, vec] + _mod_specs(l_gate, j_gate)
    args = [x] + ([x_sample] if x_sample is not None else []) + [y, g_post.reshape(1, D_MODEL), mod, mod]
    out_specs = [row]
    out_shape = [jax.ShapeDtypeStruct((N_ALL, D_MODEL), F32)]
    if nxt is not None:
        assert not split_out
        g_pre, l_n, j_shift, j_scale = nxt
        in_specs += [vec] + _mod_specs(l_n, j_scale) + _mod_specs(l_n, j_shift)
        args += [g_pre.reshape(1, D_MODEL), mod, mod, mod, mod]
        out_specs.append(row)
        out_shape.append(jax.ShapeDtypeStruct((N_ALL, D_MODEL), BF16))
    elif split_out:
        out_specs.append(smp)
        out_shape = [jax.ShapeDtypeStruct((N_PROMPT, D_MODEL), F32),
                     jax.ShapeDtypeStruct((DEC_BATCH, D_MODEL), F32)]
    return pl.pallas_call(
        functools.partial(_resid_kernel, with_next=nxt is not None, split_in=x_sample is not None,
                          split_out=split_out),
        grid=(N_ALL // NORM_TILE,),
        in_specs=in_specs,
        out_specs=out_specs,
        out_shape=out_shape,
        compiler_params=_cparams(("arbitrary",) if split_out else ("parallel",)),
        name="resid_norm",
    )(*args)


def _act_none(x):
    return x


def _act_relu2(x):
    return jnp.square(jnp.maximum(x, 0.0))


def _nt(a, b):
    return lax.dot_general(a, b, (((1,), (1,)), ((), ())), preferred_element_type=F32)


def _mm_kernel(a_ref, w_ref, o_ref, *, nk, act, w_transposed, row_sub):
    w = w_ref[...].astype(BF16)
    if nk > 1:
        @pl.when(pl.program_id(2) == 0)
        def _():
            o_ref[...] = jnp.zeros_like(o_ref)

    for r in range(0, a_ref.shape[0], row_sub):
        rows = slice(r, r + row_sub)
        a = a_ref[rows, :]
        part = _nt(a, w) if w_transposed else jnp.dot(a, w, preferred_element_type=F32)
        if nk == 1:
            o_ref[rows, :] = act(part).astype(o_ref.dtype)
        else:
            o_ref[rows, :] += part


def _matmul(a, w, l, *, n_out, col_off=0, tm=MM_TM, tn=256, tk=None, row_sub=MM_ROW_SUB,
            act=_act_none, out_dtype=F32, w_transposed=False, single_buffer_a=False, name):
    k_dim = a.shape[1]
    tk = k_dim if tk is None else tk
    nk = k_dim // tk
    assert nk == 1 or (act is _act_none and out_dtype == F32)
    if single_buffer_a:
        a_spec = pl.BlockSpec((tm, tk), lambda i, j, k: (i, k), pipeline_mode=pl.Buffered(1))
    else:
        a_spec = pl.BlockSpec((tm, tk), lambda i, j, k: (i, k))
    if w_transposed:
        assert col_off % 8 == 0 and tn % 8 == 0 and tk % LANES == 0
        w_spec = pl.BlockSpec((None, pl.Element(tn), pl.Element(tk)),
                              lambda i, j, k: (l, pl.multiple_of(col_off + j * tn, 8),
                                               pl.multiple_of(k * tk, LANES)))
    else:
        w_spec = pl.BlockSpec((None, tk, tn), lambda i, j, k: (l, k, col_off // tn + j))
    return pl.pallas_call(
        functools.partial(_mm_kernel, nk=nk, act=act, w_transposed=w_transposed, row_sub=row_sub),
        grid=(N_ALL // tm, n_out // tn, nk),
        in_specs=[a_spec, w_spec],
        out_specs=pl.BlockSpec((tm, tn), lambda i, j, k: (i, j)),
        out_shape=jax.ShapeDtypeStruct((N_ALL, n_out), out_dtype),
        compiler_params=_cparams(("parallel", "parallel", "arbitrary")),
        name=name,
    )(a, w)


def _merge_kernel(ya_ref, yb_ref, yc_ref, wa_ref, wb_ref, wc_ref, ga_ref, gb_ref, gc_ref, o_ref):
    wa = wa_ref[...].astype(BF16)
    wb = wb_ref[...].astype(BF16)
    wc = wc_ref[...].astype(BF16)
    sub = MM_TM_HALF // 2
    for r in range(0, MM_TM_HALF, sub):
        rows = slice(r, r + sub)
        merged = ga_ref[rows, :] * jnp.dot(ya_ref[rows, :], wa, preferred_element_type=F32)
        merged = merged + gb_ref[rows, :] * jnp.dot(yb_ref[rows, :], wb, preferred_element_type=F32)
        merged = merged + gc_ref[rows, :] * jnp.dot(yc_ref[rows, :], wc, preferred_element_type=F32)
        o_ref[rows, :] = merged.astype(BF16)


def _merge(ya, yb, yc, w_branch, gates, l):
    tn = 256
    tm = MM_TM_HALF
    nb = D_MODEL // tn
    y_spec = pl.BlockSpec((tm, BRANCH_WIDTH), lambda i, j: (i, 0))

    def w_spec(b):
        return pl.BlockSpec((None, None, BRANCH_WIDTH, tn), lambda i, j: (l, b, 0, j))

    def g_spec(b):
        return pl.BlockSpec((tm, tn), lambda i, j: (i, b * nb + j))

    return pl.pallas_call(
        _merge_kernel,
        grid=(N_ALL // tm, nb),
        in_specs=[y_spec, y_spec, y_spec, w_spec(0), w_spec(1), w_spec(2),
                  g_spec(0), g_spec(1), g_spec(2)],
        out_specs=pl.BlockSpec((tm, tn), lambda i, j: (i, j)),
        out_shape=jax.ShapeDtypeStruct((N_ALL, D_MODEL), BF16),
        compiler_params=_cparams(("parallel", "parallel")),
        name="branch_merge",
    )(ya, yb, yc, w_branch, w_branch, w_branch, gates, gates, gates)


def _sgu_uv(zu_ref, zv_ref, lg_ref, lb_ref):
    u = jax.nn.gelu(zu_ref[...], approximate=True)
    v = jax.nn.gelu(zv_ref[...], approximate=True)
    mu = jnp.mean(v, axis=-1, keepdims=True)
    vc = v - mu
    v = vc * lax.rsqrt(jnp.mean(vc * vc, axis=-1, keepdims=True) + EPS) * lg_ref[...] + lb_ref[...]
    return u, v


def _sgu_prompt_chunk(zu_ref, zv_ref, lg_ref, lb_ref, ws_ref, bst_ref, y_ref):
    u, v = _sgu_uv(zu_ref, zv_ref, lg_ref, lb_ref)
    n_idx = lax.broadcasted_iota(jnp.int32, (SGU_CHUNK, SGU_CHUNK), 0)
    m_idx = lax.broadcasted_iota(jnp.int32, (SGU_CHUNK, SGU_CHUNK), 1)
    causal = m_idx <= n_idx
    for g in range(SGU_GROUPS):
        sl = slice(g * LANES, (g + 1) * LANES)
        wm = jnp.where(causal, ws_ref[g], 0.0).astype(BF16)
        s = jnp.dot(wm, v[:, sl].astype(BF16), preferred_element_type=F32) + bst_ref[:, g:g + 1]
        y_ref[:, sl] = (u[:, sl] * s).astype(BF16)


def _sgu_sample_kernel(zu_ref, zv_ref, lg_ref, lb_ref, ws_ref, bst_ref, ya_any, y_ref, vs_ref):
    del ya_any
    u, v = _sgu_uv(zu_ref, zv_ref, lg_ref, lb_ref)
    vs_ref[...] = v
    for g in range(SGU_GROUPS):
        sl = slice(g * LANES, (g + 1) * LANES)
        s = v[:, sl] * ws_ref[g][0:1, 0:1] + bst_ref[0:1, g:g + 1]
        y_ref[:, sl] = (u[:, sl] * s).astype(BF16)


def _sgu_sample(z, y_a, ln_g, ln_b, w_s, b_s_t, l):
    blk = N_PROMPT // ROW_TILE
    vec = pl.BlockSpec((1, BRANCH_WIDTH), lambda i: (0, 0))
    tile = pl.BlockSpec((ROW_TILE, BRANCH_WIDTH), lambda i: (blk, 0))
    return pl.pallas_call(
        _sgu_sample_kernel,
        grid=(1,),
        in_specs=[
            tile,
            pl.BlockSpec((ROW_TILE, BRANCH_WIDTH), lambda i: (blk, 1)),
            vec, vec,
            pl.BlockSpec((None, SGU_GROUPS, SGU_CHUNK, SGU_CHUNK), lambda i: (l, 0, 0, 0)),
            pl.BlockSpec((None, SGU_CHUNK, SGU_GROUPS), lambda i: (l, 0, 0)),
            pl.BlockSpec(memory_space=pl.ANY),
        ],
        out_specs=[tile, pl.BlockSpec((DEC_BATCH, BRANCH_WIDTH), lambda i: (0, 0))],
        out_shape=[
            jax.ShapeDtypeStruct((N_ALL, BRANCH_WIDTH), BF16),
            jax.ShapeDtypeStruct((DEC_BATCH, BRANCH_WIDTH), F32),
        ],
        input_output_aliases={6: 0},
        compiler_params=_cparams(("arbitrary",)),
        name="sgu_sample",
    )(z, z, ln_g[l].reshape(1, BRANCH_WIDTH), ln_b[l].reshape(1, BRANCH_WIDTH), w_s, b_s_t, y_a)


def _hgrn_lower_bound(lb_ref, l):
    raw = lb_ref[...]
    e = jnp.exp(raw - jnp.max(raw, axis=0, keepdims=True))
    p = e / jnp.sum(e, axis=0, keepdims=True)
    acc = p[0:1, :]
    for j in range(1, l + 1):
        acc = acc + p[j:j + 1, :]
    return jnp.maximum(acc - p[0:1, :], 0.0)


def _hgrn_prep(zq, zf, lb_ref, l):
    e = jnp.exp(-jnp.abs(zf))
    ope = 1.0 + e
    sig_neg = jnp.where(zf >= 0.0, e, 1.0) / ope
    log_sig = jnp.minimum(zf, 0.0) - jnp.log(ope)
    if l == 0:
        return _silu(zq), sig_neg, log_sig
    lb = _hgrn_lower_bound(lb_ref, l)
    a = jnp.log(lb)
    c = jnp.log1p(-lb) + log_sig
    log_f = jnp.maximum(a, c) + jnp.log(1.0 + jnp.exp(-jnp.abs(a - c)))
    return _silu(zq), (1.0 - lb) * sig_neg, log_f


def _gla_prep(zq, zk, glr, wup_ref, bup_ref):
    x = jnp.dot(glr[:, :GLA_RANK].astype(BF16), wup_ref[...].astype(BF16),
                preferred_element_type=F32) + bup_ref[...]
    log_a = _log_sigmoid(x) / GLA_TAU
    return zq, zk * (GLA_DK ** -0.5), log_a


def _lane_bcast_col(row):
    return jnp.broadcast_to(row, (LANES, LANES)).T


def _head_out(o, g_row, gate):
    o = o * lax.rsqrt(jnp.mean(o * o, axis=-1, keepdims=True) + EPS) * g_row
    return o * _silu(gate)


class _RecRefs(NamedTuple):
    gnorm: object
    y: object
    s_out: object
    s: object
    s_prev: object
    q: object
    b: object
    o_intra: object


def _rec_scratch(heads, dk, dv):
    return [
        pltpu.VMEM((heads, dk, dv), F32),
        pltpu.VMEM((heads, dk, dv), F32),
        pltpu.VMEM((REC_CHUNK, heads * dk), F32),
        pltpu.VMEM((REC_CHUNK, heads * dk), F32),
        pltpu.VMEM((heads, REC_CHUNK, dv), F32),
    ]


def _finish_head(h, o_intra, q_dec, k_dec, b_last, vh, gate, s_old, refs, dv):
    vs = slice(h * dv, (h + 1) * dv)
    o = o_intra + jnp.dot(q_dec.astype(BF16), s_old.astype(BF16), preferred_element_type=F32)
    refs.y[:, vs] = _head_out(o, refs.gnorm[...], gate[:, vs]).astype(BF16)
    upd = jnp.dot(k_dec.T.astype(BF16), vh, preferred_element_type=F32)
    decay = _lane_bcast_col(jnp.exp(b_last))
    for j in range(dv // LANES):
        ls = slice(j * LANES, (j + 1) * LANES)
        refs.s[h, :, ls] = decay * s_old[:, ls] + upd[:, ls]


def _chunk_start(c, refs):
    @pl.when(c == 0)
    def _():
        refs.s[...] = jnp.zeros_like(refs.s)


def _chunk_main(q, k, v, log_f, gate, refs, *, heads, dk, dv):
    t_idx = lax.broadcasted_iota(jnp.int32, (REC_CHUNK, REC_CHUNK), 0)
    s_idx = lax.broadcasted_iota(jnp.int32, (REC_CHUNK, REC_CHUNK), 1)
    tril = jnp.where(s_idx <= t_idx, 1.0, 0.0).astype(F32)
    b_all = jnp.dot(tril, log_f, precision=lax.Precision.HIGHEST, preferred_element_type=F32)
    v16 = v.astype(BF16)

    mid = HALF // 2 - 1

    def drop(lo, hi):
        return b_all[lo:lo + 1] - b_all[hi:hi + 1]

    worst = jnp.maximum(jnp.maximum(drop(0, mid), drop(mid, HALF - 1)),
                        jnp.maximum(drop(HALF, HALF + mid), drop(HALF + mid, REC_CHUNK - 1)))
    unsafe = jnp.max(worst) > MAX_SPLIT_EXPONENT

    causal_half = (lax.broadcasted_iota(jnp.int32, (HALF, HALF), 1)
                   <= lax.broadcasted_iota(jnp.int32, (HALF, HALF), 0))
    for h in range(heads):
        ks = slice(h * dk, (h + 1) * dk)
        qh, kh, bh = q[:, ks], k[:, ks], b_all[:, ks]
        vh = v16[:, h * dv:(h + 1) * dv]
        b_a, b_b = bh[:HALF], bh[HALF:]
        v_a, v_b = vh[:HALF], vh[HALF:]
        r_a = b_a[mid:mid + 1]
        r_b = b_b[mid:mid + 1]
        r_m = b_a[HALF - 1:HALF]
        b_last = b_b[HALF - 1:HALF]
        q_a = qh[:HALF] * jnp.exp(b_a - r_a)
        q_b = qh[HALF:] * jnp.exp(b_b - r_b)
        k_a = kh[:HALF] * jnp.exp(r_a - b_a)
        k_b = kh[HALF:] * jnp.exp(r_b - b_b)
        att_aa = jnp.where(causal_half, _nt(q_a.astype(BF16), k_a.astype(BF16)), 0.0).astype(BF16)
        att_bb = jnp.where(causal_half, _nt(q_b.astype(BF16), k_b.astype(BF16)), 0.0).astype(BF16)
        att_ba = _nt((q_b * jnp.exp(r_b - r_m)).astype(BF16),
                     (k_a * jnp.exp(r_m - r_a)).astype(BF16)).astype(BF16)
        o_a = jnp.dot(att_aa, v_a, preferred_element_type=F32)
        o_b = (jnp.dot(att_ba, v_a, preferred_element_type=F32)
               + jnp.dot(att_bb, v_b, preferred_element_type=F32))
        q_dec = jnp.concatenate([q_a * jnp.exp(r_a), q_b * jnp.exp(r_b)], axis=0)
        k_dec = jnp.concatenate([k_a * jnp.exp(b_last - r_a), k_b * jnp.exp(b_last - r_b)], axis=0)
        s_old = refs.s[h]
        refs.s_prev[h] = s_old
        _finish_head(h, jnp.concatenate([o_a, o_b], axis=0), q_dec, k_dec, b_last, vh, gate, s_old,
                     refs, dv)
    return unsafe, b_all, v16


def _chunk_redo(unsafe, q, k, v, b_all, v16, gate, refs, *, heads, dk, dv):
    @pl.when(unsafe)
    def _():
        refs.q[...] = q
        refs.b[...] = b_all
        s_pos = lax.broadcasted_iota(jnp.int32, (REC_CHUNK, 1), 0)
        for h in range(heads):
            ks = slice(h * dk, (h + 1) * dk)
            vs = slice(h * dv, (h + 1) * dv)
            qh, kh, bh = q[:, ks], k[:, ks], b_all[:, ks]
            vh = v[:, vs]

            def row(t, carry):
                b_t = refs.b[pl.ds(t, 1), :][:, ks]
                q_t = refs.q[pl.ds(t, 1), :][:, ks]
                p = jnp.exp(jnp.minimum(b_t - bh, 0.0)) * kh * q_t
                w = jnp.where(s_pos <= t, jnp.sum(p, axis=1, keepdims=True), 0.0)
                refs.o_intra[h, pl.ds(t, 1), :] = jnp.sum(w * vh, axis=0, keepdims=True)
                return carry

            lax.fori_loop(0, REC_CHUNK, row, 0)
            b_last = bh[REC_CHUNK - 1:REC_CHUNK]
            _finish_head(h, refs.o_intra[h], qh * jnp.exp(bh), kh * jnp.exp(b_last - bh), b_last,
                         v16[:, vs], gate, refs.s_prev[h], refs, dv)


def _chunk_end(last, refs):
    @pl.when(last)
    def _():
        refs.s_out[...] = refs.s[...]


def _prompt_mixers_kernel(zu_ref, zv_ref, lng_ref, lnb_ref, ws_ref, bst_ref,
                          hq_ref, hf_ref, hi_ref, hg_ref, lb_ref, hgn_ref,
                          gq_ref, gk_ref, gv_ref, gr_ref, glr_ref, wup_ref, bup_ref, glan_ref,
                          ya_ref, yb_ref, shg_ref, yc_ref, sgla_ref, *scratch, l):
    hg = _RecRefs(hgn_ref, yb_ref, shg_ref, *scratch[:5])
    gla = _RecRefs(glan_ref, yc_ref, sgla_ref, *scratch[5:])
    hg_dims = dict(heads=HG_HEADS, dk=HG_DK, dv=HG_DV)
    gla_dims = dict(heads=GLA_HEADS, dk=GLA_DK, dv=GLA_DV)
    c = pl.program_id(1)
    last = c == pl.num_programs(1) - 1
    _chunk_start(c, hg)
    _chunk_start(c, gla)

    _sgu_prompt_chunk(zu_ref, zv_ref, lng_ref, lnb_ref, ws_ref, bst_ref, ya_ref)

    hq, hk, hlf = _hgrn_prep(hq_ref[...], hf_ref[...], lb_ref, l)
    hv, hgate = hi_ref[...], hg_ref[...]
    h_unsafe, h_b, h_v16 = _chunk_main(hq, hk, hv, hlf, hgate, hg, **hg_dims)
    gq, gk, gla_la = _gla_prep(gq_ref[...], gk_ref[...], glr_ref[...], wup_ref, bup_ref)
    gv, ggate = gv_ref[...], gr_ref[...]
    g_unsafe, g_b, g_v16 = _chunk_main(gq, gk, gv, gla_la, ggate, gla, **gla_dims)

    _chunk_redo(h_unsafe, hq, hk, hv, h_b, h_v16, hgate, hg, **hg_dims)
    _chunk_redo(g_unsafe, gq, gk, gv, g_b, g_v16, ggate, gla, **gla_dims)
    _chunk_end(last, hg)
    _chunk_end(last, gla)


def _prompt_mixers(z, z_glr, ln_g, ln_b, w_s, b_s_t, hg_lb, hg_norm_g, w_up, b_up, gla_norm_g, l):
    n_c = SEQ // REC_CHUNK
    w = BRANCH_WIDTH
    kw = GLA_DK_TOTAL
    assert REC_CHUNK == SGU_CHUNK

    def zcol(width, block):
        return pl.BlockSpec((REC_CHUNK, width), lambda b, c: (b * n_c + c, block))

    def const(shape):
        return pl.BlockSpec(shape, lambda b, c: (0,) * len(shape))

    def state_spec(heads, dk, dv):
        return pl.BlockSpec((None, heads, dk, dv), lambda b, c: (b, 0, 0, 0))

    y_spec = pl.BlockSpec((REC_CHUNK, w), lambda b, c: (b * n_c + c, 0))
    in_specs = [
        zcol(w, 0), zcol(w, 1), const((1, w)), const((1, w)),
        pl.BlockSpec((None, SGU_GROUPS, SGU_CHUNK, SGU_CHUNK), lambda b, c: (l, 0, 0, 0)),
        pl.BlockSpec((None, SGU_CHUNK, SGU_GROUPS), lambda b, c: (l, 0, 0)),
        zcol(w, 2), zcol(w, 3), zcol(w, 4), zcol(w, 5), const((DEPTH, w)), const((1, HG_DV)),
        zcol(kw, 6 * w // kw), zcol(kw, 6 * w // kw + 1), zcol(w, 7), zcol(w, 8), zcol(LANES, 0),
        pl.BlockSpec((None, GLA_RANK, kw), lambda b, c: (l, 0, 0)), const((1, kw)), const((1, GLA_DV)),
    ]
    args = [z, z, ln_g[l].reshape(1, w), ln_b[l].reshape(1, w), w_s, b_s_t,
            z, z, z, z, hg_lb, hg_norm_g[l].reshape(1, HG_DV),
            z, z, z, z, z_glr, w_up, b_up[l].reshape(1, kw), gla_norm_g[l].reshape(1, GLA_DV)]
    return pl.pallas_call(
        functools.partial(_prompt_mixers_kernel, l=l),
        grid=(BATCH, n_c),
        in_specs=in_specs,
        out_specs=[y_spec, y_spec, state_spec(HG_HEADS, HG_DK, HG_DV),
                   y_spec, state_spec(GLA_HEADS, GLA_DK, GLA_DV)],
        out_shape=[
            jax.ShapeDtypeStruct((N_ALL, w), BF16),
            jax.ShapeDtypeStruct((N_ALL, w), BF16),
            jax.ShapeDtypeStruct((BATCH, HG_HEADS, HG_DK, HG_DV), F32),
            jax.ShapeDtypeStruct((N_ALL, w), BF16),
            jax.ShapeDtypeStruct((BATCH, GLA_HEADS, GLA_DK, GLA_DV), F32),
        ],
        scratch_shapes=_rec_scratch(HG_HEADS, HG_DK, HG_DV) + _rec_scratch(GLA_HEADS, GLA_DK, GLA_DV),
        compiler_params=_cparams(("parallel", "arbitrary")),
        name="prompt_mixers",
    )(*args)


def _sample_step(q, k, v, log_f, gate, gnorm_ref, s_ref, so_ref, i, *, heads, dk, dv):
    outs = []
    for h in range(heads):
        ks = slice(h * dk, (h + 1) * dk)
        f_col = _lane_bcast_col(jnp.exp(log_f[:, ks]))
        k_col = _lane_bcast_col(k[:, ks])
        q16 = jnp.broadcast_to(q[:, ks], (16, dk)).astype(BF16)
        parts = []
        for j in range(dv // LANES):
            ls = slice(j * LANES, (j + 1) * LANES)
            v_row = v[:, h * dv + j * LANES:h * dv + (j + 1) * LANES]
            s_new = f_col * s_ref[i, h, :, ls] + k_col * v_row
            so_ref[i, h, :, ls] = s_new
            parts.append(jnp.dot(q16, s_new.astype(BF16), preferred_element_type=F32)[0:1])
        o = parts[0] if len(parts) == 1 else jnp.concatenate(parts, axis=1)
        outs.append(_head_out(o, gnorm_ref[...], gate[:, h * dv:(h + 1) * dv]))
    return jnp.concatenate(outs, axis=1)


def _flush_sample_rows(y_ref, yacc_ref):
    @pl.when(pl.program_id(0) == pl.num_programs(0) - 1)
    def _():
        y_ref[...] = yacc_ref[...].astype(BF16)


def _gate_tile(h_ref, w_ref, g_ref):
    w = w_ref[...].astype(BF16)
    for r in range(0, MM_TM, MM_ROW_SUB):
        rows = slice(r, r + MM_ROW_SUB)
        g_ref[rows, :] = _sigmoid(_nt(h_ref[rows, :], w)).astype(BF16)


def _unit_index(step):
    return jnp.minimum(step, DEC_BATCH // SAMPLE_NB - 1)


def _hgrn_sample_kernel(h_ref, w_ref, zq_ref, zf_ref, zi_ref, zg_ref, lb_ref, gnorm_ref, s_ref, *rest, l):
    g_ref, y_ref, so_ref, yacc_ref = rest[-4:]
    _gate_tile(h_ref, w_ref, g_ref)
    base = _unit_index(pl.program_id(0)) * SAMPLE_NB
    q, k, log_f = _hgrn_prep(zq_ref[...], zf_ref[...], lb_ref, l)
    v, gate = zi_ref[...], zg_ref[...]
    for i in range(SAMPLE_NB):
        row = slice(i, i + 1)
        yacc_ref[pl.ds(base + i, 1), :] = _sample_step(
            q[row], k[row], v[row], log_f[row], gate[row], gnorm_ref, s_ref, so_ref, i,
            heads=HG_HEADS, dk=HG_DK, dv=HG_DV)
    _flush_sample_rows(y_ref, yacc_ref)


def _gla_sample_kernel(h_ref, w_ref, zq_ref, zk_ref, zv_ref, zr_ref, glr_ref, wup_ref, bup_ref,
                       gnorm_ref, s_ref, *rest):
    g_ref, y_ref, so_ref, yacc_ref = rest[-4:]
    _gate_tile(h_ref, w_ref, g_ref)
    base = _unit_index(pl.program_id(0)) * SAMPLE_NB
    glr = jnp.concatenate([glr_ref[...], jnp.zeros((16 - SAMPLE_NB, LANES), F32)], axis=0)
    q, k, log_a = _gla_prep(zq_ref[...], zk_ref[...], glr, wup_ref, bup_ref)
    v, gate = zv_ref[...], zr_ref[...]
    for i in range(SAMPLE_NB):
        row = slice(i, i + 1)
        yacc_ref[pl.ds(base + i, 1), :] = _sample_step(
            q[row], k[row], v[row], log_a[row], gate[row], gnorm_ref, s_ref, so_ref, i,
            heads=GLA_HEADS, dk=GLA_DK, dv=GLA_DV)
    _flush_sample_rows(y_ref, yacc_ref)


def _zrow(width, block):
    return pl.BlockSpec((SAMPLE_NB, width), lambda j: (N_PROMPT // SAMPLE_NB + _unit_index(j), block))


def _gate_sample_call(kernel, in_specs, args, h, w_in_t, state, y_all, new_state, gates, row_tile, l,
                      heads, dk, dv, name):
    tn = GATE_TN
    n_steps = N_BRANCH * D_MODEL // tn
    assert n_steps >= DEC_BATCH // SAMPLE_NB
    h_spec = pl.BlockSpec((MM_TM, D_MODEL), lambda j: (row_tile, 0), pipeline_mode=pl.Buffered(1))
    w_spec = pl.BlockSpec((None, pl.Element(tn), pl.Element(D_MODEL)),
                          lambda j: (l, pl.multiple_of(OFF_GATE + j * tn, 8), 0))
    st = pl.BlockSpec((None, SAMPLE_NB, heads, dk, dv), lambda j: (l, _unit_index(j), 0, 0, 0))
    any_spec = pl.BlockSpec(memory_space=pl.ANY)
    in_specs = [h_spec, w_spec] + in_specs + [st, any_spec]
    args = [h, w_in_t] + list(args) + [state, y_all]
    aliases = {len(args) - 1: 1}
    for buf, out_idx in ((new_state, 2), (gates, 0)):
        if buf is not None:
            aliases[len(args)] = out_idx
            in_specs.append(any_spec)
            args.append(buf)
    return pl.pallas_call(
        kernel,
        grid=(n_steps,),
        in_specs=in_specs,
        out_specs=[
            pl.BlockSpec((MM_TM, tn), lambda j: (row_tile, j)),
            pl.BlockSpec((DEC_BATCH, BRANCH_WIDTH), lambda j: (N_PROMPT // DEC_BATCH, 0)),
            st,
        ],
        out_shape=[
            jax.ShapeDtypeStruct((N_ALL, N_BRANCH * D_MODEL), BF16),
            jax.ShapeDtypeStruct((N_ALL, BRANCH_WIDTH), BF16),
            jax.ShapeDtypeStruct((DEPTH, DEC_BATCH, heads, dk, dv), F32),
        ],
        scratch_shapes=[pltpu.VMEM((DEC_BATCH, BRANCH_WIDTH), F32)],
        input_output_aliases=aliases,
        compiler_params=_cparams(("arbitrary",)),
        name=name,
    )(*args)


def _hgrn_sample(h, w_in_t, z, hg_lb, g_norm, state, y_all, new_state, gates, row_tile, l):
    w = BRANCH_WIDTH
    in_specs = [_zrow(w, 2), _zrow(w, 3), _zrow(w, 4), _zrow(w, 5),
                pl.BlockSpec((DEPTH, w), lambda j: (0, 0)),
                pl.BlockSpec((1, HG_DV), lambda j: (0, 0))]
    args = [z, z, z, z, hg_lb, g_norm[l].reshape(1, HG_DV)]
    return _gate_sample_call(functools.partial(_hgrn_sample_kernel, l=l), in_specs, args, h, w_in_t,
                             state, y_all, new_state, gates, row_tile, l,
                             HG_HEADS, HG_DK, HG_DV, "gate_hgrn_sample")


def _gla_sample(h, w_in_t, z, z_glr, w_up, b_up, g_norm, state, y_all, new_state, gates, row_tile, l):
    w = BRANCH_WIDTH
    kw = GLA_DK_TOTAL
    in_specs = [_zrow(kw, 6 * w // kw), _zrow(kw, 6 * w // kw + 1), _zrow(w, 7), _zrow(w, 8),
                _zrow(LANES, 0),
                pl.BlockSpec((None, GLA_RANK, kw), lambda j: (l, 0, 0)),
                pl.BlockSpec((1, kw), lambda j: (0, 0)),
                pl.BlockSpec((1, GLA_DV), lambda j: (0, 0))]
    args = [z, z, z, z, z_glr, w_up, b_up[l].reshape(1, kw), g_norm[l].reshape(1, GLA_DV)]
    return _gate_sample_call(_gla_sample_kernel, in_specs, args, h, w_in_t, state, y_all, new_state,
                             gates, row_tile, l, GLA_HEADS, GLA_DK, GLA_DV, "gate_gla_sample")


def kernel(x_prompt, x_sample, state_hgrn, state_gla, c_prompt, c_sample, w_ada, b_ada, g_pre_mix, g_post_mix, g_pre_mlp, g_post_mlp, w_in, sgu_ln_g, sgu_ln_b, sgu_w_s, sgu_b_s, hg_lb, hg_norm_g, gla_w_up, gla_b_up, gla_norm_g, w_branch, w_out, w_mlp_up, w_mlp_down):
    x = x_prompt.reshape(N_PROMPT, D_MODEL)
    x_s = x_sample.reshape(DEC_BATCH, D_MODEL)
    c_all = jnp.concatenate([c_sample, c_prompt, jnp.zeros((MOD_ROWS - DEC_BATCH - BATCH, D_MODEL), F32)], axis=0)
    mod = _modulation(c_all, w_ada, b_ada)
    w_in_t = jnp.swapaxes(w_in, 1, 2)
    b_s_t = jnp.swapaxes(sgu_b_s, 1, 2)

    hg_p, gla_p, v_rows = [], [], []
    hg_s = gla_s = None
    h = _prenorm(x, x_s, g_pre_mix[0], mod, 0, 0, 1)
    for l in range(DEPTH):
        z = _matmul(h, w_in_t, l, n_out=Z_MAIN, tn=WIDE_TN, w_transposed=True, single_buffer_a=True,
                    name="in_proj")
        z_glr = _matmul(h, w_in_t, l, n_out=LANES, col_off=OFF_GLR, tn=LANES, w_transposed=True,
                        name="in_proj_lowrank")

        y_a, y_b, s_hg_p, y_c, s_gla_p = _prompt_mixers(
            z, z_glr, sgu_ln_g, sgu_ln_b, sgu_w_s, b_s_t, hg_lb, hg_norm_g, gla_w_up, gla_b_up,
            gla_norm_g, l)
        y_a, v_s = _sgu_sample(z, y_a, sgu_ln_g, sgu_ln_b, sgu_w_s, b_s_t, l)
        gates, y_b, hg_s = _hgrn_sample(h, w_in_t, z, hg_lb, hg_norm_g, state_hgrn, y_b, hg_s,
                                        None, 0, l)
        gates, y_c, gla_s = _gla_sample(h, w_in_t, z, z_glr, gla_w_up, gla_b_up, gla_norm_g,
                                        state_gla, y_c, gla_s, gates, 1, l)

        merged = _merge(y_a, y_b, y_c, w_branch, gates, l)
        out = _matmul(merged, w_out, l, n_out=D_MODEL, name="out_proj")
        x, h2 = _resid(x, out, g_post_mix[l], mod, l, 2, nxt=(g_pre_mlp[l], l, 3, 4),
                       x_sample=x_s if l == 0 else None)

        up = _matmul(h2, w_mlp_up, l, n_out=D_FF, tn=WIDE_TN, act=_act_relu2, out_dtype=BF16,
                     name="mlp_up")
        y2 = _matmul(up, w_mlp_down, l, n_out=D_MODEL, tm=MM_TM_HALF, tn=1024, tk=2048,
                     row_sub=MM_TM_HALF // 4, name="mlp_down")
        if l + 1 < DEPTH:
            x, h = _resid(x, y2, g_post_mlp[l], mod, l, 5, nxt=(g_pre_mix[l + 1], l + 1, 0, 1))
        else:
            y_prompt, y_sample = _resid(x, y2, g_post_mlp[l], mod, l, 5, split_out=True)

        hg_p.append(s_hg_p)
        gla_p.append(s_gla_p)
        v_rows.append(v_s.reshape(DEC_BATCH, 1, BRANCH_WIDTH))

    y_prompt = y_prompt.reshape(BATCH, SEQ, D_MODEL)
    y_sample = y_sample.reshape(DEC_BATCH, 1, D_MODEL)
    return (y_prompt, y_sample, jnp.stack(hg_p), jnp.stack(gla_p), hg_s, gla_s, jnp.stack(v_rows))
```

```python
import functools
from typing import NamedTuple

import jax
import jax.numpy as jnp
from jax import lax
from jax.experimental import pallas as pl
from jax.experimental.pallas import tpu as pltpu

F32 = jnp.float32
BF16 = jnp.bfloat16

D_MODEL = 2048
BATCH = 4
SEQ = 2048
DEPTH = 4
DEC_BATCH = 128
BRANCH_WIDTH = D_MODEL // 2
N_BRANCH = 3
SGU_GROUPS = 8
SGU_CHUNK = 128
HG_HEADS = 8
HG_DK = BRANCH_WIDTH // HG_HEADS
HG_DV = BRANCH_WIDTH // HG_HEADS
GLA_HEADS = 4
GLA_DK_TOTAL = BRANCH_WIDTH // 2
GLA_DK = GLA_DK_TOTAL // GLA_HEADS
GLA_DV = BRANCH_WIDTH // GLA_HEADS
GLA_RANK = 16
GLA_TAU = 16.0
D_FF = 4 * D_MODEL
EPS = 1e-6

N_PROMPT = BATCH * SEQ
N_ALL = N_PROMPT + DEC_BATCH
MOD_ROWS = DEC_BATCH + 8

OFF_GLR = 2 * BRANCH_WIDTH + 4 * BRANCH_WIDTH + 2 * GLA_DK_TOTAL + 2 * BRANCH_WIDTH
OFF_GATE = OFF_GLR + GLA_RANK
Z_MAIN = OFF_GLR

LANES = 128
ROW_TILE = 128
NORM_TILE = 640
LAST_NORM_TILE = N_ALL // NORM_TILE - 1
SAMPLE_SUB = (N_PROMPT - LAST_NORM_TILE * NORM_TILE) // ROW_TILE
MAX_SPLIT_EXPONENT = 60.0
MM_TM = 4160
MM_ROW_SUB = 832
MM_TM_HALF = 2080
REC_CHUNK = 128
HALF = REC_CHUNK // 2
SAMPLE_NB = 8
GATE_TN = 256
GATE_STEPS = N_BRANCH * D_MODEL // GATE_TN
WIDE_TN = 512
VMEM_LIMIT = 56 * 1024 * 1024


def _cparams(sem):
    return pltpu.CompilerParams(dimension_semantics=sem, vmem_limit_bytes=VMEM_LIMIT)


def _sigmoid(x):
    return 0.5 + 0.5 * jnp.tanh(0.5 * x)


def _silu(x):
    return x * _sigmoid(x)


def _log_sigmoid(x):
    return jnp.minimum(x, 0.0) - jnp.log(1.0 + jnp.exp(-jnp.abs(x)))


def _rms(x, g):
    return x * lax.rsqrt(jnp.mean(x * x, axis=-1, keepdims=True) + EPS) * g


def _mod_kernel(c_ref, w_ref, b_ref, o_ref):
    a = _silu(c_ref[...]).astype(BF16)
    o_ref[...] = jnp.dot(a, w_ref[...].astype(BF16), preferred_element_type=F32) + b_ref[...]


def _modulation(c_all, w_ada, b_ada):
    tn = 1024
    n = 6 * D_MODEL
    return pl.pallas_call(
        _mod_kernel,
        grid=(DEPTH, n // tn),
        in_specs=[
            pl.BlockSpec((MOD_ROWS, D_MODEL), lambda l, j: (0, 0)),
            pl.BlockSpec((None, D_MODEL, tn), lambda l, j: (l, 0, j)),
            pl.BlockSpec((None, 1, tn), lambda l, j: (l, 0, j)),
        ],
        out_specs=pl.BlockSpec((None, MOD_ROWS, tn), lambda l, j: (l, 0, j)),
        out_shape=jax.ShapeDtypeStruct((DEPTH, MOD_ROWS, n), F32),
        compiler_params=_cparams(("parallel", "parallel")),
        name="adaln_mod",
    )(c_all, w_ada, b_ada.reshape(DEPTH, 1, n))


def _mod_specs(l, j):
    return [
        pl.BlockSpec((None, DEC_BATCH, D_MODEL), lambda i: (l, 0, j)),
        pl.BlockSpec((None, 8, D_MODEL), lambda i: (l, DEC_BATCH // 8, j)),
    ]


def _pick_mod(s_ref, p_ref, sub):
    g = pl.program_id(0) * (NORM_TILE // ROW_TILE) + sub
    b = jnp.minimum(g // (SEQ // ROW_TILE), BATCH - 1)
    is_sample = g >= N_PROMPT // ROW_TILE
    return jnp.where(is_sample, s_ref[...], p_ref[pl.ds(b, 1), :])


def _x_rows(x_ref, xs_ref, sub):
    x = x_ref[sub * ROW_TILE:(sub + 1) * ROW_TILE, :]
    if xs_ref is None or sub != SAMPLE_SUB:
        return x
    return jnp.where(pl.program_id(0) == LAST_NORM_TILE, xs_ref[...], x)


def _prenorm_kernel(x_ref, xs_ref, g_ref, sc_s, sc_p, sh_s, sh_p, h_ref):
    for sub in range(NORM_TILE // ROW_TILE):
        rows = slice(sub * ROW_TILE, (sub + 1) * ROW_TILE)
        h = (_rms(_x_rows(x_ref, xs_ref, sub), g_ref[...]) * (1.0 + _pick_mod(sc_s, sc_p, sub))
             + _pick_mod(sh_s, sh_p, sub))
        h_ref[rows, :] = h.astype(BF16)


def _prenorm(x_p, x_s, g, mod, l, j_shift, j_scale):
    row = pl.BlockSpec((NORM_TILE, D_MODEL), lambda i: (i, 0))
    vec = pl.BlockSpec((1, D_MODEL), lambda i: (0, 0))
    smp = pl.BlockSpec((DEC_BATCH, D_MODEL), lambda i: (0, 0))
    return pl.pallas_call(
        _prenorm_kernel,
        grid=(N_ALL // NORM_TILE,),
        in_specs=[row, smp, vec] + _mod_specs(l, j_scale) + _mod_specs(l, j_shift),
        out_specs=row,
        out_shape=jax.ShapeDtypeStruct((N_ALL, D_MODEL), BF16),
        compiler_params=_cparams(("parallel",)),
        name="prenorm",
    )(x_p, x_s, g.reshape(1, D_MODEL), mod, mod, mod, mod)


def _resid_kernel(x_ref, *rest, with_next, split_in, split_out):
    xs_ref = None
    if split_in:
        xs_ref, *rest = rest
    y_ref, gpost_ref, gt_s, gt_p, *rest = rest
    for sub in range(NORM_TILE // ROW_TILE):
        rows = slice(sub * ROW_TILE, (sub + 1) * ROW_TILE)
        x_new = (_x_rows(x_ref, xs_ref, sub)
                 + _pick_mod(gt_s, gt_p, sub) * _rms(y_ref[rows, :], gpost_ref[...]))
        if with_next:
            gpre_ref, sc_s, sc_p, sh_s, sh_p, xo_ref, h_ref = rest
            xo_ref[rows, :] = x_new
            h = (_rms(x_new, gpre_ref[...]) * (1.0 + _pick_mod(sc_s, sc_p, sub))
                 + _pick_mod(sh_s, sh_p, sub))
            h_ref[rows, :] = h.astype(BF16)
        elif split_out:
            xo_ref, xso_ref = rest
            xo_ref[rows, :] = x_new
            if sub == SAMPLE_SUB:
                @pl.when(pl.program_id(0) == LAST_NORM_TILE)
                def _():
                    xso_ref[...] = x_new
        else:
            (xo_ref,) = rest
            xo_ref[rows, :] = x_new


def _resid(x, y, g_post, mod, l_gate, j_gate, nxt=None, x_sample=None, split_out=False):
    row = pl.BlockSpec((NORM_TILE, D_MODEL), lambda i: (i, 0))
    vec = pl.BlockSpec((1, D_MODEL), lambda i: (0, 0))
    smp = pl.BlockSpec((DEC_BATCH, D_MODEL), lambda i: (0, 0))
    in_specs = [row] + ([smp] if x_sample is not None else []) + [row, vec] + _mod_specs(l_gate, j_gate)
    args = [x] + ([x_sample] if x_sample is not None else []) + [y, g_post.reshape(1, D_MODEL), mod, mod]
    out_specs = [row]
    out_shape = [jax.ShapeDtypeStruct((N_ALL, D_MODEL), F32)]
    if nxt is not None:
        assert not split_out
        g_pre, l_n, j_shift, j_scale = nxt
        in_specs += [vec] + _mod_specs(l_n, j_scale) + _mod_specs(l_n, j_shift)
        args += [g_pre.reshape(1, D_MODEL), mod, mod, mod, mod]
        out_specs.append(row)
        out_shape.append(jax.ShapeDtypeStruct((N_ALL, D_MODEL), BF16))
    elif split_out:
        out_specs.append(smp)
        out_shape = [jax.ShapeDtypeStruct((N_PROMPT, D_MODEL), F32),
                     jax.ShapeDtypeStruct((DEC_BATCH, D_MODEL), F32)]
    return pl.pallas_call(
        functools.partial(_resid_kernel, with_next=nxt is not None, split_in=x_sample is not None,
                          split_out=split_out),
        grid=(N_ALL // NORM_TILE,),
        in_specs=in_specs,
        out_specs=out_specs,
        out_shape=out_shape,
        compiler_params=_cparams(("arbitrary",) if split_out else ("parallel",)),
        name="resid_norm",
    )(*args)


def _act_none(x):
    return x


def _act_relu2(x):
    return jnp.square(jnp.maximum(x, 0.0))


def _nt(a, b):
    return lax.dot_general(a, b, (((1,), (1,)), ((), ())), preferred_element_type=F32)


def _mm_kernel(a_ref, w_ref, o_ref, *, nk, act, w_transposed, row_sub):
    w = w_ref[...].astype(BF16)
    if nk > 1:
        @pl.when(pl.program_id(2) == 0)
        def _():
            o_ref[...] = jnp.zeros_like(o_ref)

    for r in range(0, a_ref.shape[0], row_sub):
        rows = slice(r, r + row_sub)
        a = a_ref[rows, :]
        part = _nt(a, w) if w_transposed else jnp.dot(a, w, preferred_element_type=F32)
        if nk == 1:
            o_ref[rows, :] = act(part).astype(o_ref.dtype)
        else:
            o_ref[rows, :] += part


def _matmul(a, w, l, *, n_out, col_off=0, tm=MM_TM, tn=256, tk=None, row_sub=MM_ROW_SUB,
            act=_act_none, out_dtype=F32, w_transposed=False, single_buffer_a=False, name):
    k_dim = a.shape[1]
    tk = k_dim if tk is None else tk
    nk = k_dim // tk
    assert nk == 1 or (act is _act_none and out_dtype == F32)
    if single_buffer_a:
        a_spec = pl.BlockSpec((tm, tk), lambda i, j, k: (i, k), pipeline_mode=pl.Buffered(1))
    else:
        a_spec = pl.BlockSpec((tm, tk), lambda i, j, k: (i, k))
    if w_transposed:
        assert col_off % 8 == 0 and tn % 8 == 0 and tk % LANES == 0
        w_spec = pl.BlockSpec((None, pl.Element(tn), pl.Element(tk)),
                              lambda i, j, k: (l, pl.multiple_of(col_off + j * tn, 8),
                                               pl.multiple_of(k * tk, LANES)))
    else:
        w_spec = pl.BlockSpec((None, tk, tn), lambda i, j, k: (l, k, col_off // tn + j))
    return pl.pallas_call(
        functools.partial(_mm_kernel, nk=nk, act=act, w_transposed=w_transposed, row_sub=row_sub),
        grid=(N_ALL // tm, n_out // tn, nk),
        in_specs=[a_spec, w_spec],
        out_specs=pl.BlockSpec((tm, tn), lambda i, j, k: (i, j)),
        out_shape=jax.ShapeDtypeStruct((N_ALL, n_out), out_dtype),
        compiler_params=_cparams(("parallel", "parallel", "arbitrary")),
        name=name,
    )(a, w)


def _merge_kernel(ya_ref, yb_ref, yc_ref, wa_ref, wb_ref, wc_ref, ga_ref, gb_ref, gc_ref, o_ref):
    wa = wa_ref[...].astype(BF16)
    wb = wb_ref[...].astype(BF16)
    wc = wc_ref[...].astype(BF16)
    sub = MM_TM_HALF // 2
    for r in range(0, MM_TM_HALF, sub):
        rows = slice(r, r + sub)
        merged = ga_ref[rows, :] * jnp.dot(ya_ref[rows, :], wa, preferred_element_type=F32)
        merged = merged + gb_ref[rows, :] * jnp.dot(yb_ref[rows, :], wb, preferred_element_type=F32)
        merged = merged + gc_ref[rows, :] * jnp.dot(yc_ref[rows, :], wc, preferred_element_type=F32)
        o_ref[rows, :] = merged.astype(BF16)


def _merge(ya, yb, yc, w_branch, gates, l):
    tn = 256
    tm = MM_TM_HALF
    nb = D_MODEL // tn
    y_spec = pl.BlockSpec((tm, BRANCH_WIDTH), lambda i, j: (i, 0))

    def w_spec(b):
        return pl.BlockSpec((None, None, BRANCH_WIDTH, tn), lambda i, j: (l, b, 0, j))

    def g_spec(b):
        return pl.BlockSpec((tm, tn), lambda i, j: (i, b * nb + j))

    return pl.pallas_call(
        _merge_kernel,
        grid=(N_ALL // tm, nb),
        in_specs=[y_spec, y_spec, y_spec, w_spec(0), w_spec(1), w_spec(2),
                  g_spec(0), g_spec(1), g_spec(2)],
        out_specs=pl.BlockSpec((tm, tn), lambda i, j: (i, j)),
        out_shape=jax.ShapeDtypeStruct((N_ALL, D_MODEL), BF16),
        compiler_params=_cparams(("parallel", "parallel")),
        name="branch_merge",
    )(ya, yb, yc, w_branch, w_branch, w_branch, gates, gates, gates)


def _sgu_uv(zu_ref, zv_ref, lg_ref, lb_ref):
    u = jax.nn.gelu(zu_ref[...], approximate=True)
    v = jax.nn.gelu(zv_ref[...], approximate=True)
    mu = jnp.mean(v, axis=-1, keepdims=True)
    vc = v - mu
    v = vc * lax.rsqrt(jnp.mean(vc * vc, axis=-1, keepdims=True) + EPS) * lg_ref[...] + lb_ref[...]
    return u, v


def _sgu_prompt_chunk(zu_ref, zv_ref, lg_ref, lb_ref, ws_ref, bst_ref, y_ref):
    u, v = _sgu_uv(zu_ref, zv_ref, lg_ref, lb_ref)
    n_idx = lax.broadcasted_iota(jnp.int32, (SGU_CHUNK, SGU_CHUNK), 0)
    m_idx = lax.broadcasted_iota(jnp.int32, (SGU_CHUNK, SGU_CHUNK), 1)
    causal = m_idx <= n_idx
    for g in range(SGU_GROUPS):
        sl = slice(g * LANES, (g + 1) * LANES)
        wm = jnp.where(causal, ws_ref[g], 0.0).astype(BF16)
        s = jnp.dot(wm, v[:, sl].astype(BF16), preferred_element_type=F32) + bst_ref[:, g:g + 1]
        y_ref[:, sl] = (u[:, sl] * s).astype(BF16)


def _sgu_sample_kernel(zu_ref, zv_ref, lg_ref, lb_ref, ws_ref, bst_ref, ya_any, y_ref, vs_ref):
    del ya_any
    u, v = _sgu_uv(zu_ref, zv_ref, lg_ref, lb_ref)
    vs_ref[...] = v
    for g in range(SGU_GROUPS):
        sl = slice(g * LANES, (g + 1) * LANES)
        s = v[:, sl] * ws_ref[g][0:1, 0:1] + bst_ref[0:1, g:g + 1]
        y_ref[:, sl] = (u[:, sl] * s).astype(BF16)


def _sgu_sample(z, y_a, ln_g, ln_b, w_s, b_s_t, l):
    blk = N_PROMPT // ROW_TILE
    vec = pl.BlockSpec((1, BRANCH_WIDTH), lambda i: (0, 0))
    tile = pl.BlockSpec((ROW_TILE, BRANCH_WIDTH), lambda i: (blk, 0))
    return pl.pallas_call(
        _sgu_sample_kernel,
        grid=(1,),
        in_specs=[
            tile,
            pl.BlockSpec((ROW_TILE, BRANCH_WIDTH), lambda i: (blk, 1)),
            vec, vec,
            pl.BlockSpec((None, SGU_GROUPS, SGU_CHUNK, SGU_CHUNK), lambda i: (l, 0, 0, 0)),
            pl.BlockSpec((None, SGU_CHUNK, SGU_GROUPS), lambda i: (l, 0, 0)),
            pl.BlockSpec(memory_space=pl.ANY),
        ],
        out_specs=[tile, pl.BlockSpec((DEC_BATCH, BRANCH_WIDTH), lambda i: (0, 0))],
        out_shape=[
            jax.ShapeDtypeStruct((N_ALL, BRANCH_WIDTH), BF16),
            jax.ShapeDtypeStruct((DEC_BATCH, BRANCH_WIDTH), F32),
        ],
        input_output_aliases={6: 0},
        compiler_params=_cparams(("arbitrary",)),
        name="sgu_sample",
    )(z, z, ln_g[l].reshape(1, BRANCH_WIDTH), ln_b[l].reshape(1, BRANCH_WIDTH), w_s, b_s_t, y_a)


def _hgrn_lower_bound(lb_ref, l):
    raw = lb_ref[...]
    e = jnp.exp(raw - jnp.max(raw, axis=0, keepdims=True))
    p = e / jnp.sum(e, axis=0, keepdims=True)
    acc = p[0:1, :]
    for j in range(1, l + 1):
        acc = acc + p[j:j + 1, :]
    return jnp.maximum(acc - p[0:1, :], 0.0)


def _hgrn_prep(zq, zf, lb_ref, l):
    e = jnp.exp(-jnp.abs(zf))
    ope = 1.0 + e
    sig_neg = jnp.where(zf >= 0.0, e, 1.0) / ope
    log_sig = jnp.minimum(zf, 0.0) - jnp.log(ope)
    if l == 0:
        return _silu(zq), sig_neg, log_sig
    lb = _hgrn_lower_bound(lb_ref, l)
    a = jnp.log(lb)
    c = jnp.log1p(-lb) + log_sig
    log_f = jnp.maximum(a, c) + jnp.log(1.0 + jnp.exp(-jnp.abs(a - c)))
    return _silu(zq), (1.0 - lb) * sig_neg, log_f


def _gla_prep(zq, zk, glr, wup_ref, bup_ref):
    x = jnp.dot(glr[:, :GLA_RANK].astype(BF16), wup_ref[...].astype(BF16),
                preferred_element_type=F32) + bup_ref[...]
    log_a = _log_sigmoid(x) / GLA_TAU
    return zq, zk * (GLA_DK ** -0.5), log_a


def _lane_bcast_col(row):
    return jnp.broadcast_to(row, (LANES, LANES)).T


def _head_out(o, g_row, gate):
    o = o * lax.rsqrt(jnp.mean(o * o, axis=-1, keepdims=True) + EPS) * g_row
    return o * _silu(gate)


class _RecRefs(NamedTuple):
    gnorm: object
    y: object
    s_out: object
    s: object
    s_prev: object
    q: object
    b: object
    o_intra: object


def _rec_scratch(heads, dk, dv):
    return [
        pltpu.VMEM((heads, dk, dv), F32),
        pltpu.VMEM((heads, dk, dv), F32),
        pltpu.VMEM((REC_CHUNK, heads * dk), F32),
        pltpu.VMEM((REC_CHUNK, heads * dk), F32),
        pltpu.VMEM((heads, REC_CHUNK, dv), F32),
    ]


def _finish_head(h, o_intra, q_dec, k_dec, b_last, vh, gate, s_old, refs, dv):
    vs = slice(h * dv, (h + 1) * dv)
    o = o_intra + jnp.dot(q_dec.astype(BF16), s_old.astype(BF16), preferred_element_type=F32)
    refs.y[:, vs] = _head_out(o, refs.gnorm[...], gate[:, vs]).astype(BF16)
    upd = jnp.dot(k_dec.T.astype(BF16), vh, preferred_element_type=F32)
    decay = _lane_bcast_col(jnp.exp(b_last))
    for j in range(dv // LANES):
        ls = slice(j * LANES, (j + 1) * LANES)
        refs.s[h, :, ls] = decay * s_old[:, ls] + upd[:, ls]


def _chunk_start(c, refs):
    @pl.when(c == 0)
    def _():
        refs.s[...] = jnp.zeros_like(refs.s)


def _chunk_main(q, k, v, log_f, gate, refs, *, heads, dk, dv):
    t_idx = lax.broadcasted_iota(jnp.int32, (REC_CHUNK, REC_CHUNK), 0)
    s_idx = lax.broadcasted_iota(jnp.int32, (REC_CHUNK, REC_CHUNK), 1)
    tril = jnp.where(s_idx <= t_idx, 1.0, 0.0).astype(F32)
    b_all = jnp.dot(tril, log_f, precision=lax.Precision.HIGHEST, preferred_element_type=F32)
    v16 = v.astype(BF16)

    mid = HALF // 2 - 1

    def drop(lo, hi):
        return b_all[lo:lo + 1] - b_all[hi:hi + 1]

    worst = jnp.maximum(jnp.maximum(drop(0, mid), drop(mid, HALF - 1)),
                        jnp.maximum(drop(HALF, HALF + mid), drop(HALF + mid, REC_CHUNK - 1)))
    unsafe = jnp.max(worst) > MAX_SPLIT_EXPONENT

    causal_half = (lax.broadcasted_iota(jnp.int32, (HALF, HALF), 1)
                   <= lax.broadcasted_iota(jnp.int32, (HALF, HALF), 0))
    for h in range(heads):
        ks = slice(h * dk, (h + 1) * dk)
        qh, kh, bh = q[:, ks], k[:, ks], b_all[:, ks]
        vh = v16[:, h * dv:(h + 1) * dv]
        b_a, b_b = bh[:HALF], bh[HALF:]
        v_a, v_b = vh[:HALF], vh[HALF:]
        r_a = b_a[mid:mid + 1]
        r_b = b_b[mid:mid + 1]
        r_m = b_a[HALF - 1:HALF]
        b_last = b_b[HALF - 1:HALF]
        q_a = qh[:HALF] * jnp.exp(b_a - r_a)
        q_b = qh[HALF:] * jnp.exp(b_b - r_b)
        k_a = kh[:HALF] * jnp.exp(r_a - b_a)
        k_b = kh[HALF:] * jnp.exp(r_b - b_b)
        att_aa = jnp.where(causal_half, _nt(q_a.astype(BF16), k_a.astype(BF16)), 0.0).astype(BF16)
        att_bb = jnp.where(causal_half, _nt(q_b.astype(BF16), k_b.astype(BF16)), 0.0).astype(BF16)
        att_ba = _nt((q_b * jnp.exp(r_b - r_m)).astype(BF16),
                     (k_a * jnp.exp(r_m - r_a)).astype(BF16)).astype(BF16)
        o_a = jnp.dot(att_aa, v_a, preferred_element_type=F32)
        o_b = (jnp.dot(att_ba, v_a, preferred_element_type=F32)
               + jnp.dot(att_bb, v_b, preferred_element_type=F32))
        q_dec = jnp.concatenate([q_a * jnp.exp(r_a), q_b * jnp.exp(r_b)], axis=0)
        k_dec = jnp.concatenate([k_a * jnp.exp(b_last - r_a), k_b * jnp.exp(b_last - r_b)], axis=0)
        s_old = refs.s[h]
        refs.s_prev[h] = s_old
        _finish_head(h, jnp.concatenate([o_a, o_b], axis=0), q_dec, k_dec, b_last, vh, gate, s_old,
                     refs, dv)
    return unsafe, b_all, v16


def _chunk_redo(unsafe, q, k, v, b_all, v16, gate, refs, *, heads, dk, dv):
    @pl.when(unsafe)
    def _():
        refs.q[...] = q
        refs.b[...] = b_all
        s_pos = lax.broadcasted_iota(jnp.int32, (REC_CHUNK, 1), 0)
        for h in range(heads):
            ks = slice(h * dk, (h + 1) * dk)
            vs = slice(h * dv, (h + 1) * dv)
            qh, kh, bh = q[:, ks], k[:, ks], b_all[:, ks]
            vh = v[:, vs]

            def row(t, carry):
                b_t = refs.b[pl.ds(t, 1), :][:, ks]
                q_t = refs.q[pl.ds(t, 1), :][:, ks]
                p = jnp.exp(jnp.minimum(b_t - bh, 0.0)) * kh * q_t
                w = jnp.where(s_pos <= t, jnp.sum(p, axis=1, keepdims=True), 0.0)
                refs.o_intra[h, pl.ds(t, 1), :] = jnp.sum(w * vh, axis=0, keepdims=True)
                return carry

            lax.fori_loop(0, REC_CHUNK, row, 0)
            b_last = bh[REC_CHUNK - 1:REC_CHUNK]
            _finish_head(h, refs.o_intra[h], qh * jnp.exp(bh), kh * jnp.exp(b_last - bh), b_last,
                         v16[:, vs], gate, refs.s_prev[h], refs, dv)


def _chunk_end(last, refs):
    @pl.when(last)
    def _():
        refs.s_out[...] = refs.s[...]


def _prompt_mixers_kernel(zu_ref, zv_ref, lng_ref, lnb_ref, ws_ref, bst_ref,
                          hq_ref, hf_ref, hi_ref, hg_ref, lb_ref, hgn_ref,
                          gq_ref, gk_ref, gv_ref, gr_ref, glr_ref, wup_ref, bup_ref, glan_ref,
                          ya_ref, yb_ref, shg_ref, yc_ref, sgla_ref, *scratch, l):
    hg = _RecRefs(hgn_ref, yb_ref, shg_ref, *scratch[:5])
    gla = _RecRefs(glan_ref, yc_ref, sgla_ref, *scratch[5:])
    hg_dims = dict(heads=HG_HEADS, dk=HG_DK, dv=HG_DV)
    gla_dims = dict(heads=GLA_HEADS, dk=GLA_DK, dv=GLA_DV)
    c = pl.program_id(1)
    last = c == pl.num_programs(1) - 1
    _chunk_start(c, hg)
    _chunk_start(c, gla)

    _sgu_prompt_chunk(zu_ref, zv_ref, lng_ref, lnb_ref, ws_ref, bst_ref, ya_ref)

    hq, hk, hlf = _hgrn_prep(hq_ref[...], hf_ref[...], lb_ref, l)
    hv, hgate = hi_ref[...], hg_ref[...]
    h_unsafe, h_b, h_v16 = _chunk_main(hq, hk, hv, hlf, hgate, hg, **hg_dims)
    gq, gk, gla_la = _gla_prep(gq_ref[...], gk_ref[...], glr_ref[...], wup_ref, bup_ref)
    gv, ggate = gv_ref[...], gr_ref[...]
    g_unsafe, g_b, g_v16 = _chunk_main(gq, gk, gv, gla_la, ggate, gla, **gla_dims)

    _chunk_redo(h_unsafe, hq, hk, hv, h_b, h_v16, hgate, hg, **hg_dims)
    _chunk_redo(g_unsafe, gq, gk, gv, g_b, g_v16, ggate, gla, **gla_dims)
    _chunk_end(last, hg)
    _chunk_end(last, gla)


def _prompt_mixers(z, z_glr, ln_g, ln_b, w_s, b_s_t, hg_lb, hg_norm_g, w_up, b_up, gla_norm_g, l):
    n_c = SEQ // REC_CHUNK
    w = BRANCH_WIDTH
    kw = GLA_DK_TOTAL
    assert REC_CHUNK == SGU_CHUNK

    def zcol(width, block):
        return pl.BlockSpec((REC_CHUNK, width), lambda b, c: (b * n_c + c, block))

    def const(shape):
        return pl.BlockSpec(shape, lambda b, c: (0,) * len(shape))

    def state_spec(heads, dk, dv):
        return pl.BlockSpec((None, heads, dk, dv), lambda b, c: (b, 0, 0, 0))

    y_spec = pl.BlockSpec((REC_CHUNK, w), lambda b, c: (b * n_c + c, 0))
    in_specs = [
        zcol(w, 0), zcol(w, 1), const((1, w)), const((1, w)),
        pl.BlockSpec((None, SGU_GROUPS, SGU_CHUNK, SGU_CHUNK), lambda b, c: (l, 0, 0, 0)),
        pl.BlockSpec((None, SGU_CHUNK, SGU_GROUPS), lambda b, c: (l, 0, 0)),
        zcol(w, 2), zcol(w, 3), zcol(w, 4), zcol(w, 5), const((DEPTH, w)), const((1, HG_DV)),
        zcol(kw, 6 * w // kw), zcol(kw, 6 * w // kw + 1), zcol(w, 7), zcol(w, 8), zcol(LANES, 0),
        pl.BlockSpec((None, GLA_RANK, kw), lambda b, c: (l, 0, 0)), const((1, kw)), const((1, GLA_DV)),
    ]
    args = [z, z, ln_g[l].reshape(1, w), ln_b[l].reshape(1, w), w_s, b_s_t,
            z, z, z, z, hg_lb, hg_norm_g[l].reshape(1, HG_DV),
            z, z, z, z, z_glr, w_up, b_up[l].reshape(1, kw), gla_norm_g[l].reshape(1, GLA_DV)]
    return pl.pallas_call(
        functools.partial(_prompt_mixers_kernel, l=l),
        grid=(BATCH, n_c),
        in_specs=in_specs,
        out_specs=[y_spec, y_spec, state_spec(HG_HEADS, HG_DK, HG_DV),
                   y_spec, state_spec(GLA_HEADS, GLA_DK, GLA_DV)],
        out_shape=[
            jax.ShapeDtypeStruct((N_ALL, w), BF16),
            jax.ShapeDtypeStruct((N_ALL, w), BF16),
            jax.ShapeDtypeStruct((BATCH, HG_HEADS, HG_DK, HG_DV), F32),
            jax.ShapeDtypeStruct((N_ALL, w), BF16),
            jax.ShapeDtypeStruct((BATCH, GLA_HEADS, GLA_DK, GLA_DV), F32),
        ],
        scratch_shapes=_rec_scratch(HG_HEADS, HG_DK, HG_DV) + _rec_scratch(GLA_HEADS, GLA_DK, GLA_DV),
        compiler_params=_cparams(("parallel", "arbitrary")),
        name="prompt_mixers",
    )(*args)


def _sample_step(q, k, v, log_f, gate, gnorm_ref, s_ref, so_ref, i, *, heads, dk, dv):
    outs = []
    for h in range(heads):
        ks = slice(h * dk, (h + 1) * dk)
        f_col = _lane_bcast_col(jnp.exp(log_f[:, ks]))
        k_col = _lane_bcast_col(k[:, ks])
        q16 = jnp.broadcast_to(q[:, ks], (16, dk)).astype(BF16)
        parts = []
        for j in range(dv // LANES):
            ls = slice(j * LANES, (j + 1) * LANES)
            v_row = v[:, h * dv + j * LANES:h * dv + (j + 1) * LANES]
            s_new = f_col * s_ref[i, h, :, ls] + k_col * v_row
            so_ref[i, h, :, ls] = s_new
            parts.append(jnp.dot(q16, s_new.astype(BF16), preferred_element_type=F32)[0:1])
        o = parts[0] if len(parts) == 1 else jnp.concatenate(parts, axis=1)
        outs.append(_head_out(o, gnorm_ref[...], gate[:, h * dv:(h + 1) * dv]))
    return jnp.concatenate(outs, axis=1)


def _flush_sample_rows(y_ref, yacc_ref):
    @pl.when(pl.program_id(0) == pl.num_programs(0) - 1)
    def _():
        y_ref[...] = yacc_ref[...].astype(BF16)


def _gate_tile(h_ref, w_ref, g_ref):
    w = w_ref[...].astype(BF16)
    for r in range(0, MM_TM, MM_ROW_SUB):
        rows = slice(r, r + MM_ROW_SUB)
        g_ref[rows, :] = _sigmoid(_nt(h_ref[rows, :], w)).astype(BF16)


def _unit_index(step):
    return step * (DEC_BATCH // SAMPLE_NB) // GATE_STEPS


def _hgrn_sample_kernel(h_ref, w_ref, zq_ref, zf_ref, zi_ref, zg_ref, lb_ref, gnorm_ref, s_ref, *rest, l):
    g_ref, y_ref, so_ref, yacc_ref = rest[-4:]
    _gate_tile(h_ref, w_ref, g_ref)
    base = _unit_index(pl.program_id(0)) * SAMPLE_NB
    q, k, log_f = _hgrn_prep(zq_ref[...], zf_ref[...], lb_ref, l)
    v, gate = zi_ref[...], zg_ref[...]
    for i in range(SAMPLE_NB):
        row = slice(i, i + 1)
        yacc_ref[pl.ds(base + i, 1), :] = _sample_step(
            q[row], k[row], v[row], log_f[row], gate[row], gnorm_ref, s_ref, so_ref, i,
            heads=HG_HEADS, dk=HG_DK, dv=HG_DV)
    _flush_sample_rows(y_ref, yacc_ref)


def _gla_sample_kernel(h_ref, w_ref, zq_ref, zk_ref, zv_ref, zr_ref, glr_ref, wup_ref, bup_ref,
                       gnorm_ref, s_ref, *rest):
    g_ref, y_ref, so_ref, yacc_ref = rest[-4:]
    _gate_tile(h_ref, w_ref, g_ref)
    base = _unit_index(pl.program_id(0)) * SAMPLE_NB
    glr = jnp.concatenate([glr_ref[...], jnp.zeros((16 - SAMPLE_NB, LANES), F32)], axis=0)
    q, k, log_a = _gla_prep(zq_ref[...], zk_ref[...], glr, wup_ref, bup_ref)
    v, gate = zv_ref[...], zr_ref[...]
    for i in range(SAMPLE_NB):
        row = slice(i, i + 1)
        yacc_ref[pl.ds(base + i, 1), :] = _sample_step(
            q[row], k[row], v[row], log_a[row], gate[row], gnorm_ref, s_ref, so_ref, i,
            heads=GLA_HEADS, dk=GLA_DK, dv=GLA_DV)
    _flush_sample_rows(y_ref, yacc_ref)


def _zrow(width, block):
    return pl.BlockSpec((SAMPLE_NB, width), lambda j: (N_PROMPT // SAMPLE_NB + _unit_index(j), block))


def _gate_sample_call(kernel, in_specs, args, h, w_in_t, state, y_all, new_state, gates, row_tile, l,
                      heads, dk, dv, name):
    tn = GATE_TN
    n_steps = GATE_STEPS
    assert n_steps >= DEC_BATCH // SAMPLE_NB
    h_spec = pl.BlockSpec((MM_TM, D_MODEL), lambda j: (row_tile, 0), pipeline_mode=pl.Buffered(1))
    w_spec = pl.BlockSpec((None, pl.Element(tn), pl.Element(D_MODEL)),
                          lambda j: (l, pl.multiple_of(OFF_GATE + j * tn, 8), 0))
    st = pl.BlockSpec((None, SAMPLE_NB, heads, dk, dv), lambda j: (l, _unit_index(j), 0, 0, 0))
    any_spec = pl.BlockSpec(memory_space=pl.ANY)
    in_specs = [h_spec, w_spec] + in_specs + [st, any_spec]
    args = [h, w_in_t] + list(args) + [state, y_all]
    aliases = {len(args) - 1: 1}
    for buf, out_idx in ((new_state, 2), (gates, 0)):
        if buf is not None:
            aliases[len(args)] = out_idx
            in_specs.append(any_spec)
            args.append(buf)
    return pl.pallas_call(
        kernel,
        grid=(n_steps,),
        in_specs=in_specs,
        out_specs=[
            pl.BlockSpec((MM_TM, tn), lambda j: (row_tile, j)),
            pl.BlockSpec((DEC_BATCH, BRANCH_WIDTH), lambda j: (N_PROMPT // DEC_BATCH, 0)),
            st,
        ],
        out_shape=[
            jax.ShapeDtypeStruct((N_ALL, N_BRANCH * D_MODEL), BF16),
            jax.ShapeDtypeStruct((N_ALL, BRANCH_WIDTH), BF16),
            jax.ShapeDtypeStruct((DEPTH, DEC_BATCH, heads, dk, dv), F32),
        ],
        scratch_shapes=[pltpu.VMEM((DEC_BATCH, BRANCH_WIDTH), F32)],
        input_output_aliases=aliases,
        compiler_params=_cparams(("arbitrary",)),
        name=name,
    )(*args)


def _hgrn_sample(h, w_in_t, z, hg_lb, g_norm, state, y_all, new_state, gates, row_tile, l):
    w = BRANCH_WIDTH
    in_specs = [_zrow(w, 2), _zrow(w, 3), _zrow(w, 4), _zrow(w, 5),
                pl.BlockSpec((DEPTH, w), lambda j: (0, 0)),
                pl.BlockSpec((1, HG_DV), lambda j: (0, 0))]
    args = [z, z, z, z, hg_lb, g_norm[l].reshape(1, HG_DV)]
    return _gate_sample_call(functools.partial(_hgrn_sample_kernel, l=l), in_specs, args, h, w_in_t,
                             state, y_all, new_state, gates, row_tile, l,
                             HG_HEADS, HG_DK, HG_DV, "gate_hgrn_sample")


def _gla_sample(h, w_in_t, z, z_glr, w_up, b_up, g_norm, state, y_all, new_state, gates, row_tile, l):
    w = BRANCH_WIDTH
    kw = GLA_DK_TOTAL
    in_specs = [_zrow(kw, 6 * w // kw), _zrow(kw, 6 * w // kw + 1), _zrow(w, 7), _zrow(w, 8),
                _zrow(LANES, 0),
                pl.BlockSpec((None, GLA_RANK, kw), lambda j: (l, 0, 0)),
                pl.BlockSpec((1, kw), lambda j: (0, 0)),
                pl.BlockSpec((1, GLA_DV), lambda j: (0, 0))]
    args = [z, z, z, z, z_glr, w_up, b_up[l].reshape(1, kw), g_norm[l].reshape(1, GLA_DV)]
    return _gate_sample_call(_gla_sample_kernel, in_specs, args, h, w_in_t, state, y_all, new_state,
                             gates, row_tile, l, GLA_HEADS, GLA_DK, GLA_DV, "gate_gla_sample")


def kernel(x_prompt, x_sample, state_hgrn, state_gla, c_prompt, c_sample, w_ada, b_ada, g_pre_mix, g_post_mix, g_pre_mlp, g_post_mlp, w_in, sgu_ln_g, sgu_ln_b, sgu_w_s, sgu_b_s, hg_lb, hg_norm_g, gla_w_up, gla_b_up, gla_norm_g, w_branch, w_out, w_mlp_up, w_mlp_down):
    x = x_prompt.reshape(N_PROMPT, D_MODEL)
    x_s = x_sample.reshape(DEC_BATCH, D_MODEL)
    c_all = jnp.concatenate([c_sample, c_prompt, jnp.zeros((MOD_ROWS - DEC_BATCH - BATCH, D_MODEL), F32)], axis=0)
    mod = _modulation(c_all, w_ada, b_ada)
    w_in_t = jnp.swapaxes(w_in, 1, 2)
    b_s_t = jnp.swapaxes(sgu_b_s, 1, 2)

    hg_p, gla_p, v_rows = [], [], []
    hg_s = gla_s = None
    h = _prenorm(x, x_s, g_pre_mix[0], mod, 0, 0, 1)
    for l in range(DEPTH):
        z = _matmul(h, w_in_t, l, n_out=Z_MAIN, tn=WIDE_TN, w_transposed=True, single_buffer_a=True,
                    name="in_proj")
        z_glr = _matmul(h, w_in_t, l, n_out=LANES, col_off=OFF_GLR, tn=LANES, w_transposed=True,
                        name="in_proj_lowrank")

        y_a, y_b, s_hg_p, y_c, s_gla_p = _prompt_mixers(
            z, z_glr, sgu_ln_g, sgu_ln_b, sgu_w_s, b_s_t, hg_lb, hg_norm_g, gla_w_up, gla_b_up,
            gla_norm_g, l)
        y_a, v_s = _sgu_sample(z, y_a, sgu_ln_g, sgu_ln_b, sgu_w_s, b_s_t, l)
        gates, y_b, hg_s = _hgrn_sample(h, w_in_t, z, hg_lb, hg_norm_g, state_hgrn, y_b, hg_s,
                                        None, 0, l)
        gates, y_c, gla_s = _gla_sample(h, w_in_t, z, z_glr, gla_w_up, gla_b_up, gla_norm_g,
                                        state_gla, y_c, gla_s, gates, 1, l)

        merged = _merge(y_a, y_b, y_c, w_branch, gates, l)
        out = _matmul(merged, w_out, l, n_out=D_MODEL, name="out_proj")
        x, h2 = _resid(x, out, g_post_mix[l], mod, l, 2, nxt=(g_pre_mlp[l], l, 3, 4),
                       x_sample=x_s if l == 0 else None)

        up = _matmul(h2, w_mlp_up, l, n_out=D_FF, tn=WIDE_TN, act=_act_relu2, out_dtype=BF16,
                     name="mlp_up")
        y2 = _matmul(up, w_mlp_down, l, n_out=D_MODEL, tm=MM_TM_HALF, tn=1024, tk=2048,
                     row_sub=MM_TM_HALF // 4, name="mlp_down")
        if l + 1 < DEPTH:
            x, h = _resid(x, y2, g_post_mlp[l], mod, l, 5, nxt=(g_pre_mix[l + 1], l + 1, 0, 1))
        else:
            y_prompt, y_sample = _resid(x, y2, g_post_mlp[l], mod, l, 5, split_out=True)

        hg_p.append(s_hg_p)
        gla_p.append(s_gla_p)
        v_rows.append(v_s.reshape(DEC_BATCH, 1, BRANCH_WIDTH))

    y_prompt = y_prompt.reshape(BATCH, SEQ, D_MODEL)
    y_sample = y_sample.reshape(DEC_BATCH, 1, D_MODEL)
    return (y_prompt, y_sample, jnp.stack(hg_p), jnp.stack(gla_p), hg_s, gla_s, jnp.stack(v_rows))
```

```python
import functools
from typing import NamedTuple

import jax
import jax.numpy as jnp
from jax import lax
from jax.experimental import pallas as pl
from jax.experimental.pallas import tpu as pltpu

F32 = jnp.float32
BF16 = jnp.bfloat16

D_MODEL = 2048
BATCH = 4
SEQ = 2048
DEPTH = 4
DEC_BATCH = 128
BRANCH_WIDTH = D_MODEL // 2
N_BRANCH = 3
SGU_GROUPS = 8
SGU_CHUNK = 128
HG_HEADS = 8
HG_DK = BRANCH_WIDTH // HG_HEADS
HG_DV = BRANCH_WIDTH // HG_HEADS
GLA_HEADS = 4
GLA_DK_TOTAL = BRANCH_WIDTH // 2
GLA_DK = GLA_DK_TOTAL // GLA_HEADS
GLA_DV = BRANCH_WIDTH // GLA_HEADS
GLA_RANK = 16
GLA_TAU = 16.0
D_FF = 4 * D_MODEL
EPS = 1e-6

N_PROMPT = BATCH * SEQ
N_ALL = N_PROMPT + DEC_BATCH
MOD_ROWS = DEC_BATCH + 8

OFF_GLR = 2 * BRANCH_WIDTH + 4 * BRANCH_WIDTH + 2 * GLA_DK_TOTAL + 2 * BRANCH_WIDTH
OFF_GATE = OFF_GLR + GLA_RANK
Z_MAIN = OFF_GLR

LANES = 128
ROW_TILE = 128
NORM_TILE = 640
LAST_NORM_TILE = N_ALL // NORM_TILE - 1
SAMPLE_SUB = (N_PROMPT - LAST_NORM_TILE * NORM_TILE) // ROW_TILE
NORM_PART = 16
MAX_SPLIT_EXPONENT = 60.0
MM_TM = 4160
MM_ROW_SUB = 832
MM_TM_HALF = 2080
REC_CHUNK = 128
HALF = REC_CHUNK // 2
SAMPLE_NB = 8
GATE_TN = 256
WIDE_TN = 512
VMEM_LIMIT = 56 * 1024 * 1024


def _cparams(sem):
    return pltpu.CompilerParams(dimension_semantics=sem, vmem_limit_bytes=VMEM_LIMIT)


def _sigmoid(x):
    return 0.5 + 0.5 * jnp.tanh(0.5 * x)


def _silu(x):
    return x * _sigmoid(x)


def _log_sigmoid(x):
    return jnp.minimum(x, 0.0) - jnp.log(1.0 + jnp.exp(-jnp.abs(x)))


def _rms(x, g):
    return x * lax.rsqrt(jnp.mean(x * x, axis=-1, keepdims=True) + EPS) * g


def _mod_kernel(c_ref, w_ref, b_ref, o_ref):
    a = _silu(c_ref[...]).astype(BF16)
    o_ref[...] = jnp.dot(a, w_ref[...].astype(BF16), preferred_element_type=F32) + b_ref[...]


def _modulation(c_all, w_ada, b_ada):
    tn = 1024
    n = 6 * D_MODEL
    return pl.pallas_call(
        _mod_kernel,
        grid=(DEPTH, n // tn),
        in_specs=[
            pl.BlockSpec((MOD_ROWS, D_MODEL), lambda l, j: (0, 0)),
            pl.BlockSpec((None, D_MODEL, tn), lambda l, j: (l, 0, j)),
            pl.BlockSpec((None, 1, tn), lambda l, j: (l, 0, j)),
        ],
        out_specs=pl.BlockSpec((None, MOD_ROWS, tn), lambda l, j: (l, 0, j)),
        out_shape=jax.ShapeDtypeStruct((DEPTH, MOD_ROWS, n), F32),
        compiler_params=_cparams(("parallel", "parallel")),
        name="adaln_mod",
    )(c_all, w_ada, b_ada.reshape(DEPTH, 1, n))


def _mod_specs(l, j):
    return [
        pl.BlockSpec((None, DEC_BATCH, D_MODEL), lambda i: (l, 0, j)),
        pl.BlockSpec((None, 8, D_MODEL), lambda i: (l, DEC_BATCH // 8, j)),
    ]


def _row_parts():
    for sub in range(NORM_TILE // ROW_TILE):
        for r in range(0, ROW_TILE, NORM_PART):
            yield sub, slice(r, r + NORM_PART), slice(sub * ROW_TILE + r, sub * ROW_TILE + r + NORM_PART)


def _pick_mod(s_ref, p_ref, sub, local):
    g = pl.program_id(0) * (NORM_TILE // ROW_TILE) + sub
    b = jnp.minimum(g // (SEQ // ROW_TILE), BATCH - 1)
    is_sample = g >= N_PROMPT // ROW_TILE
    return jnp.where(is_sample, s_ref[local, :], p_ref[pl.ds(b, 1), :])


def _x_rows(x_ref, xs_ref, sub, local, rows):
    x = x_ref[rows, :]
    if xs_ref is None or sub != SAMPLE_SUB:
        return x
    return jnp.where(pl.program_id(0) == LAST_NORM_TILE, xs_ref[local, :], x)


def _prenorm_kernel(x_ref, xs_ref, g_ref, sc_s, sc_p, sh_s, sh_p, h_ref):
    for sub, local, rows in _row_parts():
        h = (_rms(_x_rows(x_ref, xs_ref, sub, local, rows), g_ref[...])
             * (1.0 + _pick_mod(sc_s, sc_p, sub, local)) + _pick_mod(sh_s, sh_p, sub, local))
        h_ref[rows, :] = h.astype(BF16)


def _prenorm(x_p, x_s, g, mod, l, j_shift, j_scale):
    row = pl.BlockSpec((NORM_TILE, D_MODEL), lambda i: (i, 0))
    vec = pl.BlockSpec((1, D_MODEL), lambda i: (0, 0))
    smp = pl.BlockSpec((DEC_BATCH, D_MODEL), lambda i: (0, 0))
    return pl.pallas_call(
        _prenorm_kernel,
        grid=(N_ALL // NORM_TILE,),
        in_specs=[row, smp, vec] + _mod_specs(l, j_scale) + _mod_specs(l, j_shift),
        out_specs=row,
        out_shape=jax.ShapeDtypeStruct((N_ALL, D_MODEL), BF16),
        compiler_params=_cparams(("parallel",)),
        name="prenorm",
    )(x_p, x_s, g.reshape(1, D_MODEL), mod, mod, mod, mod)


def _resid_kernel(x_ref, *rest, with_next, split_in, split_out):
    xs_ref = None
    if split_in:
        xs_ref, *rest = rest
    y_ref, gpost_ref, gt_s, gt_p, *rest = rest
    for sub, local, rows in _row_parts():
        x_new = (_x_rows(x_ref, xs_ref, sub, local, rows)
                 + _pick_mod(gt_s, gt_p, sub, local) * _rms(y_ref[rows, :], gpost_ref[...]))
        if with_next:
            gpre_ref, sc_s, sc_p, sh_s, sh_p, xo_ref, h_ref = rest
            xo_ref[rows, :] = x_new
            h = (_rms(x_new, gpre_ref[...]) * (1.0 + _pick_mod(sc_s, sc_p, sub, local))
                 + _pick_mod(sh_s, sh_p, sub, local))
            h_ref[rows, :] = h.astype(BF16)
        elif split_out:
            xo_ref, xso_ref = rest
            xo_ref[rows, :] = x_new
            if sub == SAMPLE_SUB:
                @pl.when(pl.program_id(0) == LAST_NORM_TILE)
                def _(x_new=x_new, local=local):
                    xso_ref[local, :] = x_new
        else:
            (xo_ref,) = rest
            xo_ref[rows, :] = x_new


def _resid(x, y, g_post, mod, l_gate, j_gate, nxt=None, x_sample=None, split_out=False):
    row = pl.BlockSpec((NORM_TILE, D_MODEL), lambda i: (i, 0))
    vec = pl.BlockSpec((1, D_MODEL), lambda i: (0, 0))
    smp = pl.BlockSpec((DEC_BATCH, D_MODEL), lambda i: (0, 0))
    in_specs = [row] + ([smp] if x_sample is not None else []) + [row, vec] + _mod_specs(l_gate, j_gate)
    args = [x] + ([x_sample] if x_sample is not None else []) + [y, g_post.reshape(1, D_MODEL), mod, mod]
    out_specs = [row]
    out_shape = [jax.ShapeDtypeStruct((N_ALL, D_MODEL), F32)]
    if nxt is not None:
        assert not split_out
        g_pre, l_n, j_shift, j_scale = nxt
        in_specs += [vec] + _mod_specs(l_n, j_scale) + _mod_specs(l_n, j_shift)
        args += [g_pre.reshape(1, D_MODEL), mod, mod, mod, mod]
        out_specs.append(row)
        out_shape.append(jax.ShapeDtypeStruct((N_ALL, D_MODEL), BF16))
    elif split_out:
        out_specs.append(smp)
        out_shape = [jax.ShapeDtypeStruct((N_PROMPT, D_MODEL), F32),
                     jax.ShapeDtypeStruct((DEC_BATCH, D_MODEL), F32)]
    return pl.pallas_call(
        functools.partial(_resid_kernel, with_next=nxt is not None, split_in=x_sample is not None,
                          split_out=split_out),
        grid=(N_ALL // NORM_TILE,),
        in_specs=in_specs,
        out_specs=out_specs,
        out_shape=out_shape,
        compiler_params=_cparams(("arbitrary",) if split_out else ("parallel",)),
        name="resid_norm",
    )(*args)


def _act_none(x):
    return x


def _act_relu2(x):
    return jnp.square(jnp.maximum(x, 0.0))


def _nt(a, b):
    return lax.dot_general(a, b, (((1,), (1,)), ((), ())), preferred_element_type=F32)


def _mm_kernel(a_ref, w_ref, o_ref, *, nk, act, w_transposed, row_sub):
    w = w_ref[...].astype(BF16)
    if nk > 1:
        @pl.when(pl.program_id(2) == 0)
        def _():
            o_ref[...] = jnp.zeros_like(o_ref)

    for r in range(0, a_ref.shape[0], row_sub):
        rows = slice(r, r + row_sub)
        a = a_ref[rows, :]
        part = _nt(a, w) if w_transposed else jnp.dot(a, w, preferred_element_type=F32)
        if nk == 1:
            o_ref[rows, :] = act(part).astype(o_ref.dtype)
        else:
            o_ref[rows, :] += part


def _matmul(a, w, l, *, n_out, col_off=0, tm=MM_TM, tn=256, tk=None, row_sub=MM_ROW_SUB,
            act=_act_none, out_dtype=F32, w_transposed=False, single_buffer_a=False, name):
    k_dim = a.shape[1]
    tk = k_dim if tk is None else tk
    nk = k_dim // tk
    assert nk == 1 or (act is _act_none and out_dtype == F32)
    if single_buffer_a:
        a_spec = pl.BlockSpec((tm, tk), lambda i, j, k: (i, k), pipeline_mode=pl.Buffered(1))
    else:
        a_spec = pl.BlockSpec((tm, tk), lambda i, j, k: (i, k))
    if w_transposed:
        assert col_off % 8 == 0 and tn % 8 == 0 and tk % LANES == 0
        w_spec = pl.BlockSpec((None, pl.Element(tn), pl.Element(tk)),
                              lambda i, j, k: (l, pl.multiple_of(col_off + j * tn, 8),
                                               pl.multiple_of(k * tk, LANES)))
    else:
        w_spec = pl.BlockSpec((None, tk, tn), lambda i, j, k: (l, k, col_off // tn + j))
    return pl.pallas_call(
        functools.partial(_mm_kernel, nk=nk, act=act, w_transposed=w_transposed, row_sub=row_sub),
        grid=(N_ALL // tm, n_out // tn, nk),
        in_specs=[a_spec, w_spec],
        out_specs=pl.BlockSpec((tm, tn), lambda i, j, k: (i, j)),
        out_shape=jax.ShapeDtypeStruct((N_ALL, n_out), out_dtype),
        compiler_params=_cparams(("parallel", "parallel", "arbitrary")),
        name=name,
    )(a, w)


def _merge_kernel(ya_ref, yb_ref, yc_ref, wa_ref, wb_ref, wc_ref, ga_ref, gb_ref, gc_ref, o_ref):
    wa = wa_ref[...].astype(BF16)
    wb = wb_ref[...].astype(BF16)
    wc = wc_ref[...].astype(BF16)
    sub = MM_TM_HALF // 2
    for r in range(0, MM_TM_HALF, sub):
        rows = slice(r, r + sub)
        merged = ga_ref[rows, :] * jnp.dot(ya_ref[rows, :], wa, preferred_element_type=F32)
        merged = merged + gb_ref[rows, :] * jnp.dot(yb_ref[rows, :], wb, preferred_element_type=F32)
        merged = merged + gc_ref[rows, :] * jnp.dot(yc_ref[rows, :], wc, preferred_element_type=F32)
        o_ref[rows, :] = merged.astype(BF16)


def _merge(ya, yb, yc, w_branch, gates, l):
    tn = 256
    tm = MM_TM_HALF
    nb = D_MODEL // tn
    y_spec = pl.BlockSpec((tm, BRANCH_WIDTH), lambda i, j: (i, 0))

    def w_spec(b):
        return pl.BlockSpec((None, None, BRANCH_WIDTH, tn), lambda i, j: (l, b, 0, j))

    def g_spec(b):
        return pl.BlockSpec((tm, tn), lambda i, j: (i, b * nb + j))

    return pl.pallas_call(
        _merge_kernel,
        grid=(N_ALL // tm, nb),
        in_specs=[y_spec, y_spec, y_spec, w_spec(0), w_spec(1), w_spec(2),
                  g_spec(0), g_spec(1), g_spec(2)],
        out_specs=pl.BlockSpec((tm, tn), lambda i, j: (i, j)),
        out_shape=jax.ShapeDtypeStruct((N_ALL, D_MODEL), BF16),
        compiler_params=_cparams(("parallel", "parallel")),
        name="branch_merge",
    )(ya, yb, yc, w_branch, w_branch, w_branch, gates, gates, gates)


def _sgu_uv(zu_ref, zv_ref, lg_ref, lb_ref):
    u = jax.nn.gelu(zu_ref[...], approximate=True)
    v = jax.nn.gelu(zv_ref[...], approximate=True)
    mu = jnp.mean(v, axis=-1, keepdims=True)
    vc = v - mu
    v = vc * lax.rsqrt(jnp.mean(vc * vc, axis=-1, keepdims=True) + EPS) * lg_ref[...] + lb_ref[...]
    return u, v


def _sgu_prompt_chunk(zu_ref, zv_ref, lg_ref, lb_ref, ws_ref, bst_ref, y_ref):
    u, v = _sgu_uv(zu_ref, zv_ref, lg_ref, lb_ref)
    n_idx = lax.broadcasted_iota(jnp.int32, (SGU_CHUNK, SGU_CHUNK), 0)
    m_idx = lax.broadcasted_iota(jnp.int32, (SGU_CHUNK, SGU_CHUNK), 1)
    causal = m_idx <= n_idx
    for g in range(SGU_GROUPS):
        sl = slice(g * LANES, (g + 1) * LANES)
        wm = jnp.where(causal, ws_ref[g], 0.0).astype(BF16)
        s = jnp.dot(wm, v[:, sl].astype(BF16), preferred_element_type=F32) + bst_ref[:, g:g + 1]
        y_ref[:, sl] = (u[:, sl] * s).astype(BF16)


def _sgu_sample_kernel(zu_ref, zv_ref, lg_ref, lb_ref, ws_ref, bst_ref, ya_any, y_ref, vs_ref):
    del ya_any
    u, v = _sgu_uv(zu_ref, zv_ref, lg_ref, lb_ref)
    vs_ref[...] = v
    for g in range(SGU_GROUPS):
        sl = slice(g * LANES, (g + 1) * LANES)
        s = v[:, sl] * ws_ref[g][0:1, 0:1] + bst_ref[0:1, g:g + 1]
        y_ref[:, sl] = (u[:, sl] * s).astype(BF16)


def _sgu_sample(z, y_a, ln_g, ln_b, w_s, b_s_t, l):
    blk = N_PROMPT // ROW_TILE
    vec = pl.BlockSpec((1, BRANCH_WIDTH), lambda i: (0, 0))
    tile = pl.BlockSpec((ROW_TILE, BRANCH_WIDTH), lambda i: (blk, 0))
    return pl.pallas_call(
        _sgu_sample_kernel,
        grid=(1,),
        in_specs=[
            tile,
            pl.BlockSpec((ROW_TILE, BRANCH_WIDTH), lambda i: (blk, 1)),
            vec, vec,
            pl.BlockSpec((None, SGU_GROUPS, SGU_CHUNK, SGU_CHUNK), lambda i: (l, 0, 0, 0)),
            pl.BlockSpec((None, SGU_CHUNK, SGU_GROUPS), lambda i: (l, 0, 0)),
            pl.BlockSpec(memory_space=pl.ANY),
        ],
        out_specs=[tile, pl.BlockSpec((DEC_BATCH, BRANCH_WIDTH), lambda i: (0, 0))],
        out_shape=[
            jax.ShapeDtypeStruct((N_ALL, BRANCH_WIDTH), BF16),
            jax.ShapeDtypeStruct((DEC_BATCH, BRANCH_WIDTH), F32),
        ],
        input_output_aliases={6: 0},
        compiler_params=_cparams(("arbitrary",)),
        name="sgu_sample",
    )(z, z, ln_g[l].reshape(1, BRANCH_WIDTH), ln_b[l].reshape(1, BRANCH_WIDTH), w_s, b_s_t, y_a)


def _hgrn_lower_bound(lb_ref, l):
    raw = lb_ref[...]
    e = jnp.exp(raw - jnp.max(raw, axis=0, keepdims=True))
    p = e / jnp.sum(e, axis=0, keepdims=True)
    acc = p[0:1, :]
    for j in range(1, l + 1):
        acc = acc + p[j:j + 1, :]
    return jnp.maximum(acc - p[0:1, :], 0.0)


def _hgrn_prep(zq, zf, lb_ref, l):
    e = jnp.exp(-jnp.abs(zf))
    ope = 1.0 + e
    sig_neg = jnp.where(zf >= 0.0, e, 1.0) / ope
    log_sig = jnp.minimum(zf, 0.0) - jnp.log(ope)
    if l == 0:
        return _silu(zq), sig_neg, log_sig
    lb = _hgrn_lower_bound(lb_ref, l)
    a = jnp.log(lb)
    c = jnp.log1p(-lb) + log_sig
    log_f = jnp.maximum(a, c) + jnp.log(1.0 + jnp.exp(-jnp.abs(a - c)))
    return _silu(zq), (1.0 - lb) * sig_neg, log_f


def _gla_prep(zq, zk, glr, wup_ref, bup_ref):
    x = jnp.dot(glr[:, :GLA_RANK].astype(BF16), wup_ref[...].astype(BF16),
                preferred_element_type=F32) + bup_ref[...]
    log_a = _log_sigmoid(x) / GLA_TAU
    return zq, zk * (GLA_DK ** -0.5), log_a


def _lane_bcast_col(row):
    return jnp.broadcast_to(row, (LANES, LANES)).T


def _head_out(o, g_row, gate):
    o = o * lax.rsqrt(jnp.mean(o * o, axis=-1, keepdims=True) + EPS) * g_row
    return o * _silu(gate)


class _RecRefs(NamedTuple):
    gnorm: object
    y: object
    s_out: object
    s: object
    s_prev: object
    q: object
    b: object
    o_intra: object


def _rec_scratch(heads, dk, dv):
    return [
        pltpu.VMEM((heads, dk, dv), F32),
        pltpu.VMEM((heads, dk, dv), F32),
        pltpu.VMEM((REC_CHUNK, heads * dk), F32),
        pltpu.VMEM((REC_CHUNK, heads * dk), F32),
        pltpu.VMEM((heads, REC_CHUNK, dv), F32),
    ]


def _finish_head(h, o_intra, q_dec, k_dec, b_last, vh, gate, s_old, refs, dv):
    vs = slice(h * dv, (h + 1) * dv)
    o = o_intra + jnp.dot(q_dec.astype(BF16), s_old.astype(BF16), preferred_element_type=F32)
    refs.y[:, vs] = _head_out(o, refs.gnorm[...], gate[:, vs]).astype(BF16)
    upd = jnp.dot(k_dec.T.astype(BF16), vh, preferred_element_type=F32)
    decay = _lane_bcast_col(jnp.exp(b_last))
    for j in range(dv // LANES):
        ls = slice(j * LANES, (j + 1) * LANES)
        refs.s[h, :, ls] = decay * s_old[:, ls] + upd[:, ls]


def _chunk_start(c, refs):
    @pl.when(c == 0)
    def _():
        refs.s[...] = jnp.zeros_like(refs.s)


def _chunk_main(q, k, v, log_f, gate, refs, *, heads, dk, dv):
    t_idx = lax.broadcasted_iota(jnp.int32, (REC_CHUNK, REC_CHUNK), 0)
    s_idx = lax.broadcasted_iota(jnp.int32, (REC_CHUNK, REC_CHUNK), 1)
    tril = jnp.where(s_idx <= t_idx, 1.0, 0.0).astype(F32)
    b_all = jnp.dot(tril, log_f, precision=lax.Precision.HIGHEST, preferred_element_type=F32)
    v16 = v.astype(BF16)

    mid = HALF // 2 - 1

    def drop(lo, hi):
        return b_all[lo:lo + 1] - b_all[hi:hi + 1]

    worst = jnp.maximum(jnp.maximum(drop(0, mid), drop(mid, HALF - 1)),
                        jnp.maximum(drop(HALF, HALF + mid), drop(HALF + mid, REC_CHUNK - 1)))
    unsafe = jnp.max(worst) > MAX_SPLIT_EXPONENT

    causal_half = (lax.broadcasted_iota(jnp.int32, (HALF, HALF), 1)
                   <= lax.broadcasted_iota(jnp.int32, (HALF, HALF), 0))
    for h in range(heads):
        ks = slice(h * dk, (h + 1) * dk)
        qh, kh, bh = q[:, ks], k[:, ks], b_all[:, ks]
        vh = v16[:, h * dv:(h + 1) * dv]
        b_a, b_b = bh[:HALF], bh[HALF:]
        v_a, v_b = vh[:HALF], vh[HALF:]
        r_a = b_a[mid:mid + 1]
        r_b = b_b[mid:mid + 1]
        r_m = b_a[HALF - 1:HALF]
        b_last = b_b[HALF - 1:HALF]
        q_a = qh[:HALF] * jnp.exp(b_a - r_a)
        q_b = qh[HALF:] * jnp.exp(b_b - r_b)
        k_a = kh[:HALF] * jnp.exp(r_a - b_a)
        k_b = kh[HALF:] * jnp.exp(r_b - b_b)
        att_aa = jnp.where(causal_half, _nt(q_a.astype(BF16), k_a.astype(BF16)), 0.0).astype(BF16)
        att_bb = jnp.where(causal_half, _nt(q_b.astype(BF16), k_b.astype(BF16)), 0.0).astype(BF16)
        att_ba = _nt((q_b * jnp.exp(r_b - r_m)).astype(BF16),
                     (k_a * jnp.exp(r_m - r_a)).astype(BF16)).astype(BF16)
        o_a = jnp.dot(att_aa, v_a, preferred_element_type=F32)
        o_b = (jnp.dot(att_ba, v_a, preferred_element_type=F32)
               + jnp.dot(att_bb, v_b, preferred_element_type=F32))
        q_dec = jnp.concatenate([q_a * jnp.exp(r_a), q_b * jnp.exp(r_b)], axis=0)
        k_dec = jnp.concatenate([k_a * jnp.exp(b_last - r_a), k_b * jnp.exp(b_last - r_b)], axis=0)
        s_old = refs.s[h]
        refs.s_prev[h] = s_old
        _finish_head(h, jnp.concatenate([o_a, o_b], axis=0), q_dec, k_dec, b_last, vh, gate, s_old,
                     refs, dv)
    return unsafe, b_all, v16


def _chunk_redo(unsafe, q, k, v, b_all, v16, gate, refs, *, heads, dk, dv):
    @pl.when(unsafe)
    def _():
        refs.q[...] = q
        refs.b[...] = b_all
        s_pos = lax.broadcasted_iota(jnp.int32, (REC_CHUNK, 1), 0)
        for h in range(heads):
            ks = slice(h * dk, (h + 1) * dk)
            vs = slice(h * dv, (h + 1) * dv)
            qh, kh, bh = q[:, ks], k[:, ks], b_all[:, ks]
            vh = v[:, vs]

            def row(t, carry):
                b_t = refs.b[pl.ds(t, 1), :][:, ks]
                q_t = refs.q[pl.ds(t, 1), :][:, ks]
                p = jnp.exp(jnp.minimum(b_t - bh, 0.0)) * kh * q_t
                w = jnp.where(s_pos <= t, jnp.sum(p, axis=1, keepdims=True), 0.0)
                refs.o_intra[h, pl.ds(t, 1), :] = jnp.sum(w * vh, axis=0, keepdims=True)
                return carry

            lax.fori_loop(0, REC_CHUNK, row, 0)
            b_last = bh[REC_CHUNK - 1:REC_CHUNK]
            _finish_head(h, refs.o_intra[h], qh * jnp.exp(bh), kh * jnp.exp(b_last - bh), b_last,
                         v16[:, vs], gate, refs.s_prev[h], refs, dv)


def _chunk_end(last, refs):
    @pl.when(last)
    def _():
        refs.s_out[...] = refs.s[...]


def _prompt_mixers_kernel(zu_ref, zv_ref, lng_ref, lnb_ref, ws_ref, bst_ref,
                          hq_ref, hf_ref, hi_ref, hg_ref, lb_ref, hgn_ref,
                          gq_ref, gk_ref, gv_ref, gr_ref, glr_ref, wup_ref, bup_ref, glan_ref,
                          ya_ref, yb_ref, shg_ref, yc_ref, sgla_ref, *scratch, l):
    hg = _RecRefs(hgn_ref, yb_ref, shg_ref, *scratch[:5])
    gla = _RecRefs(glan_ref, yc_ref, sgla_ref, *scratch[5:])
    hg_dims = dict(heads=HG_HEADS, dk=HG_DK, dv=HG_DV)
    gla_dims = dict(heads=GLA_HEADS, dk=GLA_DK, dv=GLA_DV)
    c = pl.program_id(1)
    last = c == pl.num_programs(1) - 1
    _chunk_start(c, hg)
    _chunk_start(c, gla)

    _sgu_prompt_chunk(zu_ref, zv_ref, lng_ref, lnb_ref, ws_ref, bst_ref, ya_ref)

    hq, hk, hlf = _hgrn_prep(hq_ref[...], hf_ref[...], lb_ref, l)
    hv, hgate = hi_ref[...], hg_ref[...]
    h_unsafe, h_b, h_v16 = _chunk_main(hq, hk, hv, hlf, hgate, hg, **hg_dims)
    gq, gk, gla_la = _gla_prep(gq_ref[...], gk_ref[...], glr_ref[...], wup_ref, bup_ref)
    gv, ggate = gv_ref[...], gr_ref[...]
    g_unsafe, g_b, g_v16 = _chunk_main(gq, gk, gv, gla_la, ggate, gla, **gla_dims)

    _chunk_redo(h_unsafe, hq, hk, hv, h_b, h_v16, hgate, hg, **hg_dims)
    _chunk_redo(g_unsafe, gq, gk, gv, g_b, g_v16, ggate, gla, **gla_dims)
    _chunk_end(last, hg)
    _chunk_end(last, gla)


def _prompt_mixers(z, z_glr, ln_g, ln_b, w_s, b_s_t, hg_lb, hg_norm_g, w_up, b_up, gla_norm_g, l):
    n_c = SEQ // REC_CHUNK
    w = BRANCH_WIDTH
    kw = GLA_DK_TOTAL
    assert REC_CHUNK == SGU_CHUNK

    def zcol(width, block):
        return pl.BlockSpec((REC_CHUNK, width), lambda b, c: (b * n_c + c, block))

    def const(shape):
        return pl.BlockSpec(shape, lambda b, c: (0,) * len(shape))

    def state_spec(heads, dk, dv):
        return pl.BlockSpec((None, heads, dk, dv), lambda b, c: (b, 0, 0, 0))

    y_spec = pl.BlockSpec((REC_CHUNK, w), lambda b, c: (b * n_c + c, 0))
    in_specs = [
        zcol(w, 0), zcol(w, 1), const((1, w)), const((1, w)),
        pl.BlockSpec((None, SGU_GROUPS, SGU_CHUNK, SGU_CHUNK), lambda b, c: (l, 0, 0, 0)),
        pl.BlockSpec((None, SGU_CHUNK, SGU_GROUPS), lambda b, c: (l, 0, 0)),
        zcol(w, 2), zcol(w, 3), zcol(w, 4), zcol(w, 5), const((DEPTH, w)), const((1, HG_DV)),
        zcol(kw, 6 * w // kw), zcol(kw, 6 * w // kw + 1), zcol(w, 7), zcol(w, 8), zcol(LANES, 0),
        pl.BlockSpec((None, GLA_RANK, kw), lambda b, c: (l, 0, 0)), const((1, kw)), const((1, GLA_DV)),
    ]
    args = [z, z, ln_g[l].reshape(1, w), ln_b[l].reshape(1, w), w_s, b_s_t,
            z, z, z, z, hg_lb, hg_norm_g[l].reshape(1, HG_DV),
            z, z, z, z, z_glr, w_up, b_up[l].reshape(1, kw), gla_norm_g[l].reshape(1, GLA_DV)]
    return pl.pallas_call(
        functools.partial(_prompt_mixers_kernel, l=l),
        grid=(BATCH, n_c),
        in_specs=in_specs,
        out_specs=[y_spec, y_spec, state_spec(HG_HEADS, HG_DK, HG_DV),
                   y_spec, state_spec(GLA_HEADS, GLA_DK, GLA_DV)],
        out_shape=[
            jax.ShapeDtypeStruct((N_ALL, w), BF16),
            jax.ShapeDtypeStruct((N_ALL, w), BF16),
            jax.ShapeDtypeStruct((BATCH, HG_HEADS, HG_DK, HG_DV), F32),
            jax.ShapeDtypeStruct((N_ALL, w), BF16),
            jax.ShapeDtypeStruct((BATCH, GLA_HEADS, GLA_DK, GLA_DV), F32),
        ],
        scratch_shapes=_rec_scratch(HG_HEADS, HG_DK, HG_DV) + _rec_scratch(GLA_HEADS, GLA_DK, GLA_DV),
        compiler_params=_cparams(("parallel", "arbitrary")),
        name="prompt_mixers",
    )(*args)


def _sample_step(q, k, v, log_f, gate, gnorm_ref, s_ref, so_ref, i, *, heads, dk, dv):
    outs = []
    for h in range(heads):
        ks = slice(h * dk, (h + 1) * dk)
        f_col = _lane_bcast_col(jnp.exp(log_f[:, ks]))
        k_col = _lane_bcast_col(k[:, ks])
        q16 = jnp.broadcast_to(q[:, ks], (16, dk)).astype(BF16)
        parts = []
        for j in range(dv // LANES):
            ls = slice(j * LANES, (j + 1) * LANES)
            v_row = v[:, h * dv + j * LANES:h * dv + (j + 1) * LANES]
            s_new = f_col * s_ref[i, h, :, ls] + k_col * v_row
            so_ref[i, h, :, ls] = s_new
            parts.append(jnp.dot(q16, s_new.astype(BF16), preferred_element_type=F32)[0:1])
        o = parts[0] if len(parts) == 1 else jnp.concatenate(parts, axis=1)
        outs.append(_head_out(o, gnorm_ref[...], gate[:, h * dv:(h + 1) * dv]))
    return jnp.concatenate(outs, axis=1)


def _flush_sample_rows(y_ref, yacc_ref):
    @pl.when(pl.program_id(0) == pl.num_programs(0) - 1)
    def _():
        y_ref[...] = yacc_ref[...].astype(BF16)


def _gate_tile(h_ref, w_ref, g_ref):
    w = w_ref[...].astype(BF16)
    for r in range(0, MM_TM, MM_ROW_SUB):
        rows = slice(r, r + MM_ROW_SUB)
        g_ref[rows, :] = _sigmoid(_nt(h_ref[rows, :], w)).astype(BF16)


def _unit_index(step):
    return jnp.minimum(step, DEC_BATCH // SAMPLE_NB - 1)


def _hgrn_sample_kernel(h_ref, w_ref, zq_ref, zf_ref, zi_ref, zg_ref, lb_ref, gnorm_ref, s_ref, *rest, l):
    g_ref, y_ref, so_ref, yacc_ref = rest[-4:]
    _gate_tile(h_ref, w_ref, g_ref)
    base = _unit_index(pl.program_id(0)) * SAMPLE_NB
    q, k, log_f = _hgrn_prep(zq_ref[...], zf_ref[...], lb_ref, l)
    v, gate = zi_ref[...], zg_ref[...]
    for i in range(SAMPLE_NB):
        row = slice(i, i + 1)
        yacc_ref[pl.ds(base + i, 1), :] = _sample_step(
            q[row], k[row], v[row], log_f[row], gate[row], gnorm_ref, s_ref, so_ref, i,
            heads=HG_HEADS, dk=HG_DK, dv=HG_DV)
    _flush_sample_rows(y_ref, yacc_ref)


def _gla_sample_kernel(h_ref, w_ref, zq_ref, zk_ref, zv_ref, zr_ref, glr_ref, wup_ref, bup_ref,
                       gnorm_ref, s_ref, *rest):
    g_ref, y_ref, so_ref, yacc_ref = rest[-4:]
    _gate_tile(h_ref, w_ref, g_ref)
    base = _unit_index(pl.program_id(0)) * SAMPLE_NB
    glr = jnp.concatenate([glr_ref[...], jnp.zeros((16 - SAMPLE_NB, LANES), F32)], axis=0)
    q, k, log_a = _gla_prep(zq_ref[...], zk_ref[...], glr, wup_ref, bup_ref)
    v, gate = zv_ref[...], zr_ref[...]
    for i in range(SAMPLE_NB):
        row = slice(i, i + 1)
        yacc_ref[pl.ds(base + i, 1), :] = _sample_step(
            q[row], k[row], v[row], log_a[row], gate[row], gnorm_ref, s_ref, so_ref, i,
            heads=GLA_HEADS, dk=GLA_DK, dv=GLA_DV)
    _flush_sample_rows(y_ref, yacc_ref)


def _zrow(width, block):
    return pl.BlockSpec((SAMPLE_NB, width), lambda j: (N_PROMPT // SAMPLE_NB + _unit_index(j), block))


def _gate_sample_call(kernel, in_specs, args, h, w_in_t, state, y_all, new_state, gates, row_tile, l,
                      heads, dk, dv, name):
    tn = GATE_TN
    n_steps = N_BRANCH * D_MODEL // tn
    assert n_steps >= DEC_BATCH // SAMPLE_NB
    h_spec = pl.BlockSpec((MM_TM, D_MODEL), lambda j: (row_tile, 0), pipeline_mode=pl.Buffered(1))
    w_spec = pl.BlockSpec((None, pl.Element(tn), pl.Element(D_MODEL)),
                          lambda j: (l, pl.multiple_of(OFF_GATE + j * tn, 8), 0))
    st = pl.BlockSpec((None, SAMPLE_NB, heads, dk, dv), lambda j: (l, _unit_index(j), 0, 0, 0))
    any_spec = pl.BlockSpec(memory_space=pl.ANY)
    in_specs = [h_spec, w_spec] + in_specs + [st, any_spec]
    args = [h, w_in_t] + list(args) + [state, y_all]
    aliases = {len(args) - 1: 1}
    for buf, out_idx in ((new_state, 2), (gates, 0)):
        if buf is not None:
            aliases[len(args)] = out_idx
            in_specs.append(any_spec)
            args.append(buf)
    return pl.pallas_call(
        kernel,
        grid=(n_steps,),
        in_specs=in_specs,
        out_specs=[
            pl.BlockSpec((MM_TM, tn), lambda j: (row_tile, j)),
            pl.BlockSpec((DEC_BATCH, BRANCH_WIDTH), lambda j: (N_PROMPT // DEC_BATCH, 0)),
            st,
        ],
        out_shape=[
            jax.ShapeDtypeStruct((N_ALL, N_BRANCH * D_MODEL), BF16),
            jax.ShapeDtypeStruct((N_ALL, BRANCH_WIDTH), BF16),
            jax.ShapeDtypeStruct((DEPTH, DEC_BATCH, heads, dk, dv), F32),
        ],
        scratch_shapes=[pltpu.VMEM((DEC_BATCH, BRANCH_WIDTH), F32)],
        input_output_aliases=aliases,
        compiler_params=_cparams(("arbitrary",)),
        name=name,
    )(*args)


def _hgrn_sample(h, w_in_t, z, hg_lb, g_norm, state, y_all, new_state, gates, row_tile, l):
    w = BRANCH_WIDTH
    in_specs = [_zrow(w, 2), _zrow(w, 3), _zrow(w, 4), _zrow(w, 5),
                pl.BlockSpec((DEPTH, w), lambda j: (0, 0)),
                pl.BlockSpec((1, HG_DV), lambda j: (0, 0))]
    args = [z, z, z, z, hg_lb, g_norm[l].reshape(1, HG_DV)]
    return _gate_sample_call(functools.partial(_hgrn_sample_kernel, l=l), in_specs, args, h, w_in_t,
                             state, y_all, new_state, gates, row_tile, l,
                             HG_HEADS, HG_DK, HG_DV, "gate_hgrn_sample")


def _gla_sample(h, w_in_t, z, z_glr, w_up, b_up, g_norm, state, y_all, new_state, gates, row_tile, l):
    w = BRANCH_WIDTH
    kw = GLA_DK_TOTAL
    in_specs = [_zrow(kw, 6 * w // kw), _zrow(kw, 6 * w // kw + 1), _zrow(w, 7), _zrow(w, 8),
                _zrow(LANES, 0),
                pl.BlockSpec((None, GLA_RANK, kw), lambda j: (l, 0, 0)),
                pl.BlockSpec((1, kw), lambda j: (0, 0)),
                pl.BlockSpec((1, GLA_DV), lambda j: (0, 0))]
    args = [z, z, z, z, z_glr, w_up, b_up[l].reshape(1, kw), g_norm[l].reshape(1, GLA_DV)]
    return _gate_sample_call(_gla_sample_kernel, in_specs, args, h, w_in_t, state, y_all, new_state,
                             gates, row_tile, l, GLA_HEADS, GLA_DK, GLA_DV, "gate_gla_sample")


def kernel(x_prompt, x_sample, state_hgrn, state_gla, c_prompt, c_sample, w_ada, b_ada, g_pre_mix, g_post_mix, g_pre_mlp, g_post_mlp, w_in, sgu_ln_g, sgu_ln_b, sgu_w_s, sgu_b_s, hg_lb, hg_norm_g, gla_w_up, gla_b_up, gla_norm_g, w_branch, w_out, w_mlp_up, w_mlp_down):
    x = x_prompt.reshape(N_PROMPT, D_MODEL)
    x_s = x_sample.reshape(DEC_BATCH, D_MODEL)
    c_all = jnp.concatenate([c_sample, c_prompt, jnp.zeros((MOD_ROWS - DEC_BATCH - BATCH, D_MODEL), F32)], axis=0)
    mod = _modulation(c_all, w_ada, b_ada)
    w_in_t = jnp.swapaxes(w_in, 1, 2)
    b_s_t = jnp.swapaxes(sgu_b_s, 1, 2)

    hg_p, gla_p, v_rows = [], [], []
    hg_s = gla_s = None
    h = _prenorm(x, x_s, g_pre_mix[0], mod, 0, 0, 1)
    for l in range(DEPTH):
        z = _matmul(h, w_in_t, l, n_out=Z_MAIN, tn=WIDE_TN, w_transposed=True, single_buffer_a=True,
                    name="in_proj")
        z_glr = _matmul(h, w_in_t, l, n_out=LANES, col_off=OFF_GLR, tn=LANES, w_transposed=True,
                        name="in_proj_lowrank")

        y_a, y_b, s_hg_p, y_c, s_gla_p = _prompt_mixers(
            z, z_glr, sgu_ln_g, sgu_ln_b, sgu_w_s, b_s_t, hg_lb, hg_norm_g, gla_w_up, gla_b_up,
            gla_norm_g, l)
        y_a, v_s = _sgu_sample(z, y_a, sgu_ln_g, sgu_ln_b, sgu_w_s, b_s_t, l)
        gates, y_b, hg_s = _hgrn_sample(h, w_in_t, z, hg_lb, hg_norm_g, state_hgrn, y_b, hg_s,
                                        None, 0, l)
        gates, y_c, gla_s = _gla_sample(h, w_in_t, z, z_glr, gla_w_up, gla_b_up, gla_norm_g,
                                        state_gla, y_c, gla_s, gates, 1, l)

        merged = _merge(y_a, y_b, y_c, w_branch, gates, l)
        out = _matmul(merged, w_out, l, n_out=D_MODEL, name="out_proj")
        x, h2 = _resid(x, out, g_post_mix[l], mod, l, 2, nxt=(g_pre_mlp[l], l, 3, 4),
                       x_sample=x_s if l == 0 else None)

        up = _matmul(h2, w_mlp_up, l, n_out=D_FF, tn=WIDE_TN, act=_act_relu2, out_dtype=BF16,
                     name="mlp_up")
        y2 = _matmul(up, w_mlp_down, l, n_out=D_MODEL, tm=MM_TM_HALF, tn=1024, tk=2048,
                     row_sub=MM_TM_HALF // 4, name="mlp_down")
        if l + 1 < DEPTH:
            x, h = _resid(x, y2, g_post_mlp[l], mod, l, 5, nxt=(g_pre_mix[l + 1], l + 1, 0, 1))
        else:
            y_prompt, y_sample = _resid(x, y2, g_post_mlp[l], mod, l, 5, split_out=True)

        hg_p.append(s_hg_p)
        gla_p.append(s_gla_p)
        v_rows.append(v_s.reshape(DEC_BATCH, 1, BRANCH_WIDTH))

    y_prompt = y_prompt.reshape(BATCH, SEQ, D_MODEL)
    y_sample = y_sample.reshape(DEC_BATCH, 1, D_MODEL)
    return (y_prompt, y_sample, jnp.stack(hg_p), jnp.stack(gla_p), hg_s, gla_s, jnp.stack(v_rows))
```
